```python
import math
import jax, jax.numpy as jnp
from jax import lax
import numpy as np

D_MODEL = 1024
BATCH = 16
SEQ = 256
DEPTH = 2
DEC_BATCH = 4
DEC_SEQ = 2048
PAST_LEN = 256

GRID_W = 64
ROPE_BASE = 10000.0
EPS = 1e-6
NEG_INF = -1e30
Q_BLOCK = 128

A_HEADS = 8
A_KV_HEADS = 2
A_GROUP = A_HEADS // A_KV_HEADS
A_HEAD_DIM = 64
A_WINDOW = 128
A_SCALE = A_HEAD_DIM ** -0.5
B_HEADS = 8
B_Q_LORA = 384
B_KV_LORA = 256
B_NOPE = 64
B_ROPE = 32
B_V = 64
MLA_SCALE = (B_NOPE + B_ROPE) ** -0.5
C_HEADS = 4
C_DK = 128
C_DV = 128
C_CONV = 3
C_CHUNK = 64
D_FF = 4 * D_MODEL

N_BRANCH = 3
BRANCH_W = 512
A_Q_W = A_HEADS * A_HEAD_DIM
A_KV_W = A_KV_HEADS * A_HEAD_DIM
C_QK_W = C_HEADS * C_DK
C_V_W = C_HEADS * C_DV
C_QKV_W = 2 * C_QK_W + C_V_W
IN_SPLITS = (A_Q_W, A_KV_W, A_KV_W, B_Q_LORA, B_KV_LORA, B_ROPE, C_QKV_W, 2 * C_HEADS, 2 * C_HEADS, C_V_W, N_BRANCH * D_MODEL)
D_IN = A_Q_W + 2 * A_KV_W + B_Q_LORA + B_KV_LORA + B_ROPE + C_QKV_W + 4 * C_HEADS + C_V_W + N_BRANCH * D_MODEL

kernel_name = 'hybrid_diffusion_prefix_trunk_step'


def rmsnorm(x, g):
    xf = x.astype(jnp.float32)
    y = xf * lax.rsqrt(jnp.mean(xf * xf, axis=-1, keepdims=True) + EPS)
    return (y * g.astype(jnp.float32)).astype(x.dtype)


def l2norm(x):
    xf = x.astype(jnp.float32)
    return (xf * lax.rsqrt(jnp.sum(xf * xf, axis=-1, keepdims=True) + EPS)).astype(x.dtype)


def axial_rope(x):
    n, r = x.shape[1], x.shape[-1]
    rows = n // GRID_W
    row = jnp.repeat(jnp.arange(rows), GRID_W)
    col = jnp.tile(jnp.arange(GRID_W), rows)
    half = r // 2
    quarter = half // 2
    inv_freq = jnp.power(ROPE_BASE, -jnp.arange(quarter, dtype=jnp.float32) / quarter)

    def rotate(xa, pos):
        ang = pos.astype(jnp.float32)[:, None] * inv_freq[None, :]
        cos = jnp.cos(ang)[:, None, :]
        sin = jnp.sin(ang)[:, None, :]
        x1 = xa[..., :quarter].astype(jnp.float32)
        x2 = xa[..., quarter:].astype(jnp.float32)
        return jnp.concatenate([x1 * cos - x2 * sin, x1 * sin + x2 * cos], axis=-1)

    out = jnp.concatenate([rotate(x[..., :half], row), rotate(x[..., half:], col)], axis=-1)
    return out.astype(x.dtype)


def softmax_with_sink(s, sink):
    m = jnp.maximum(jnp.max(s, axis=-1, keepdims=True), sink)
    e = jnp.exp(s - m)
    return e / (jnp.sum(e, axis=-1, keepdims=True) + jnp.exp(sink - m))


def blocked_attention(q, k, v, scale, sink=None):
    bsz, nq, hk, g, dk = q.shape
    nb = nq // Q_BLOCK
    qb = jnp.moveaxis(q.reshape(bsz, nb, Q_BLOCK, hk, g, dk), 1, 0)

    def one_block(qi):
        s = jnp.einsum('bqhgd,bkhd->bhgqk', qi, k).astype(jnp.float32) * scale
        if sink is None:
            p = jax.nn.softmax(s, axis=-1)
        else:
            p = softmax_with_sink(s, sink.astype(jnp.float32).reshape(1, hk, g, 1, 1))
        return jnp.einsum('bhgqk,bkhd->bqhgd', p.astype(v.dtype), v)

    out = lax.map(one_block, qb)
    return jnp.moveaxis(out, 0, 1).reshape(bsz, nq, hk, g, v.shape[-1])


def window_attention(q, k, v, k_ctx, v_ctx, sink, scale):
    bsz, n, hk, g, d = q.shape
    nb = n // Q_BLOCK
    qb = q.reshape(bsz, nb, Q_BLOCK, hk, g, d)
    pad = ((0, 0), (Q_BLOCK, Q_BLOCK), (0, 0), (0, 0))
    kp = jnp.pad(k, pad).reshape(bsz, nb + 2, Q_BLOCK, hk, d)
    vp = jnp.pad(v, pad).reshape(bsz, nb + 2, Q_BLOCK, hk, v.shape[-1])
    kw = jnp.concatenate([kp[:, :-2], kp[:, 1:-1], kp[:, 2:]], axis=2)
    vw = jnp.concatenate([vp[:, :-2], vp[:, 1:-1], vp[:, 2:]], axis=2)
    qpos = jnp.arange(n).reshape(nb, Q_BLOCK)
    kpos = (jnp.arange(nb) - 1)[:, None] * Q_BLOCK + jnp.arange(3 * Q_BLOCK)[None, :]
    valid = ((jnp.abs(qpos[:, :, None] - kpos[:, None, :]) <= A_WINDOW)
             & (kpos >= 0)[:, None, :] & (kpos < n)[:, None, :])
    s_loc = jnp.einsum('bnqhgd,bnkhd->bnhgqk', qb, kw).astype(jnp.float32) * scale
    s_loc = jnp.where(valid[None, :, None, None], s_loc, NEG_INF)
    s_ctx = jnp.einsum('bnqhgd,bkhd->bnhgqk', qb, k_ctx).astype(jnp.float32) * scale
    p = softmax_with_sink(jnp.concatenate([s_loc, s_ctx], axis=-1),
                          sink.astype(jnp.float32).reshape(1, 1, hk, g, 1, 1))
    p_loc = p[..., :3 * Q_BLOCK].astype(v.dtype)
    p_ctx = p[..., 3 * Q_BLOCK:].astype(v.dtype)
    o = (jnp.einsum('bnhgqk,bnkhd->bnqhgd', p_loc, vw)
         + jnp.einsum('bnhgqk,bkhd->bnqhgd', p_ctx, v_ctx))
    return o.reshape(bsz, n, hk, g, v.shape[-1])


def centred_dwconv(x, w):
    pad = C_CONV // 2
    return lax.conv_general_dilated(x, w[:, None, :], window_strides=(1,), padding=[(pad, pad)],
                                    dimension_numbers=('NWC', 'WIO', 'NWC'),
                                    feature_group_count=x.shape[-1])


def gated_delta_chunked(q, k, v, g, beta, s0):
    f32 = jnp.float32
    bsz, n, h, _ = q.shape
    dv = v.shape[-1]
    nc = n // C_CHUNK

    def to_chunks(t):
        t = jnp.moveaxis(t.astype(f32), 2, 1)
        return t.reshape(bsz, h, nc, C_CHUNK, *t.shape[3:])

    qc, kc, vc, gl, bc = (to_chunks(t) for t in (q, k, v, g, beta))
    gc = jnp.cumsum(gl, axis=-1)
    ar = jnp.arange(C_CHUNK)
    incl = ar[:, None] >= ar[None, :]
    strict = ar[:, None] > ar[None, :]
    decay = jnp.where(incl, jnp.exp(jnp.where(incl, gc[..., :, None] - gc[..., None, :], 0.0)), 0.0)
    kb = kc * bc[..., None]
    a_mat = jnp.where(strict, jnp.einsum('bhnid,bhnjd->bhnij', kb, kc) * decay, 0.0)
    eye = jnp.eye(C_CHUNK, dtype=f32)
    t_mat = lax.linalg.triangular_solve(eye + a_mat, jnp.broadcast_to(eye, a_mat.shape),
                                        left_side=True, lower=True, unit_diagonal=True)
    u = t_mat @ (vc * bc[..., None])
    w = t_mat @ (kb * jnp.exp(gc)[..., None])
    qk = jnp.where(incl, jnp.einsum('bhnid,bhnjd->bhnij', qc, kc) * decay, 0.0)
    q_dec = qc * jnp.exp(gc)[..., None]
    k_dec = kc * jnp.exp(gc[..., -1:] - gc)[..., None]
    g_end = jnp.exp(gc[..., -1])
    xs = tuple(jnp.moveaxis(t, 2, 0) for t in (u, w, qk, q_dec, k_dec, g_end))

    def step(s, xi):
        u_i, w_i, qk_i, qd_i, kd_i, ge_i = xi
        v_new = u_i - jnp.einsum('bhcd,bhde->bhce', w_i, s)
        o_i = jnp.einsum('bhcd,bhde->bhce', qd_i, s) + jnp.einsum('bhcj,bhje->bhce', qk_i, v_new)
        s = s * ge_i[..., None, None] + jnp.einsum('bhcd,bhce->bhde', kd_i, v_new)
        return s, o_i

    s_fin, o = lax.scan(step, s0.astype(f32), xs)
    o = jnp.moveaxis(o, 0, 2).reshape(bsz, h, n, dv)
    return jnp.moveaxis(o, 1, 2).astype(v.dtype), s_fin.astype(v.dtype)


def bidir_gated_delta(q, k, v, g, beta, s0_f, s0_b):
    o_f, s_f = gated_delta_chunked(q, k, v, g[:, :, 0], beta[:, :, 0], s0_f)
    fl = lambda t: jnp.flip(t, axis=1)
    o_b, s_b = gated_delta_chunked(fl(q), fl(k), fl(v), fl(g[:, :, 1]), fl(beta[:, :, 1]), s0_b)
    return o_f + fl(o_b), jnp.stack([s_f, s_b], axis=1)


def project_in(h, w_in):
    offs = np.cumsum(IN_SPLITS)[:-1].tolist()
    return jnp.split(h @ w_in, offs, axis=-1)


def mla_query(b_cq, p):
    bsz, n = b_cq.shape[:2]
    q = (rmsnorm(b_cq, p['b_gq']) @ p['b_w_uq']).reshape(bsz, n, B_HEADS, B_NOPE + B_ROPE)
    return q[..., :B_NOPE], q[..., B_NOPE:]


def mla_keys_values(ckv, k_rope, p):
    bsz, n = ckv.shape[:2]
    kv = (ckv @ p['b_w_ukv']).reshape(bsz, n, B_HEADS, B_NOPE + B_V)
    k = jnp.concatenate([kv[..., :B_NOPE], jnp.broadcast_to(k_rope[:, :, None, :], (bsz, n, B_HEADS, B_ROPE))], axis=-1)
    return k, kv[..., B_NOPE:]


def delta_branch(c_qkv, c_a, c_b, c_z, p, s0):
    bsz, n = c_qkv.shape[:2]
    qkv = jax.nn.silu(centred_dwconv(c_qkv, p['c_conv_w']))
    q, k, v = jnp.split(qkv, [C_QK_W, 2 * C_QK_W], axis=-1)
    q = l2norm(q.reshape(bsz, n, C_HEADS, C_DK)) * (C_DK ** -0.5)
    k = l2norm(k.reshape(bsz, n, C_HEADS, C_DK))
    v = v.reshape(bsz, n, C_HEADS, C_DV)
    g = -jnp.exp(p['c_a_log'].astype(jnp.float32)) * jax.nn.softplus(
        c_a.astype(jnp.float32).reshape(bsz, n, 2, C_HEADS) + p['c_dt_bias'].astype(jnp.float32))
    beta = jax.nn.sigmoid(c_b.astype(jnp.float32).reshape(bsz, n, 2, C_HEADS))
    o, s_fin = bidir_gated_delta(q, k, v, g, beta, s0[:, 0], s0[:, 1])
    o = rmsnorm(o, p['c_onorm_g']) * jax.nn.silu(c_z.reshape(bsz, n, C_HEADS, C_DV))
    return o.reshape(bsz, n, C_V_W), s_fin


def merge_branches(oa, ob, oc, gates, p):
    bsz, n = oa.shape[:2]
    br = jnp.stack([oa, ob, oc], axis=2)
    y = jnp.einsum('bnjc,jcd->bnjd', br, p['w_branch'])
    gt = jax.nn.sigmoid(gates.reshape(bsz, n, N_BRANCH, D_MODEL))
    return jnp.einsum('bnjd,de->bne', gt * y, p['w_out'])


def mixer_context(h, p):
    bsz, n = h.shape[:2]
    a_q, a_k, a_v, b_cq, b_ckv, b_kr, c_qkv, c_a, c_b, c_z, gates = project_in(h, p['w_in'])
    ka = a_k.reshape(bsz, n, A_KV_HEADS, A_HEAD_DIM)
    va = a_v.reshape(bsz, n, A_KV_HEADS, A_HEAD_DIM)
    oa = blocked_attention(a_q.reshape(bsz, n, A_KV_HEADS, A_GROUP, A_HEAD_DIM), ka, va, A_SCALE,
                           p['a_sink'].reshape(A_KV_HEADS, A_GROUP))
    q_nope, q_rope = mla_query(b_cq, p)
    ckv = rmsnorm(b_ckv, p['b_gkv'])
    kb, vb = mla_keys_values(ckv, b_kr, p)
    ob = blocked_attention(jnp.concatenate([q_nope, q_rope], axis=-1)[:, :, :, None], kb, vb, MLA_SCALE)
    s0 = jnp.zeros((bsz, 2, C_HEADS, C_DK, C_DV), h.dtype)
    oc, s_fin = delta_branch(c_qkv, c_a, c_b, c_z, p, s0)
    y = merge_branches(oa.reshape(bsz, n, BRANCH_W), ob.reshape(bsz, n, BRANCH_W), oc, gates, p)
    return y, (ka, va, ckv, b_kr, s_fin)


def mixer_latent(h, p, ka_ctx, va_ctx, ckv_ctx, kr_ctx, s_ctx):
    bsz, n = h.shape[:2]
    a_q, a_k, a_v, b_cq, b_ckv, b_kr, c_qkv, c_a, c_b, c_z, gates = project_in(h, p['w_in'])
    qa = axial_rope(a_q.reshape(bsz, n, A_HEADS, A_HEAD_DIM)).reshape(bsz, n, A_KV_HEADS, A_GROUP, A_HEAD_DIM)
    ka = axial_rope(a_k.reshape(bsz, n, A_KV_HEADS, A_HEAD_DIM))
    va = a_v.reshape(bsz, n, A_KV_HEADS, A_HEAD_DIM)
    oa = window_attention(qa, ka, va, ka_ctx, va_ctx, p['a_sink'].reshape(A_KV_HEADS, A_GROUP), A_SCALE)
    q_nope, q_rope = mla_query(b_cq, p)
    qb = jnp.concatenate([q_nope, axial_rope(q_rope)], axis=-1)
    kr = axial_rope(b_kr[:, :, None, :])[:, :, 0]
    kb_lat, vb_lat = mla_keys_values(rmsnorm(b_ckv, p['b_gkv']), kr, p)
    kb_ctx, vb_ctx = mla_keys_values(ckv_ctx, kr_ctx, p)
    ob = blocked_attention(qb[:, :, :, None], jnp.concatenate([kb_lat, kb_ctx], axis=1),
                           jnp.concatenate([vb_lat, vb_ctx], axis=1), MLA_SCALE)
    oc, _ = delta_branch(c_qkv, c_a, c_b, c_z, p, s_ctx)
    y = merge_branches(oa.reshape(bsz, n, BRANCH_W), ob.reshape(bsz, n, BRANCH_W), oc, gates, p)
    return y, None


def adaln(cond, w, bias):
    return (jax.nn.silu(cond) @ w + bias)[:, None, :]


def residual_block(x, mod, p, mixer):
    sh1, sc1, g1, sh2, sc2, g2 = jnp.split(mod, 6, axis=-1)
    y, extra = mixer(rmsnorm(x, p['norm1_g']) * (1 + sc1) + sh1)
    x = x + g1 * y
    h = rmsnorm(x, p['norm2_g']) * (1 + sc2) + sh2
    x = x + g2 * (jnp.square(jax.nn.relu(h @ p['w_ff1'])) @ p['w_ff2'])
    return x, extra


def setup_inputs(seed: int = 0) -> dict:
    key = jax.random.key(seed)
    ks = jax.random.split(key, 32)
    nrm = lambda k, shape, s: jax.random.normal(k, shape, jnp.float32) * s
    dt = jnp.exp(jax.random.uniform(ks[20], (DEPTH, 2, C_HEADS), jnp.float32, math.log(1e-3), math.log(1e-1)))
    return {
        'x_prompt': nrm(ks[0], (BATCH, SEQ, D_MODEL), 1.0),
        'x_sample': nrm(ks[1], (DEC_BATCH, DEC_SEQ, D_MODEL), 1.0),
        'cache_a_k': nrm(ks[2], (DEC_BATCH, DEPTH, PAST_LEN, A_KV_HEADS, A_HEAD_DIM), 1.0),
        'cache_a_v': nrm(ks[3], (DEC_BATCH, DEPTH, PAST_LEN, A_KV_HEADS, A_HEAD_DIM), 1.0),
        'cache_b_ckv': nrm(ks[4], (DEC_BATCH, DEPTH, PAST_LEN, B_KV_LORA), 1.0),
        'cache_b_krope': nrm(ks[5], (DEC_BATCH, DEPTH, PAST_LEN, B_ROPE), 1.0),
        'state_c': nrm(ks[6], (DEC_BATCH, DEPTH, 2, C_HEADS, C_DK, C_DV), 0.05),
        'c': nrm(ks[7], (DEC_BATCH, D_MODEL), 1.0),
        'c_ctx': nrm(ks[8], (D_MODEL,), 1.0),
        'norm1_g': 1.0 + nrm(ks[9], (DEPTH, D_MODEL), 0.02),
        'norm2_g': 1.0 + nrm(ks[10], (DEPTH, D_MODEL), 0.02),
        'w_ada': nrm(ks[11], (DEPTH, D_MODEL, 6 * D_MODEL), 0.5 * D_MODEL ** -0.5),
        'b_ada': nrm(ks[12], (DEPTH, 6 * D_MODEL), 0.02),
        'w_in': nrm(ks[13], (DEPTH, D_MODEL, D_IN), D_MODEL ** -0.5),
        'a_sink': nrm(ks[14], (DEPTH, A_HEADS), 0.5),
        'b_gq': 1.0 + nrm(ks[15], (DEPTH, B_Q_LORA), 0.02),
        'b_w_uq': nrm(ks[16], (DEPTH, B_Q_LORA, B_HEADS * (B_NOPE + B_ROPE)), B_Q_LORA ** -0.5),
        'b_gkv': 1.0 + nrm(ks[17], (DEPTH, B_KV_LORA), 0.02),
        'b_w_ukv': nrm(ks[18], (DEPTH, B_KV_LORA, B_HEADS * (B_NOPE + B_V)), B_KV_LORA ** -0.5),
        'c_conv_w': nrm(ks[19], (DEPTH, C_CONV, C_QKV_W), C_CONV ** -0.5),
        'c_a_log': jnp.log(jax.random.uniform(ks[21], (DEPTH, 2, C_HEADS), jnp.float32, 1.0, 16.0)),
        'c_dt_bias': dt + jnp.log(-jnp.expm1(-dt)),
        'c_onorm_g': 1.0 + nrm(ks[22], (DEPTH, C_DV), 0.02),
        'w_branch': nrm(ks[23], (DEPTH, N_BRANCH, BRANCH_W, D_MODEL), BRANCH_W ** -0.5),
        'w_out': nrm(ks[24], (DEPTH, D_MODEL, D_MODEL), D_MODEL ** -0.5),
        'w_ff1': nrm(ks[25], (DEPTH, D_MODEL, D_FF), D_MODEL ** -0.5),
        'w_ff2': nrm(ks[26], (DEPTH, D_FF, D_MODEL), D_FF ** -0.5),
        'final_g': 1.0 + nrm(ks[27], (D_MODEL,), 0.02),
    }


def reference(x_prompt, x_sample, cache_a_k, cache_a_v, cache_b_ckv, cache_b_krope, state_c, c, c_ctx,
              norm1_g, norm2_g, w_ada, b_ada, w_in, a_sink, b_gq, b_w_uq, b_gkv, b_w_ukv, c_conv_w,
              c_a_log, c_dt_bias, c_onorm_g, w_branch, w_out, w_ff1, w_ff2, final_g):
    xp, xs = x_prompt, x_sample
    ak_l, av_l, ckv_l, kr_l, sc_l = [], [], [], [], []
    for l in range(DEPTH):
        p = {'norm1_g': norm1_g[l], 'norm2_g': norm2_g[l], 'w_in': w_in[l], 'a_sink': a_sink[l],
             'b_gq': b_gq[l], 'b_w_uq': b_w_uq[l], 'b_gkv': b_gkv[l], 'b_w_ukv': b_w_ukv[l],
             'c_conv_w': c_conv_w[l], 'c_a_log': c_a_log[l], 'c_dt_bias': c_dt_bias[l],
             'c_onorm_g': c_onorm_g[l], 'w_branch': w_branch[l], 'w_out': w_out[l],
             'w_ff1': w_ff1[l], 'w_ff2': w_ff2[l]}
        mod_ctx = adaln(c_ctx[None, :], w_ada[l], b_ada[l])
        mod_lat = adaln(c, w_ada[l], b_ada[l])
        xp, (ak, av, ckv, kr, sc) = residual_block(xp, mod_ctx, p, lambda h: mixer_context(h, p))
        xs, _ = residual_block(xs, mod_lat, p, lambda h: mixer_latent(
            h, p, cache_a_k[:, l], cache_a_v[:, l], cache_b_ckv[:, l], cache_b_krope[:, l], state_c[:, l]))
        ak_l.append(ak)
        av_l.append(av)
        ckv_l.append(ckv)
        kr_l.append(kr)
        sc_l.append(sc)
    y_prompt = rmsnorm(xp, final_g)
    y_sample = rmsnorm(xs, final_g)
    new_a_k = jnp.stack(ak_l, axis=1)
    new_a_v = jnp.stack(av_l, axis=1)
    new_b_ckv = jnp.stack(ckv_l, axis=1)
    new_b_krope = jnp.stack(kr_l, axis=1)
    new_c_state = jnp.stack(sc_l, axis=1)
    return (y_prompt, y_sample, new_a_k, new_a_v, new_b_ckv, new_b_krope, new_c_state)
```

```python
import functools
import math

import jax
import jax.numpy as jnp
from jax import lax
from jax.experimental import pallas as pl
from jax.experimental.pallas import tpu as pltpu

F32 = jnp.float32
BF16 = jnp.bfloat16

D_MODEL = 1024
BATCH = 16
SEQ = 256
DEPTH = 2
DEC_BATCH = 4
DEC_SEQ = 2048
PAST_LEN = 256
GRID_W = 64
ROPE_BASE = 10000.0
EPS = 1e-6
NEG_INF = -1e30

A_HEADS = 8
A_KV_HEADS = 2
A_GROUP = 4
A_HEAD_DIM = 64
A_WINDOW = 128
A_SCALE = A_HEAD_DIM ** -0.5
B_HEADS = 8
B_Q_LORA = 384
B_KV_LORA = 256
B_NOPE = 64
B_ROPE = 32
B_V = 64
MLA_SCALE = (B_NOPE + B_ROPE) ** -0.5
C_HEADS = 4
C_DK = 128
C_DV = 128
D_FF = 4 * D_MODEL
BRANCH_W = 512

LANES = 128

T_CTX = BATCH * SEQ
T_LAT = DEC_BATCH * DEC_SEQ
T_ALL = T_CTX + T_LAT
TM = 256
N_CTX_TILES = T_CTX // TM
LAT_TILES_PER_SEQ = DEC_SEQ // TM
N_TILES = T_ALL // TM
N_GROUPS = 8
GDN_CHUNK = 128

O_AQ, O_AK, O_AV, O_BCQ, O_BCKV, O_SMALL, O_CQKV, O_CZ, O_GATES, W_IN_COLS = (
    0, 512, 640, 768, 1152, 1408, 1536, 3072, 3584, 6656)
SM_KR, SM_A, SM_B = 0, 32, 40

VMEM_LIMIT = 56 * 1024 * 1024


def _cparams(*sem):
    return pltpu.CompilerParams(dimension_semantics=sem, vmem_limit_bytes=VMEM_LIMIT)


def _resident(shape):
    nd = len(shape)
    return pl.BlockSpec(shape, lambda *_: (0,) * nd, pipeline_mode=pl.Buffered(1))


def _group_of_tile(i):
    return jnp.where(i < N_CTX_TILES, 0, 1 + (i - N_CTX_TILES) // LAT_TILES_PER_SEQ)


def _rope_block_of_tile(i):
    return jnp.where(i < N_CTX_TILES, 0, 1 + (i - N_CTX_TILES) % LAT_TILES_PER_SEQ)


def _rms(x):
    return x * lax.rsqrt(jnp.mean(x * x, axis=-1, keepdims=True) + EPS)


def _sigmoid(x):
    return 1.0 / (1.0 + jnp.exp(-x))


def _silu(x):
    return x * _sigmoid(x)


def _softplus(x):
    return jnp.maximum(x, 0.0) + jnp.log(1.0 + jnp.exp(-jnp.abs(x)))


def _dot(a, b):
    return jnp.dot(a.astype(BF16), b.astype(BF16), preferred_element_type=F32)


def _dot_nt(a, b):
    return lax.dot_general(a.astype(BF16), b.astype(BF16), (((1,), (1,)), ((), ())),
                           preferred_element_type=F32)


def _swap_groups(x, q):
    w = x.shape[1]
    lane = lax.broadcasted_iota(jnp.int32, x.shape, 1)
    even = ((lane // q) % 2) == 0
    return jnp.where(even, pltpu.roll(x, w - q, 1), pltpu.roll(x, q, 1))


def _rope(x, cos, sin, q):
    return x * cos + _swap_groups(x, q) * sin


ADA_TN = 1536


def _adaln_kernel(c_ref, w_ref, b_ref, o_ref):
    c = c_ref[...]
    o_ref[0] = jnp.dot(_silu(c), w_ref[0], preferred_element_type=F32) + b_ref[0]


def _adaln(cond8, w_ada, b_ada):
    n = 6 * D_MODEL
    return pl.pallas_call(
        _adaln_kernel,
        grid=(DEPTH, n // ADA_TN),
        in_specs=[
            pl.BlockSpec((N_GROUPS, D_MODEL), lambda l, j: (0, 0)),
            pl.BlockSpec((1, D_MODEL, ADA_TN), lambda l, j: (l, 0, j)),
            pl.BlockSpec((1, 1, ADA_TN), lambda l, j: (l, 0, j)),
        ],
        out_specs=pl.BlockSpec((1, N_GROUPS, ADA_TN), lambda l, j: (l, 0, j)),
        out_shape=jax.ShapeDtypeStruct((DEPTH, N_GROUPS, n), F32),
        compiler_params=_cparams("arbitrary", "arbitrary"),
        name="adaln",
    )(cond8, w_ada, b_ada.reshape(DEPTH, 1, n))


def _chunk_scan(g, forward):
    n = g.shape[0]
    pos = lax.broadcasted_iota(jnp.int32, g.shape, 0) % GDN_CHUNK
    s = 1
    while s < GDN_CHUNK:
        if forward:
            g = g + jnp.where(pos >= s, pltpu.roll(g, s, 0), 0.0)
        else:
            g = g + jnp.where(pos < GDN_CHUNK - s, pltpu.roll(g, n - s, 0), 0.0)
        s *= 2
    return g


def _inproj_kernel(x_ref, mod_ref, n1g_ref, w_ref, ca_ref, sa_ref, cs_ref, ss_ref, cq_ref, sq_ref,
                   alog_ref, dtb_ref, gq_ref, gkv_ref, wuq_ref, wukk_ref, pkr_ref, wukv_ref,
                   aq_ref, ak_ref, av_ref, bq_ref, bk_ref, bv_ref, ckv_ref, small_ref,
                   cqkv_ref, gcol_ref, grow_ref, cz_ref, gates_ref):
    mod = mod_ref[0]
    h = _rms(x_ref[...]) * n1g_ref[...] * (1.0 + mod[1:2]) + mod[0:1]
    hb = h.astype(BF16)

    def proj(lo, hi):
        return jnp.dot(hb, w_ref[:, lo:hi], preferred_element_type=F32)

    ca, sa = ca_ref[...], sa_ref[...]
    for c in range(4):
        q = proj(O_AQ + c * LANES, O_AQ + (c + 1) * LANES)
        aq_ref[:, c * LANES:(c + 1) * LANES] = (_rope(q, ca, sa, 16) * A_SCALE).astype(BF16)
    ak_ref[...] = _rope(proj(O_AK, O_AV), ca, sa, 16)
    av_ref[...] = proj(O_AV, O_BCQ)

    cq = (_rms(proj(O_BCQ, O_BCKV)) * gq_ref[...]).astype(BF16)
    cqc, sqc = cq_ref[...], sq_ref[...]
    for hh in range(B_HEADS):
        qh = jnp.dot(cq, wuq_ref[:, hh * LANES:(hh + 1) * LANES], preferred_element_type=F32)
        bq_ref[:, hh * LANES:(hh + 1) * LANES] = _rope(qh, cqc, sqc, 8).astype(BF16)

    ckv = _rms(proj(O_BCKV, O_SMALL)) * gkv_ref[...]
    ckv_ref[...] = ckv
    ckvb = ckv.astype(BF16)
    small = proj(O_SMALL, O_CQKV)
    small_r = _rope(small, cs_ref[...], ss_ref[...], 8)
    small_ref[...] = small_r
    krb = small_r.astype(BF16)
    bk_ref[...] = (jnp.dot(ckvb, wukk_ref[...], preferred_element_type=F32)
                   + jnp.dot(krb, pkr_ref[...], preferred_element_type=F32)).astype(BF16)
    bv_ref[...] = jnp.dot(ckvb, wukv_ref[...], preferred_element_type=F32).astype(BF16)

    for c in range(3):
        cqkv_ref[:, c * 512:(c + 1) * 512] = proj(O_CQKV + c * 512, O_CQKV + (c + 1) * 512)
    cz_ref[...] = proj(O_CZ, O_GATES)
    for c in range(3):
        gates_ref[:, c * D_MODEL:(c + 1) * D_MODEL] = proj(O_GATES + c * D_MODEL, O_GATES + (c + 1) * D_MODEL)

    graw = -jnp.exp(alog_ref[...]) * _softplus(small + dtb_ref[...])
    lane = lax.broadcasted_iota(jnp.int32, small.shape, 1)
    gcol = jnp.where(lane < SM_A + C_HEADS, _chunk_scan(graw, True), _chunk_scan(graw, False))
    gcol = jnp.where((lane >= SM_A) & (lane < SM_B), gcol,
                     jnp.where((lane >= SM_B) & (lane < SM_B + 2 * C_HEADS), _sigmoid(small), 0.0))
    gcol_ref[...] = gcol
    grow_ref[...] = gcol.T[SM_A:SM_A + 16, :]


def _inproj(x_all, mod_l, n1g, wts, tabs):
    tile = lambda w: pl.BlockSpec((TM, w), lambda i: (i, 0))
    rope_spec = pl.BlockSpec((TM, LANES), lambda i: (_rope_block_of_tile(i), 0))
    out_shapes = [
        jax.ShapeDtypeStruct((T_ALL, 512), BF16),
        jax.ShapeDtypeStruct((T_ALL, 128), F32),
        jax.ShapeDtypeStruct((T_ALL, 128), F32),
        jax.ShapeDtypeStruct((T_ALL, 1024), BF16),
        jax.ShapeDtypeStruct((T_ALL, 1024), BF16),
        jax.ShapeDtypeStruct((T_ALL, 512), BF16),
        jax.ShapeDtypeStruct((T_ALL, 256), F32),
        jax.ShapeDtypeStruct((T_ALL, 128), F32),
        jax.ShapeDtypeStruct((T_ALL, 1536), F32),
        jax.ShapeDtypeStruct((T_ALL, 128), F32),
        jax.ShapeDtypeStruct((16, T_ALL), F32),
        jax.ShapeDtypeStruct((T_ALL, 512), F32),
        jax.ShapeDtypeStruct((T_ALL, 3072), F32),
    ]
    out_specs = [tile(512), tile(128), tile(128), tile(1024), tile(1024), tile(512), tile(256), tile(128),
                 tile(1536), tile(128), pl.BlockSpec((16, TM), lambda i: (0, i)), tile(512), tile(3072)]
    in_specs = [
        tile(D_MODEL),
        pl.BlockSpec((1, 6, D_MODEL), lambda i: (_group_of_tile(i), 0, 0)),
        _resident((1, D_MODEL)),
        _resident((D_MODEL, W_IN_COLS)),
        rope_spec, rope_spec, rope_spec, rope_spec, rope_spec, rope_spec,
        _resident((1, LANES)), _resident((1, LANES)),
        _resident((1, B_Q_LORA)), _resident((1, B_KV_LORA)),
        _resident((B_Q_LORA, 1024)), _resident((B_KV_LORA, 1024)), _resident((LANES, 1024)),
        _resident((B_KV_LORA, 512)),
    ]
    return pl.pallas_call(
        _inproj_kernel,
        grid=(N_TILES,),
        in_specs=in_specs,
        out_specs=out_specs,
        out_shape=out_shapes,
        compiler_params=_cparams("arbitrary"),
        name="inproj",
    )(x_all, mod_l, n1g, wts["w_in"], tabs["ca"], tabs["sa"], tabs["cs"], tabs["ss"], tabs["cq"], tabs["sq"],
      wts["alog"], wts["dtb"], wts["gq"], wts["gkv"], wts["wuq"], wts["wukk"], wts["pkr"], wts["wukv"])


def _mla_cache_kernel(ckv_ref, kr_ref, wukk_ref, pkr_ref, wukv_ref, k_ref, v_ref):
    ckvb = ckv_ref[...].astype(BF16)
    krb = kr_ref[...].astype(BF16)
    k_ref[...] = (jnp.dot(ckvb, wukk_ref[...], preferred_element_type=F32)
                  + jnp.dot(krb, pkr_ref[...], preferred_element_type=F32)).astype(BF16)
    v_ref[...] = jnp.dot(ckvb, wukv_ref[...], preferred_element_type=F32).astype(BF16)


def _mla_cache(ckv_c, kr_c, wts):
    n = ckv_c.shape[0]
    return pl.pallas_call(
        _mla_cache_kernel,
        grid=(n // PAST_LEN,),
        in_specs=[pl.BlockSpec((PAST_LEN, B_KV_LORA), lambda i: (i, 0)),
                  pl.BlockSpec((PAST_LEN, LANES), lambda i: (i, 0)),
                  _resident((B_KV_LORA, 1024)), _resident((LANES, 1024)), _resident((B_KV_LORA, 512))],
        out_specs=[pl.BlockSpec((PAST_LEN, 1024), lambda i: (i, 0)),
                   pl.BlockSpec((PAST_LEN, 512), lambda i: (i, 0))],
        out_shape=[jax.ShapeDtypeStruct((n, 1024), BF16), jax.ShapeDtypeStruct((n, 512), BF16)],
        compiler_params=_cparams("arbitrary"),
        name="mla_cache",
    )(ckv_c, kr_c, wts["wukk"], wts["pkr"], wts["wukv"])


TQ = 256
A_WIN_KEYS = TQ + 2 * A_WINDOW


def _softmax_pv(scores, values, extra_logit=None):
    m = scores[0].max(axis=-1, keepdims=True)
    for s in scores[1:]:
        m = jnp.maximum(m, s.max(axis=-1, keepdims=True))
    if extra_logit is not None:
        m = jnp.maximum(m, extra_logit)
    den = None
    acc = None
    for s, v in zip(scores, values):
        e = jnp.exp(s - m)
        d = e.sum(axis=-1, keepdims=True)
        den = d if den is None else den + d
        pv = jnp.dot(e.astype(BF16), v, preferred_element_type=F32)
        acc = pv if acc is None else acc + pv
    if extra_logit is not None:
        den = den + jnp.exp(extra_logit - m)
    return acc / den


def _attn_a_kernel(sink_ref, q_ref, k1_ref, v1_ref, *rest, latent):
    if latent:
        k2_ref, v2_ref, o_ref = rest
        qt = pl.program_id(1)
        t0 = qt * TQ
        start = pl.multiple_of(jnp.clip(t0 - A_WINDOW, 0, DEC_SEQ - A_WIN_KEYS), A_WINDOW)
        k1 = k1_ref[pl.ds(start, A_WIN_KEYS), :].astype(BF16)
        v1 = v1_ref[pl.ds(start, A_WIN_KEYS), :].astype(BF16)
        qpos = t0 + lax.broadcasted_iota(jnp.int32, (TQ, A_WIN_KEYS), 0)
        kpos = start + lax.broadcasted_iota(jnp.int32, (TQ, A_WIN_KEYS), 1)
        valid = jnp.abs(qpos - kpos) <= A_WINDOW
        k2 = k2_ref[0].astype(BF16)
        v2 = v2_ref[0].astype(BF16)
    else:
        (o_ref,) = rest
        k1 = k1_ref[...].astype(BF16)
        v1 = v1_ref[...].astype(BF16)
    lane = lax.broadcasted_iota(jnp.int32, (TQ, LANES), 1)
    low = lane < A_HEAD_DIM
    for g in range(A_GROUP):
        qc = q_ref[:, g * LANES:(g + 1) * LANES]
        outs = []
        for hk in range(A_KV_HEADS):
            qz = jnp.where(low if hk == 0 else ~low, qc, jnp.zeros_like(qc))
            s1 = lax.dot_general(qz, k1, (((1,), (1,)), ((), ())), preferred_element_type=F32)
            sink = sink_ref[hk * A_GROUP + g]
            if latent:
                s1 = jnp.where(valid, s1, NEG_INF)
                s2 = lax.dot_general(qz, k2, (((1,), (1,)), ((), ())), preferred_element_type=F32)
                outs.append(_softmax_pv([s1, s2], [v1, v2], sink))
            else:
                outs.append(_softmax_pv([s1], [v1], sink))
        o_ref[:, g * LANES:(g + 1) * LANES] = jnp.where(low, outs[0], outs[1]).astype(BF16)


def _attn_a_ctx(sink, aq, ak, av):
    nb = T_CTX // TQ
    return pl.pallas_call(
        functools.partial(_attn_a_kernel, latent=False),
        grid=(nb,),
        in_specs=[pl.BlockSpec(memory_space=pltpu.SMEM),
                  pl.BlockSpec((TQ, 512), lambda b: (b, 0)),
                  pl.BlockSpec((SEQ, LANES), lambda b: (b, 0)),
                  pl.BlockSpec((SEQ, LANES), lambda b: (b, 0))],
        out_specs=pl.BlockSpec((TQ, 512), lambda b: (b, 0)),
        out_shape=jax.ShapeDtypeStruct((T_CTX, 512), BF16),
        compiler_params=_cparams("arbitrary"),
        name="attn_a_ctx",
    )(sink, aq, ak, av)


def _attn_a_lat(sink, aq, ak, av, ck, cv):
    nq = DEC_SEQ // TQ
    ctx_tiles = T_CTX // TQ
    seq_blocks = T_CTX // DEC_SEQ
    return pl.pallas_call(
        functools.partial(_attn_a_kernel, latent=True),
        grid=(DEC_BATCH, nq),
        in_specs=[pl.BlockSpec(memory_space=pltpu.SMEM),
                  pl.BlockSpec((TQ, 512), lambda b, i: (ctx_tiles + b * nq + i, 0)),
                  pl.BlockSpec((DEC_SEQ, LANES), lambda b, i: (seq_blocks + b, 0)),
                  pl.BlockSpec((DEC_SEQ, LANES), lambda b, i: (seq_blocks + b, 0)),
                  pl.BlockSpec((1, PAST_LEN, LANES), lambda b, i: (b, 0, 0)),
                  pl.BlockSpec((1, PAST_LEN, LANES), lambda b, i: (b, 0, 0))],
        out_specs=pl.BlockSpec((TQ, 512), lambda b, i: (b * nq + i, 0)),
        out_shape=jax.ShapeDtypeStruct((T_LAT, 512), BF16),
        compiler_params=_cparams("arbitrary", "arbitrary"),
        name="attn_a_lat",
    )(sink, aq, ak, av, ck, cv)


def _mla_kernel(q_ref, k1_ref, v1_ref, *rest, latent):
    if latent:
        k2_ref, v2_ref, o_ref = rest
    else:
        (o_ref,) = rest
    lane = lax.broadcasted_iota(jnp.int32, (TQ, LANES), 1)
    low = lane < B_V
    nt = (((1,), (1,)), ((), ()))
    for j in range(B_HEADS // 2):
        outs = []
        for sub in range(2):
            hh = 2 * j + sub
            hs = slice(hh * LANES, (hh + 1) * LANES)
            vs = slice(j * LANES, (j + 1) * LANES)
            qh = q_ref[:, hs]
            s1 = lax.dot_general(qh, k1_ref[:, hs], nt, preferred_element_type=F32) * MLA_SCALE
            if latent:
                s2 = lax.dot_general(qh, k2_ref[:, hs], nt, preferred_element_type=F32) * MLA_SCALE
                outs.append(_softmax_pv([s1, s2], [v1_ref[:, vs], v2_ref[:, vs]]))
            else:
                outs.append(_softmax_pv([s1], [v1_ref[:, vs]]))
        o_ref[:, j * LANES:(j + 1) * LANES] = jnp.where(low, outs[0], outs[1]).astype(BF16)


def _mla_ctx(bq, bk, bv):
    nb = T_CTX // TQ
    return pl.pallas_call(
        functools.partial(_mla_kernel, latent=False),
        grid=(nb,),
        in_specs=[pl.BlockSpec((TQ, 1024), lambda b: (b, 0)),
                  pl.BlockSpec((SEQ, 1024), lambda b: (b, 0)),
                  pl.BlockSpec((SEQ, 512), lambda b: (b, 0))],
        out_specs=pl.BlockSpec((TQ, 512), lambda b: (b, 0)),
        out_shape=jax.ShapeDtypeStruct((T_CTX, 512), BF16),
        compiler_params=_cparams("arbitrary"),
        name="mla_ctx",
    )(bq, bk, bv)


def _mla_lat(bq, bk, bv, k2, v2):
    nq = DEC_SEQ // TQ
    ctx_tiles = T_CTX // TQ
    seq_blocks = T_CTX // DEC_SEQ
    return pl.pallas_call(
        functools.partial(_mla_kernel, latent=True),
        grid=(DEC_BATCH, nq),
        in_specs=[pl.BlockSpec((TQ, 1024), lambda b, i: (ctx_tiles + b * nq + i, 0)),
                  pl.BlockSpec((DEC_SEQ, 1024), lambda b, i: (seq_blocks + b, 0)),
                  pl.BlockSpec((DEC_SEQ, 512), lambda b, i: (seq_blocks + b, 0)),
                  pl.BlockSpec((PAST_LEN, 1024), lambda b, i: (b, 0)),
                  pl.BlockSpec((PAST_LEN, 512), lambda b, i: (b, 0))],
        out_specs=pl.BlockSpec((TQ, 512), lambda b, i: (b * nq + i, 0)),
        out_shape=jax.ShapeDtypeStruct((T_LAT, 512), BF16),
        compiler_params=_cparams("arbitrary", "arbitrary"),
        name="mla_lat",
    )(bq, bk, bv, k2, v2)


def _gdn_kernel(xq_ref, xk_ref, xv_ref, wq_ref, wk_ref, wv_ref, gcol_ref, grow_ref, cz_ref, ong_ref, *rest,
                n, has_s0, emit_state):
    rest = list(rest)
    s0_ref = rest.pop(0) if has_s0 else None
    oc_ref = rest.pop(0)
    sfin_ref = rest.pop(0) if emit_state else None
    q_s, k_s, v_s, u_s, w_s, qk_s, qd_s, kdt_s, ge_s, o_s = rest
    nc = n // GDN_CHUNK
    head = pl.program_id(1)
    C = GDN_CHUNK

    row_n = lax.broadcasted_iota(jnp.int32, (n, LANES), 0)

    def conv_silu(x_ref, w_ref):
        x = x_ref[...]
        w = w_ref[...]
        xp = jnp.where(row_n == 0, 0.0, pltpu.roll(x, 1, 0))
        xn = jnp.where(row_n == n - 1, 0.0, pltpu.roll(x, n - 1, 0))
        return _silu(xp * w[0:1] + x * w[1:2] + xn * w[2:3])

    def l2n(x):
        return x * lax.rsqrt(jnp.sum(x * x, axis=-1, keepdims=True) + EPS)

    q_s[...] = l2n(conv_silu(xq_ref, wq_ref)) * (C_DK ** -0.5)
    k_s[...] = l2n(conv_silu(xk_ref, wk_ref))
    v_s[...] = conv_silu(xv_ref, wv_ref)
    o_s[...] = jnp.zeros_like(o_s)

    lane_c = lax.broadcasted_iota(jnp.int32, (C, LANES), 1)
    ri = lax.broadcasted_iota(jnp.int32, (C, C), 0)
    ci = lax.broadcasted_iota(jnp.int32, (C, C), 1)
    eye = (ri == ci).astype(F32)
    same_block = {}
    s = 2
    while s <= C:
        same_block[s] = (ri // s) == (ci // s)
        s *= 2

    def prep(r, carry):
        rows = pl.ds(pl.multiple_of(r * C, C), C)
        q = q_s[rows, :]
        k = k_s[rows, :]
        v = v_s[rows, :]
        gtile = gcol_ref[rows, :]
        for d in range(2):
            j = d * C_HEADS + head
            gc = jnp.sum(jnp.where(lane_c == SM_A + j, gtile, 0.0), axis=1, keepdims=True)
            beta = jnp.sum(jnp.where(lane_c == SM_B + j, gtile, 0.0), axis=1, keepdims=True)
            grow = grow_ref[j, r]
            incl = (ri >= ci) if d == 0 else (ri <= ci)
            strict = (ri > ci) if d == 0 else (ri < ci)
            decay = jnp.where(incl, jnp.exp(jnp.where(incl, gc - grow, 0.0)), 0.0)
            kb = k * beta
            a = jnp.where(strict, _dot_nt(kb, k) * decay, 0.0)
            t = eye - jnp.where(same_block[2], a, 0.0)
            s = 2
            while s < C:
                e = jnp.where(same_block[2 * s] & ~same_block[s], a, 0.0)
                t = t - _dot(_dot(t, e), t)
                s *= 2
            g_end = gc[C - 1:C, :] if d == 0 else gc[0:1, :]
            egc = jnp.exp(gc)
            rhs = jnp.concatenate([v * beta, kb * egc], axis=1)
            uw = _dot(t, rhs)
            u_s[d, rows, :] = uw[:, :C_DV]
            w_s[d, rows, :] = uw[:, C_DV:].astype(BF16)
            qk_s[d, rows, :] = jnp.where(incl, _dot_nt(q, k) * decay, 0.0).astype(BF16)
            qd_s[d, rows, :] = (q * egc).astype(BF16)
            kdt_s[d, rows, :] = (k * jnp.exp(g_end - gc)).T.astype(BF16)
            ge_s[d, r] = jnp.broadcast_to(jnp.exp(g_end), (8, LANES))
        return carry

    lax.fori_loop(0, nc, prep, 0)

    if has_s0:
        s_init = (s0_ref[0, 0, 0, 0], s0_ref[0, 0, 1, 0])
    else:
        s_init = (jnp.zeros((C_DK, C_DV), F32), jnp.zeros((C_DK, C_DV), F32))

    def step(i, states):
        new = []
        for d in range(2):
            r = i if d == 0 else nc - 1 - i
            rows = pl.ds(pl.multiple_of(r * C, C), C)
            s = states[d]
            sb = s.astype(BF16)
            vn = u_s[d, rows, :] - jnp.dot(w_s[d, rows, :], sb, preferred_element_type=F32)
            vnb = vn.astype(BF16)
            o = (jnp.dot(qd_s[d, rows, :], sb, preferred_element_type=F32)
                 + jnp.dot(qk_s[d, rows, :], vnb, preferred_element_type=F32))
            o_s[rows, :] += o
            new.append(s * ge_s[d, r][0:1, :] + jnp.dot(kdt_s[d, rows, :], vnb, preferred_element_type=F32))
        return tuple(new)

    s_f, s_b = lax.fori_loop(0, nc, step, s_init)
    if emit_state:
        sfin_ref[0, 0, 0] = s_f
        sfin_ref[0, 1, 0] = s_b

    o = o_s[...]
    oc_ref[...] = (_rms(o) * ong_ref[...] * _silu(cz_ref[...])).astype(BF16)


def _gdn(cqkv, conv_w, gcol, grow3, cz, ong, state_c, layer, *, latent):
    n = DEC_SEQ if latent else SEQ
    nseq = DEC_BATCH if latent else BATCH
    seq0 = (T_CTX // n) if latent else 0
    nc = n // GDN_CHUNK
    col = lambda base: pl.BlockSpec((n, LANES), lambda b, h: (seq0 + b, base + h))
    wcol = lambda base: pl.BlockSpec((3, LANES), lambda b, h: (0, base + h))
    in_specs = [col(0), col(4), col(8), wcol(0), wcol(4), wcol(8),
                pl.BlockSpec((n, LANES), lambda b, h: (seq0 + b, 0)),
                pl.BlockSpec((16, nc, 1, LANES), lambda b, h: (0, seq0 + b, 0, 0)),
                pl.BlockSpec((n, LANES), lambda b, h: (seq0 + b, h)),
                pl.BlockSpec((1, LANES), lambda b, h: (0, 0))]
    args = [cqkv, cqkv, cqkv, conv_w, conv_w, conv_w, gcol, grow3, cz, ong]
    out_specs = [pl.BlockSpec((n, LANES), lambda b, h: (b, h))]
    out_shape = [jax.ShapeDtypeStruct((nseq * n, 512), BF16)]
    if latent:
        in_specs.append(pl.BlockSpec((1, 1, 2, 1, C_DK, C_DV), lambda b, h: (b, layer, 0, h, 0, 0)))
        args.append(state_c)
    else:
        out_specs.append(pl.BlockSpec((1, 2, 1, C_DK, C_DV), lambda b, h: (b, 0, h, 0, 0)))
        out_shape.append(jax.ShapeDtypeStruct((nseq, 2, C_HEADS, C_DK, C_DV), F32))
    scratch = [pltpu.VMEM((n, LANES), F32), pltpu.VMEM((n, LANES), F32), pltpu.VMEM((n, LANES), F32),
               pltpu.VMEM((2, n, LANES), F32), pltpu.VMEM((2, n, LANES), BF16), pltpu.VMEM((2, n, LANES), BF16),
               pltpu.VMEM((2, n, LANES), BF16), pltpu.VMEM((2, n, LANES), BF16),
               pltpu.VMEM((2, nc, 8, LANES), F32), pltpu.VMEM((n, LANES), F32)]
    res = pl.pallas_call(
        functools.partial(_gdn_kernel, n=n, has_s0=latent, emit_state=not latent),
        grid=(nseq, C_HEADS),
        in_specs=in_specs,
        out_specs=out_specs,
        out_shape=out_shape,
        scratch_shapes=scratch,
        compiler_params=_cparams("arbitrary", "arbitrary"),
        name="gdn_lat" if latent else "gdn_ctx",
    )(*args)
    return res


def _merge_ffn_kernel(x_ref, oa_ref, ob_ref, oc_ref, gates_ref, mod_ref, n2g_ref, wbr_ref, wout_ref,
                      wff1_ref, wff2_ref, fg_ref, o_ref, *, final):
    mod = mod_ref[0]
    z = None
    for j, br in enumerate((oa_ref, ob_ref, oc_ref)):
        y = jnp.dot(br[...], wbr_ref[j], preferred_element_type=F32)
        t = _sigmoid(gates_ref[:, j * D_MODEL:(j + 1) * D_MODEL]) * y
        z = t if z is None else z + t
    x = x_ref[...] + mod[2:3] * jnp.dot(z.astype(BF16), wout_ref[...], preferred_element_type=F32)
    h = (_rms(x) * n2g_ref[...] * (1.0 + mod[4:5]) + mod[3:4]).astype(BF16)
    f = jnp.maximum(jnp.dot(h, wff1_ref[...], preferred_element_type=F32), 0.0)
    f = (f * f).astype(BF16)
    x = x + mod[5:6] * jnp.dot(f, wff2_ref[...], preferred_element_type=F32)
    if final:
        x = _rms(x) * fg_ref[...]
    o_ref[...] = x


def _merge_ffn(x_all, oa, ob, oc, gates, mod_l, n2g, wts, final_g, *, final):
    tile = lambda w: pl.BlockSpec((TM, w), lambda i: (i, 0))
    return pl.pallas_call(
        functools.partial(_merge_ffn_kernel, final=final),
        grid=(N_TILES,),
        in_specs=[tile(D_MODEL), tile(512), tile(512), tile(512), tile(3072),
                  pl.BlockSpec((1, 6, D_MODEL), lambda i: (_group_of_tile(i), 0, 0)),
                  _resident((1, D_MODEL)),
                  _resident((3, BRANCH_W, D_MODEL)), _resident((D_MODEL, D_MODEL)),
                  _resident((D_MODEL, D_FF)), _resident((D_FF, D_MODEL)), _resident((1, D_MODEL))],
        out_specs=tile(D_MODEL),
        out_shape=jax.ShapeDtypeStruct((T_ALL, D_MODEL), F32),
        compiler_params=_cparams("arbitrary"),
        name="merge_ffn",
    )(x_all, oa, ob, oc, gates, mod_l, n2g, wts["wbr"], wts["wout"], wts["wff1"], wts["wff2"], final_g)


def _rope_tables(r):
    half, quarter = r // 2, r // 4
    t = jnp.arange(DEC_SEQ)
    inv_freq = jnp.power(ROPE_BASE, -jnp.arange(quarter, dtype=F32) / quarter)
    cs, ss = [], []
    for pos in (t // GRID_W, t % GRID_W):
        ang = pos.astype(F32)[:, None] * inv_freq[None, :]
        c, s = jnp.cos(ang), jnp.sin(ang)
        cs += [c, c]
        ss += [-s, s]
    return jnp.concatenate(cs, axis=1), jnp.concatenate(ss, axis=1)


def _tables():
    def with_identity(c, s):
        one = jnp.ones((TM, LANES), F32)
        return jnp.concatenate([one, c], axis=0), jnp.concatenate([0.0 * one, s], axis=0)

    c64, s64 = _rope_tables(A_HEAD_DIM)
    ca, sa = with_identity(jnp.tile(c64, (1, 2)), jnp.tile(s64, (1, 2)))
    c32, s32 = _rope_tables(B_ROPE)
    ones = lambda w: jnp.ones((DEC_SEQ, w), F32)
    zeros = lambda w: jnp.zeros((DEC_SEQ, w), F32)
    cs, ss = with_identity(jnp.concatenate([c32, ones(96)], axis=1), jnp.concatenate([s32, zeros(96)], axis=1))
    cq, sq = with_identity(jnp.concatenate([ones(64), c32, ones(32)], axis=1),
                           jnp.concatenate([zeros(64), s32, zeros(32)], axis=1))
    return {"ca": ca, "sa": sa, "cs": cs, "ss": ss, "cq": cq, "sq": sq}


_A_HEAD_PERM = [hk * A_GROUP + g for g in range(A_GROUP) for hk in range(A_KV_HEADS)]


def _layer_weights(l, w_in, b_gq, b_w_uq, b_gkv, b_w_ukv, c_a_log, c_dt_bias, w_branch, w_out, w_ff1, w_ff2):
    w = w_in[l]
    offs = [0]
    for s in (512, 128, 128, B_Q_LORA, B_KV_LORA, B_ROPE, 1536, 8, 8, 512, 3072):
        offs.append(offs[-1] + s)
    a_q, a_k, a_v, b_cq, b_ckv, b_kr, c_qkv, c_a, c_b, c_z, gates = (
        w[:, offs[i]:offs[i + 1]] for i in range(11))
    a_q = a_q.reshape(D_MODEL, A_HEADS, A_HEAD_DIM)[:, jnp.array(_A_HEAD_PERM)].reshape(D_MODEL, 512)
    small = jnp.concatenate([b_kr, c_a, c_b, jnp.zeros((D_MODEL, LANES - 48), F32)], axis=1)
    w_r = jnp.concatenate([a_q, a_k, a_v, b_cq, b_ckv, small, c_qkv, c_z, gates], axis=1).astype(BF16)

    pad_lane = lambda v: jnp.zeros((1, LANES), F32).at[0, SM_A:SM_A + 8].set(v.reshape(8))
    uq = b_w_uq[l].reshape(B_Q_LORA, B_HEADS, B_NOPE + B_ROPE)
    uq = jnp.pad(uq, ((0, 0), (0, 0), (0, LANES - B_NOPE - B_ROPE))).reshape(B_Q_LORA, 1024).astype(BF16)
    ukv = b_w_ukv[l].reshape(B_KV_LORA, B_HEADS, B_NOPE + B_V)
    ukk = jnp.pad(ukv[:, :, :B_NOPE], ((0, 0), (0, 0), (0, LANES - B_NOPE))).reshape(B_KV_LORA, 1024).astype(BF16)
    ukvv = ukv[:, :, B_NOPE:].reshape(B_KV_LORA, 512).astype(BF16)
    src = jnp.arange(LANES)[:, None]
    dst = jnp.arange(1024)[None, :]
    pkr = ((dst % LANES == B_NOPE + src) & (src < B_ROPE)).astype(BF16)

    wbr = w_branch[l]
    wbr_a = wbr[0].reshape(A_HEADS, A_HEAD_DIM, D_MODEL)[jnp.array(_A_HEAD_PERM)].reshape(BRANCH_W, D_MODEL)
    wbr = jnp.stack([wbr_a, wbr[1], wbr[2]]).astype(BF16)
    return {"w_in": w_r, "alog": pad_lane(c_a_log[l]), "dtb": pad_lane(c_dt_bias[l]),
            "gq": b_gq[l].reshape(1, -1), "gkv": b_gkv[l].reshape(1, -1),
            "wuq": uq, "wukk": ukk, "pkr": pkr, "wukv": ukvv,
            "wbr": wbr, "wout": w_out[l].astype(BF16), "wff1": w_ff1[l].astype(BF16),
            "wff2": w_ff2[l].astype(BF16)}


def kernel(x_prompt, x_sample, cache_a_k, cache_a_v, cache_b_ckv, cache_b_krope, state_c, c, c_ctx, norm1_g,
           norm2_g, w_ada, b_ada, w_in, a_sink, b_gq, b_w_uq, b_gkv, b_w_ukv, c_conv_w, c_a_log, c_dt_bias,
           c_onorm_g, w_branch, w_out, w_ff1, w_ff2, final_g):
    x_all = jnp.concatenate([x_prompt.reshape(T_CTX, D_MODEL), x_sample.reshape(T_LAT, D_MODEL)], axis=0)
    cond8 = jnp.concatenate([c_ctx[None, :], c, jnp.zeros((N_GROUPS - 1 - DEC_BATCH, D_MODEL), F32)], axis=0)
    mod = _adaln(cond8, w_ada, b_ada).reshape(DEPTH, N_GROUPS, 6, D_MODEL)
    tabs = _tables()
    fg = final_g.reshape(1, D_MODEL)

    ak_l, av_l, ckv_l, kr_l, sc_l = [], [], [], [], []
    for l in range(DEPTH):
        wts = _layer_weights(l, w_in, b_gq, b_w_uq, b_gkv, b_w_ukv, c_a_log, c_dt_bias, w_branch, w_out,
                             w_ff1, w_ff2)
        (aq, ak, av, bq, bk, bv, ckv, small, cqkv, gcol, grow, cz, gates) = _inproj(
            x_all, mod[l], norm1_g[l].reshape(1, D_MODEL), wts, tabs)

        sink = a_sink[l]
        oa_ctx = _attn_a_ctx(sink, aq, ak, av)
        oa_lat = _attn_a_lat(sink, aq, ak, av,
                             cache_a_k[:, l].reshape(DEC_BATCH, PAST_LEN, LANES),
                             cache_a_v[:, l].reshape(DEC_BATCH, PAST_LEN, LANES))

        kr_c = jnp.pad(cache_b_krope[:, l].reshape(DEC_BATCH * PAST_LEN, B_ROPE), ((0, 0), (0, LANES - B_ROPE)))
        k2, v2 = _mla_cache(cache_b_ckv[:, l].reshape(DEC_BATCH * PAST_LEN, B_KV_LORA), kr_c, wts)
        ob_ctx = _mla_ctx(bq, bk, bv)
        ob_lat = _mla_lat(bq, bk, bv, k2, v2)

        grow3 = grow.reshape(16, T_ALL // GDN_CHUNK, 1, GDN_CHUNK)
        ong = c_onorm_g[l].reshape(1, C_DV)
        oc_ctx, s_fin = _gdn(cqkv, c_conv_w[l], gcol, grow3, cz, ong, state_c, l, latent=False)
        (oc_lat,) = _gdn(cqkv, c_conv_w[l], gcol, grow3, cz, ong, state_c, l, latent=True)

        cat = lambda a, b: jnp.concatenate([a, b], axis=0)
        x_all = _merge_ffn(x_all, cat(oa_ctx, oa_lat), cat(ob_ctx, ob_lat), cat(oc_ctx, oc_lat), gates,
                           mod[l], norm2_g[l].reshape(1, D_MODEL), wts, fg, final=(l == DEPTH - 1))

        ak_l.append(ak[:T_CTX].reshape(BATCH, SEQ, A_KV_HEADS, A_HEAD_DIM))
        av_l.append(av[:T_CTX].reshape(BATCH, SEQ, A_KV_HEADS, A_HEAD_DIM))
        ckv_l.append(ckv[:T_CTX].reshape(BATCH, SEQ, B_KV_LORA))
        kr_l.append(small[:T_CTX, :B_ROPE].reshape(BATCH, SEQ, B_ROPE))
        sc_l.append(s_fin)

    y_prompt = x_all[:T_CTX].reshape(BATCH, SEQ, D_MODEL)
    y_sample = x_all[T_CTX:].reshape(DEC_BATCH, DEC_SEQ, D_MODEL)
    return (y_prompt, y_sample, jnp.stack(ak_l, axis=1), jnp.stack(av_l, axis=1), jnp.stack(ckv_l, axis=1),
            jnp.stack(kr_l, axis=1), jnp.stack(sc_l, axis=1))
```

```python
import functools
import math

import jax
import jax.numpy as jnp
from jax import lax
from jax.experimental import pallas as pl
from jax.experimental.pallas import tpu as pltpu

F32 = jnp.float32
BF16 = jnp.bfloat16

D_MODEL = 1024
BATCH = 16
SEQ = 256
DEPTH = 2
DEC_BATCH = 4
DEC_SEQ = 2048
PAST_LEN = 256
GRID_W = 64
ROPE_BASE = 10000.0
EPS = 1e-6
NEG_INF = -1e30

A_HEADS = 8
A_KV_HEADS = 2
A_GROUP = 4
A_HEAD_DIM = 64
A_WINDOW = 128
A_SCALE = A_HEAD_DIM ** -0.5
B_HEADS = 8
B_Q_LORA = 384
B_KV_LORA = 256
B_NOPE = 64
B_ROPE = 32
B_V = 64
MLA_SCALE = (B_NOPE + B_ROPE) ** -0.5
C_HEADS = 4
C_DK = 128
C_DV = 128
D_FF = 4 * D_MODEL
BRANCH_W = 512

LANES = 128

T_CTX = BATCH * SEQ
T_LAT = DEC_BATCH * DEC_SEQ
T_ALL = T_CTX + T_LAT
TM = 256
N_CTX_TILES = T_CTX // TM
LAT_TILES_PER_SEQ = DEC_SEQ // TM
N_TILES = T_ALL // TM
N_GROUPS = 8
GDN_CHUNK = 128
GDN_PREP_GROUP = 4

O_AQ, O_AK, O_AV, O_BCQ, O_BCKV, O_SMALL, O_CQKV, O_CZ, O_GATES, W_IN_COLS = (
    0, 512, 640, 768, 1152, 1408, 1536, 3072, 3584, 6656)
SM_KR, SM_A, SM_B = 0, 32, 40

VMEM_LIMIT = 56 * 1024 * 1024


def _cparams(*sem):
    return pltpu.CompilerParams(dimension_semantics=sem, vmem_limit_bytes=VMEM_LIMIT)


def _resident(shape):
    nd = len(shape)
    return pl.BlockSpec(shape, lambda *_: (0,) * nd, pipeline_mode=pl.Buffered(1))


def _group_of_tile(i):
    return jnp.where(i < N_CTX_TILES, 0, 1 + (i - N_CTX_TILES) // LAT_TILES_PER_SEQ)


def _rope_block_of_tile(i):
    return jnp.where(i < N_CTX_TILES, 0, 1 + (i - N_CTX_TILES) % LAT_TILES_PER_SEQ)


def _rms(x):
    return x * lax.rsqrt(jnp.mean(x * x, axis=-1, keepdims=True) + EPS)


def _sigmoid(x):
    return 1.0 / (1.0 + jnp.exp(-x))


def _silu(x):
    return x * _sigmoid(x)


def _softplus(x):
    return jnp.maximum(x, 0.0) + jnp.log(1.0 + jnp.exp(-jnp.abs(x)))


def _dot(a, b):
    return jnp.dot(a.astype(BF16), b.astype(BF16), preferred_element_type=F32)


def _dot_nt(a, b):
    return lax.dot_general(a.astype(BF16), b.astype(BF16), (((1,), (1,)), ((), ())),
                           preferred_element_type=F32)


def _swap_groups(x, q):
    w = x.shape[1]
    lane = lax.broadcasted_iota(jnp.int32, x.shape, 1)
    even = ((lane // q) % 2) == 0
    return jnp.where(even, pltpu.roll(x, w - q, 1), pltpu.roll(x, q, 1))


def _rope(x, cos, sin, q):
    return x * cos + _swap_groups(x, q) * sin


ADA_TN = 1536


def _adaln_kernel(c_ref, w_ref, b_ref, o_ref):
    c = c_ref[...]
    o_ref[0] = jnp.dot(_silu(c), w_ref[0], preferred_element_type=F32) + b_ref[0]


def _adaln(cond8, w_ada, b_ada):
    n = 6 * D_MODEL
    return pl.pallas_call(
        _adaln_kernel,
        grid=(DEPTH, n // ADA_TN),
        in_specs=[
            pl.BlockSpec((N_GROUPS, D_MODEL), lambda l, j: (0, 0)),
            pl.BlockSpec((1, D_MODEL, ADA_TN), lambda l, j: (l, 0, j)),
            pl.BlockSpec((1, 1, ADA_TN), lambda l, j: (l, 0, j)),
        ],
        out_specs=pl.BlockSpec((1, N_GROUPS, ADA_TN), lambda l, j: (l, 0, j)),
        out_shape=jax.ShapeDtypeStruct((DEPTH, N_GROUPS, n), F32),
        compiler_params=_cparams("arbitrary", "arbitrary"),
        name="adaln",
    )(cond8, w_ada, b_ada.reshape(DEPTH, 1, n))


def _chunk_scan(g, forward):
    n = g.shape[0]
    pos = lax.broadcasted_iota(jnp.int32, g.shape, 0) % GDN_CHUNK
    s = 1
    while s < GDN_CHUNK:
        if forward:
            g = g + jnp.where(pos >= s, pltpu.roll(g, s, 0), 0.0)
        else:
            g = g + jnp.where(pos < GDN_CHUNK - s, pltpu.roll(g, n - s, 0), 0.0)
        s *= 2
    return g


def _inproj_kernel(x_ref, mod_ref, n1g_ref, w_ref, ca_ref, sa_ref, cs_ref, ss_ref, cq_ref, sq_ref,
                   alog_ref, dtb_ref, gq_ref, gkv_ref, wuq_ref, wukk_ref, pkr_ref, wukv_ref,
                   aq_ref, ak_ref, av_ref, bq_ref, bk_ref, bv_ref, ckv_ref, small_ref,
                   cqkv_ref, gcol_ref, grow_ref, cz_ref, gates_ref):
    mod = mod_ref[0]
    h = _rms(x_ref[...]) * n1g_ref[...] * (1.0 + mod[1:2]) + mod[0:1]
    hb = h.astype(BF16)

    def proj(lo, hi):
        return jnp.dot(hb, w_ref[:, lo:hi], preferred_element_type=F32)

    ca, sa = ca_ref[...], sa_ref[...]
    for c in range(4):
        q = proj(O_AQ + c * LANES, O_AQ + (c + 1) * LANES)
        aq_ref[:, c * LANES:(c + 1) * LANES] = (_rope(q, ca, sa, 16) * A_SCALE).astype(BF16)
    ak_ref[...] = _rope(proj(O_AK, O_AV), ca, sa, 16)
    av_ref[...] = proj(O_AV, O_BCQ)

    cq = (_rms(proj(O_BCQ, O_BCKV)) * gq_ref[...]).astype(BF16)
    cqc, sqc = cq_ref[...], sq_ref[...]
    for hh in range(B_HEADS):
        qh = jnp.dot(cq, wuq_ref[:, hh * LANES:(hh + 1) * LANES], preferred_element_type=F32)
        bq_ref[:, hh * LANES:(hh + 1) * LANES] = _rope(qh, cqc, sqc, 8).astype(BF16)

    ckv = _rms(proj(O_BCKV, O_SMALL)) * gkv_ref[...]
    ckv_ref[...] = ckv
    ckvb = ckv.astype(BF16)
    small = proj(O_SMALL, O_CQKV)
    small_r = _rope(small, cs_ref[...], ss_ref[...], 8)
    small_ref[...] = small_r
    krb = small_r.astype(BF16)
    bk_ref[...] = (jnp.dot(ckvb, wukk_ref[...], preferred_element_type=F32)
                   + jnp.dot(krb, pkr_ref[...], preferred_element_type=F32)).astype(BF16)
    bv_ref[...] = jnp.dot(ckvb, wukv_ref[...], preferred_element_type=F32).astype(BF16)

    for c in range(3):
        cqkv_ref[:, c * 512:(c + 1) * 512] = proj(O_CQKV + c * 512, O_CQKV + (c + 1) * 512)
    cz_ref[...] = proj(O_CZ, O_GATES)
    for c in range(3):
        gates_ref[:, c * D_MODEL:(c + 1) * D_MODEL] = proj(O_GATES + c * D_MODEL, O_GATES + (c + 1) * D_MODEL)

    graw = -jnp.exp(alog_ref[...]) * _softplus(small + dtb_ref[...])
    lane = lax.broadcasted_iota(jnp.int32, small.shape, 1)
    gcol = jnp.where(lane < SM_A + C_HEADS, _chunk_scan(graw, True), _chunk_scan(graw, False))
    gcol = jnp.where((lane >= SM_A) & (lane < SM_B), gcol,
                     jnp.where((lane >= SM_B) & (lane < SM_B + 2 * C_HEADS), _sigmoid(small), 0.0))
    gcol_ref[...] = gcol
    grow_ref[...] = gcol.T[SM_A:SM_A + 16, :]


def _inproj(x_all, mod_l, n1g, wts, tabs):
    tile = lambda w: pl.BlockSpec((TM, w), lambda i: (i, 0))
    rope_spec = pl.BlockSpec((TM, LANES), lambda i: (_rope_block_of_tile(i), 0))
    out_shapes = [
        jax.ShapeDtypeStruct((T_ALL, 512), BF16),
        jax.ShapeDtypeStruct((T_ALL, 128), F32),
        jax.ShapeDtypeStruct((T_ALL, 128), F32),
        jax.ShapeDtypeStruct((T_ALL, 1024), BF16),
        jax.ShapeDtypeStruct((T_ALL, 1024), BF16),
        jax.ShapeDtypeStruct((T_ALL, 512), BF16),
        jax.ShapeDtypeStruct((T_ALL, 256), F32),
        jax.ShapeDtypeStruct((T_ALL, 128), F32),
        jax.ShapeDtypeStruct((T_ALL, 1536), F32),
        jax.ShapeDtypeStruct((T_ALL, 128), F32),
        jax.ShapeDtypeStruct((16, T_ALL), F32),
        jax.ShapeDtypeStruct((T_ALL, 512), F32),
        jax.ShapeDtypeStruct((T_ALL, 3072), F32),
    ]
    out_specs = [tile(512), tile(128), tile(128), tile(1024), tile(1024), tile(512), tile(256), tile(128),
                 tile(1536), tile(128), pl.BlockSpec((16, TM), lambda i: (0, i)), tile(512), tile(3072)]
    in_specs = [
        tile(D_MODEL),
        pl.BlockSpec((1, 6, D_MODEL), lambda i: (_group_of_tile(i), 0, 0)),
        _resident((1, D_MODEL)),
        _resident((D_MODEL, W_IN_COLS)),
        rope_spec, rope_spec, rope_spec, rope_spec, rope_spec, rope_spec,
        _resident((1, LANES)), _resident((1, LANES)),
        _resident((1, B_Q_LORA)), _resident((1, B_KV_LORA)),
        _resident((B_Q_LORA, 1024)), _resident((B_KV_LORA, 1024)), _resident((LANES, 1024)),
        _resident((B_KV_LORA, 512)),
    ]
    return pl.pallas_call(
        _inproj_kernel,
        grid=(N_TILES,),
        in_specs=in_specs,
        out_specs=out_specs,
        out_shape=out_shapes,
        compiler_params=_cparams("arbitrary"),
        name="inproj",
    )(x_all, mod_l, n1g, wts["w_in"], tabs["ca"], tabs["sa"], tabs["cs"], tabs["ss"], tabs["cq"], tabs["sq"],
      wts["alog"], wts["dtb"], wts["gq"], wts["gkv"], wts["wuq"], wts["wukk"], wts["pkr"], wts["wukv"])


def _mla_cache_kernel(ckv_ref, kr_ref, wukk_ref, pkr_ref, wukv_ref, k_ref, v_ref):
    ckvb = ckv_ref[...].astype(BF16)
    krb = kr_ref[...].astype(BF16)
    k_ref[...] = (jnp.dot(ckvb, wukk_ref[...], preferred_element_type=F32)
                  + jnp.dot(krb, pkr_ref[...], preferred_element_type=F32)).astype(BF16)
    v_ref[...] = jnp.dot(ckvb, wukv_ref[...], preferred_element_type=F32).astype(BF16)


def _mla_cache(ckv_c, kr_c, wts):
    n = ckv_c.shape[0]
    return pl.pallas_call(
        _mla_cache_kernel,
        grid=(n // PAST_LEN,),
        in_specs=[pl.BlockSpec((PAST_LEN, B_KV_LORA), lambda i: (i, 0)),
                  pl.BlockSpec((PAST_LEN, LANES), lambda i: (i, 0)),
                  _resident((B_KV_LORA, 1024)), _resident((LANES, 1024)), _resident((B_KV_LORA, 512))],
        out_specs=[pl.BlockSpec((PAST_LEN, 1024), lambda i: (i, 0)),
                   pl.BlockSpec((PAST_LEN, 512), lambda i: (i, 0))],
        out_shape=[jax.ShapeDtypeStruct((n, 1024), BF16), jax.ShapeDtypeStruct((n, 512), BF16)],
        compiler_params=_cparams("arbitrary"),
        name="mla_cache",
    )(ckv_c, kr_c, wts["wukk"], wts["pkr"], wts["wukv"])


TQ = 256
A_WIN_KEYS = TQ + 2 * A_WINDOW


def _softmax_pv(scores, values, extra_logit=None):
    m = scores[0].max(axis=-1, keepdims=True)
    for s in scores[1:]:
        m = jnp.maximum(m, s.max(axis=-1, keepdims=True))
    if extra_logit is not None:
        m = jnp.maximum(m, extra_logit)
    den = None
    acc = None
    for s, v in zip(scores, values):
        e = jnp.exp(s - m)
        d = e.sum(axis=-1, keepdims=True)
        den = d if den is None else den + d
        pv = jnp.dot(e.astype(BF16), v, preferred_element_type=F32)
        acc = pv if acc is None else acc + pv
    if extra_logit is not None:
        den = den + jnp.exp(extra_logit - m)
    return acc / den


def _attn_a_kernel(sink_ref, q_ref, k1_ref, v1_ref, *rest, latent):
    if latent:
        k2_ref, v2_ref, o_ref = rest
        qt = pl.program_id(1)
        t0 = qt * TQ
        start = pl.multiple_of(jnp.clip(t0 - A_WINDOW, 0, DEC_SEQ - A_WIN_KEYS), A_WINDOW)
        k1 = k1_ref[pl.ds(start, A_WIN_KEYS), :].astype(BF16)
        v1 = v1_ref[pl.ds(start, A_WIN_KEYS), :].astype(BF16)
        qpos = t0 + lax.broadcasted_iota(jnp.int32, (TQ, A_WIN_KEYS), 0)
        kpos = start + lax.broadcasted_iota(jnp.int32, (TQ, A_WIN_KEYS), 1)
        valid = jnp.abs(qpos - kpos) <= A_WINDOW
        k2 = k2_ref[0].astype(BF16)
        v2 = v2_ref[0].astype(BF16)
    else:
        (o_ref,) = rest
        k1 = k1_ref[...].astype(BF16)
        v1 = v1_ref[...].astype(BF16)
    lane = lax.broadcasted_iota(jnp.int32, (TQ, LANES), 1)
    low = lane < A_HEAD_DIM
    for g in range(A_GROUP):
        qc = q_ref[:, g * LANES:(g + 1) * LANES]
        outs = []
        for hk in range(A_KV_HEADS):
            qz = jnp.where(low if hk == 0 else ~low, qc, jnp.zeros_like(qc))
            s1 = lax.dot_general(qz, k1, (((1,), (1,)), ((), ())), preferred_element_type=F32)
            sink = sink_ref[hk * A_GROUP + g]
            if latent:
                s1 = jnp.where(valid, s1, NEG_INF)
                s2 = lax.dot_general(qz, k2, (((1,), (1,)), ((), ())), preferred_element_type=F32)
                outs.append(_softmax_pv([s1, s2], [v1, v2], sink))
            else:
                outs.append(_softmax_pv([s1], [v1], sink))
        o_ref[:, g * LANES:(g + 1) * LANES] = jnp.where(low, outs[0], outs[1]).astype(BF16)


def _attn_a_ctx(sink, aq, ak, av):
    nb = T_CTX // TQ
    return pl.pallas_call(
        functools.partial(_attn_a_kernel, latent=False),
        grid=(nb,),
        in_specs=[pl.BlockSpec(memory_space=pltpu.SMEM),
                  pl.BlockSpec((TQ, 512), lambda b: (b, 0)),
                  pl.BlockSpec((SEQ, LANES), lambda b: (b, 0)),
                  pl.BlockSpec((SEQ, LANES), lambda b: (b, 0))],
        out_specs=pl.BlockSpec((TQ, 512), lambda b: (b, 0)),
        out_shape=jax.ShapeDtypeStruct((T_CTX, 512), BF16),
        compiler_params=_cparams("arbitrary"),
        name="attn_a_ctx",
    )(sink, aq, ak, av)


def _attn_a_lat(sink, aq, ak, av, ck, cv):
    nq = DEC_SEQ // TQ
    ctx_tiles = T_CTX // TQ
    seq_blocks = T_CTX // DEC_SEQ
    return pl.pallas_call(
        functools.partial(_attn_a_kernel, latent=True),
        grid=(DEC_BATCH, nq),
        in_specs=[pl.BlockSpec(memory_space=pltpu.SMEM),
                  pl.BlockSpec((TQ, 512), lambda b, i: (ctx_tiles + b * nq + i, 0)),
                  pl.BlockSpec((DEC_SEQ, LANES), lambda b, i: (seq_blocks + b, 0)),
                  pl.BlockSpec((DEC_SEQ, LANES), lambda b, i: (seq_blocks + b, 0)),
                  pl.BlockSpec((1, PAST_LEN, LANES), lambda b, i: (b, 0, 0)),
                  pl.BlockSpec((1, PAST_LEN, LANES), lambda b, i: (b, 0, 0))],
        out_specs=pl.BlockSpec((TQ, 512), lambda b, i: (b * nq + i, 0)),
        out_shape=jax.ShapeDtypeStruct((T_LAT, 512), BF16),
        compiler_params=_cparams("arbitrary", "arbitrary"),
        name="attn_a_lat",
    )(sink, aq, ak, av, ck, cv)


def _mla_kernel(q_ref, k1_ref, v1_ref, *rest, latent):
    if latent:
        k2_ref, v2_ref, o_ref = rest
    else:
        (o_ref,) = rest
    lane = lax.broadcasted_iota(jnp.int32, (TQ, LANES), 1)
    low = lane < B_V
    nt = (((1,), (1,)), ((), ()))
    for j in range(B_HEADS // 2):
        outs = []
        for sub in range(2):
            hh = 2 * j + sub
            hs = slice(hh * LANES, (hh + 1) * LANES)
            vs = slice(j * LANES, (j + 1) * LANES)
            qh = q_ref[:, hs]
            s1 = lax.dot_general(qh, k1_ref[:, hs], nt, preferred_element_type=F32) * MLA_SCALE
            if latent:
                s2 = lax.dot_general(qh, k2_ref[:, hs], nt, preferred_element_type=F32) * MLA_SCALE
                outs.append(_softmax_pv([s1, s2], [v1_ref[:, vs], v2_ref[:, vs]]))
            else:
                outs.append(_softmax_pv([s1], [v1_ref[:, vs]]))
        o_ref[:, j * LANES:(j + 1) * LANES] = jnp.where(low, outs[0], outs[1]).astype(BF16)


def _mla_ctx(bq, bk, bv):
    nb = T_CTX // TQ
    return pl.pallas_call(
        functools.partial(_mla_kernel, latent=False),
        grid=(nb,),
        in_specs=[pl.BlockSpec((TQ, 1024), lambda b: (b, 0)),
                  pl.BlockSpec((SEQ, 1024), lambda b: (b, 0)),
                  pl.BlockSpec((SEQ, 512), lambda b: (b, 0))],
        out_specs=pl.BlockSpec((TQ, 512), lambda b: (b, 0)),
        out_shape=jax.ShapeDtypeStruct((T_CTX, 512), BF16),
        compiler_params=_cparams("arbitrary"),
        name="mla_ctx",
    )(bq, bk, bv)


def _mla_lat(bq, bk, bv, k2, v2):
    nq = DEC_SEQ // TQ
    ctx_tiles = T_CTX // TQ
    seq_blocks = T_CTX // DEC_SEQ
    return pl.pallas_call(
        functools.partial(_mla_kernel, latent=True),
        grid=(DEC_BATCH, nq),
        in_specs=[pl.BlockSpec((TQ, 1024), lambda b, i: (ctx_tiles + b * nq + i, 0)),
                  pl.BlockSpec((DEC_SEQ, 1024), lambda b, i: (seq_blocks + b, 0)),
                  pl.BlockSpec((DEC_SEQ, 512), lambda b, i: (seq_blocks + b, 0)),
                  pl.BlockSpec((PAST_LEN, 1024), lambda b, i: (b, 0)),
                  pl.BlockSpec((PAST_LEN, 512), lambda b, i: (b, 0))],
        out_specs=pl.BlockSpec((TQ, 512), lambda b, i: (b * nq + i, 0)),
        out_shape=jax.ShapeDtypeStruct((T_LAT, 512), BF16),
        compiler_params=_cparams("arbitrary", "arbitrary"),
        name="mla_lat",
    )(bq, bk, bv, k2, v2)


def _gdn_kernel(xq_ref, xk_ref, xv_ref, wq_ref, wk_ref, wv_ref, gcol_ref, grow_ref, cz_ref, ong_ref, *rest,
                n, has_s0, emit_state):
    rest = list(rest)
    s0_ref = rest.pop(0) if has_s0 else None
    oc_ref = rest.pop(0)
    sfin_ref = rest.pop(0) if emit_state else None
    q_s, k_s, v_s, u_s, w_s, qk_s, qd_s, kdt_s, ge_s, o_s = rest
    nc = n // GDN_CHUNK
    head = pl.program_id(1)
    C = GDN_CHUNK

    row_n = lax.broadcasted_iota(jnp.int32, (n, LANES), 0)

    def conv_silu(x_ref, w_ref):
        x = x_ref[...]
        w = w_ref[...]
        xp = jnp.where(row_n == 0, 0.0, pltpu.roll(x, 1, 0))
        xn = jnp.where(row_n == n - 1, 0.0, pltpu.roll(x, n - 1, 0))
        return _silu(xp * w[0:1] + x * w[1:2] + xn * w[2:3])

    def l2n(x):
        return x * lax.rsqrt(jnp.sum(x * x, axis=-1, keepdims=True) + EPS)

    q_s[...] = l2n(conv_silu(xq_ref, wq_ref)) * (C_DK ** -0.5)
    k_s[...] = l2n(conv_silu(xk_ref, wk_ref))
    v_s[...] = conv_silu(xv_ref, wv_ref)
    o_s[...] = jnp.zeros_like(o_s)

    lane_c = lax.broadcasted_iota(jnp.int32, (C, LANES), 1)
    ri = lax.broadcasted_iota(jnp.int32, (C, C), 0)
    ci = lax.broadcasted_iota(jnp.int32, (C, C), 1)
    eye = (ri == ci).astype(F32)
    same_block = {}
    s = 2
    while s <= C:
        same_block[s] = (ri // s) == (ci // s)
        s *= 2

    def prep(it, carry):
        items = []
        for gi in range(group):
            r = it * group + gi
            rows = pl.ds(pl.multiple_of(r * C, C), C)
            q = q_s[rows, :]
            k = k_s[rows, :]
            v = v_s[rows, :]
            gtile = gcol_ref[rows, :]
            qk_raw = _dot_nt(q, k)
            for d in range(2):
                j = d * C_HEADS + head
                gc = jnp.sum(jnp.where(lane_c == SM_A + j, gtile, 0.0), axis=1, keepdims=True)
                beta = jnp.sum(jnp.where(lane_c == SM_B + j, gtile, 0.0), axis=1, keepdims=True)
                grow = grow_ref[j, r]
                incl = (ri >= ci) if d == 0 else (ri <= ci)
                strict = (ri > ci) if d == 0 else (ri < ci)
                decay = jnp.where(incl, jnp.exp(jnp.where(incl, gc - grow, 0.0)), 0.0)
                kb = k * beta
                a = jnp.where(strict, _dot_nt(kb, k) * decay, 0.0)
                g_end = gc[C - 1:C, :] if d == 0 else gc[0:1, :]
                egc = jnp.exp(gc)
                qk_s[d, rows, :] = (qk_raw * decay).astype(BF16)
                qd_s[d, rows, :] = (q * egc).astype(BF16)
                kdt_s[d, rows, :] = (k * jnp.exp(g_end - gc)).T.astype(BF16)
                ge_s[d, r] = jnp.broadcast_to(jnp.exp(g_end), (8, LANES))
                items.append((d, rows, a, jnp.concatenate([v * beta, kb * egc], axis=1)))
        ts = [eye - jnp.where(same_block[2], a, 0.0) for (_, _, a, _) in items]
        s = 2
        while s < C:
            level = same_block[2 * s] & ~same_block[s]
            xs = [_dot(t, jnp.where(level, a, 0.0)) for t, (_, _, a, _) in zip(ts, items)]
            ts = [t - _dot(x, t) for t, x in zip(ts, xs)]
            s *= 2
        for t, (d, rows, _, rhs) in zip(ts, items):
            uw = _dot(t, rhs)
            u_s[d, rows, :] = uw[:, :C_DV]
            w_s[d, rows, :] = uw[:, C_DV:].astype(BF16)
        return carry

    group = min(nc, GDN_PREP_GROUP)
    lax.fori_loop(0, nc // group, prep, 0)

    if has_s0:
        s_init = (s0_ref[0, 0, 0, 0], s0_ref[0, 0, 1, 0])
    else:
        s_init = (jnp.zeros((C_DK, C_DV), F32), jnp.zeros((C_DK, C_DV), F32))

    def step(i, states):
        new = []
        for d in range(2):
            r = i if d == 0 else nc - 1 - i
            rows = pl.ds(pl.multiple_of(r * C, C), C)
            s = states[d]
            sb = s.astype(BF16)
            vn = u_s[d, rows, :] - jnp.dot(w_s[d, rows, :], sb, preferred_element_type=F32)
            vnb = vn.astype(BF16)
            o = (jnp.dot(qd_s[d, rows, :], sb, preferred_element_type=F32)
                 + jnp.dot(qk_s[d, rows, :], vnb, preferred_element_type=F32))
            o_s[rows, :] += o
            new.append(s * ge_s[d, r][0:1, :] + jnp.dot(kdt_s[d, rows, :], vnb, preferred_element_type=F32))
        return tuple(new)

    s_f, s_b = lax.fori_loop(0, nc, step, s_init)
    if emit_state:
        sfin_ref[0, 0, 0] = s_f
        sfin_ref[0, 1, 0] = s_b

    o = o_s[...]
    oc_ref[...] = (_rms(o) * ong_ref[...] * _silu(cz_ref[...])).astype(BF16)


def _gdn(cqkv, conv_w, gcol, grow3, cz, ong, state_c, layer, *, latent):
    n = DEC_SEQ if latent else SEQ
    nseq = DEC_BATCH if latent else BATCH
    seq0 = (T_CTX // n) if latent else 0
    nc = n // GDN_CHUNK
    col = lambda base: pl.BlockSpec((n, LANES), lambda b, h: (seq0 + b, base + h))
    wcol = lambda base: pl.BlockSpec((3, LANES), lambda b, h: (0, base + h))
    in_specs = [col(0), col(4), col(8), wcol(0), wcol(4), wcol(8),
                pl.BlockSpec((n, LANES), lambda b, h: (seq0 + b, 0)),
                pl.BlockSpec((16, nc, 1, LANES), lambda b, h: (0, seq0 + b, 0, 0)),
                pl.BlockSpec((n, LANES), lambda b, h: (seq0 + b, h)),
                pl.BlockSpec((1, LANES), lambda b, h: (0, 0))]
    args = [cqkv, cqkv, cqkv, conv_w, conv_w, conv_w, gcol, grow3, cz, ong]
    out_specs = [pl.BlockSpec((n, LANES), lambda b, h: (b, h))]
    out_shape = [jax.ShapeDtypeStruct((nseq * n, 512), BF16)]
    if latent:
        in_specs.append(pl.BlockSpec((1, 1, 2, 1, C_DK, C_DV), lambda b, h: (b, layer, 0, h, 0, 0)))
        args.append(state_c)
    else:
        out_specs.append(pl.BlockSpec((1, 2, 1, C_DK, C_DV), lambda b, h: (b, 0, h, 0, 0)))
        out_shape.append(jax.ShapeDtypeStruct((nseq, 2, C_HEADS, C_DK, C_DV), F32))
    scratch = [pltpu.VMEM((n, LANES), F32), pltpu.VMEM((n, LANES), F32), pltpu.VMEM((n, LANES), F32),
               pltpu.VMEM((2, n, LANES), F32), pltpu.VMEM((2, n, LANES), BF16), pltpu.VMEM((2, n, LANES), BF16),
               pltpu.VMEM((2, n, LANES), BF16), pltpu.VMEM((2, n, LANES), BF16),
               pltpu.VMEM((2, nc, 8, LANES), F32), pltpu.VMEM((n, LANES), F32)]
    res = pl.pallas_call(
        functools.partial(_gdn_kernel, n=n, has_s0=latent, emit_state=not latent),
        grid=(nseq, C_HEADS),
        in_specs=in_specs,
        out_specs=out_specs,
        out_shape=out_shape,
        scratch_shapes=scratch,
        compiler_params=_cparams("arbitrary", "arbitrary"),
        name="gdn_lat" if latent else "gdn_ctx",
    )(*args)
    return res


def _merge_ffn_kernel(x_ref, oa_ref, ob_ref, oc_ref, gates_ref, mod_ref, n2g_ref, wbr_ref, wout_ref,
                      wff1_ref, wff2_ref, fg_ref, o_ref, *, final):
    mod = mod_ref[0]
    z = None
    for j, br in enumerate((oa_ref, ob_ref, oc_ref)):
        y = jnp.dot(br[...], wbr_ref[j], preferred_element_type=F32)
        t = _sigmoid(gates_ref[:, j * D_MODEL:(j + 1) * D_MODEL]) * y
        z = t if z is None else z + t
    x = x_ref[...] + mod[2:3] * jnp.dot(z.astype(BF16), wout_ref[...], preferred_element_type=F32)
    h = (_rms(x) * n2g_ref[...] * (1.0 + mod[4:5]) + mod[3:4]).astype(BF16)
    f = jnp.maximum(jnp.dot(h, wff1_ref[...], preferred_element_type=F32), 0.0)
    f = (f * f).astype(BF16)
    x = x + mod[5:6] * jnp.dot(f, wff2_ref[...], preferred_element_type=F32)
    if final:
        x = _rms(x) * fg_ref[...]
    o_ref[...] = x


def _merge_ffn(x_all, oa, ob, oc, gates, mod_l, n2g, wts, final_g, *, final):
    tile = lambda w: pl.BlockSpec((TM, w), lambda i: (i, 0))
    return pl.pallas_call(
        functools.partial(_merge_ffn_kernel, final=final),
        grid=(N_TILES,),
        in_specs=[tile(D_MODEL), tile(512), tile(512), tile(512), tile(3072),
                  pl.BlockSpec((1, 6, D_MODEL), lambda i: (_group_of_tile(i), 0, 0)),
                  _resident((1, D_MODEL)),
                  _resident((3, BRANCH_W, D_MODEL)), _resident((D_MODEL, D_MODEL)),
                  _resident((D_MODEL, D_FF)), _resident((D_FF, D_MODEL)), _resident((1, D_MODEL))],
        out_specs=tile(D_MODEL),
        out_shape=jax.ShapeDtypeStruct((T_ALL, D_MODEL), F32),
        compiler_params=_cparams("arbitrary"),
        name="merge_ffn",
    )(x_all, oa, ob, oc, gates, mod_l, n2g, wts["wbr"], wts["wout"], wts["wff1"], wts["wff2"], final_g)


def _rope_tables(r):
    half, quarter = r // 2, r // 4
    t = jnp.arange(DEC_SEQ)
    inv_freq = jnp.power(ROPE_BASE, -jnp.arange(quarter, dtype=F32) / quarter)
    cs, ss = [], []
    for pos in (t // GRID_W, t % GRID_W):
        ang = pos.astype(F32)[:, None] * inv_freq[None, :]
        c, s = jnp.cos(ang), jnp.sin(ang)
        cs += [c, c]
        ss += [-s, s]
    return jnp.concatenate(cs, axis=1), jnp.concatenate(ss, axis=1)


def _tables():
    def with_identity(c, s):
        one = jnp.ones((TM, LANES), F32)
        return jnp.concatenate([one, c], axis=0), jnp.concatenate([0.0 * one, s], axis=0)

    c64, s64 = _rope_tables(A_HEAD_DIM)
    ca, sa = with_identity(jnp.tile(c64, (1, 2)), jnp.tile(s64, (1, 2)))
    c32, s32 = _rope_tables(B_ROPE)
    ones = lambda w: jnp.ones((DEC_SEQ, w), F32)
    zeros = lambda w: jnp.zeros((DEC_SEQ, w), F32)
    cs, ss = with_identity(jnp.concatenate([c32, ones(96)], axis=1), jnp.concatenate([s32, zeros(96)], axis=1))
    cq, sq = with_identity(jnp.concatenate([ones(64), c32, ones(32)], axis=1),
                           jnp.concatenate([zeros(64), s32, zeros(32)], axis=1))
    return {"ca": ca, "sa": sa, "cs": cs, "ss": ss, "cq": cq, "sq": sq}


_A_HEAD_PERM = [hk * A_GROUP + g for g in range(A_GROUP) for hk in range(A_KV_HEADS)]


def _layer_weights(l, w_in, b_gq, b_w_uq, b_gkv, b_w_ukv, c_a_log, c_dt_bias, w_branch, w_out, w_ff1, w_ff2):
    w = w_in[l]
    offs = [0]
    for s in (512, 128, 128, B_Q_LORA, B_KV_LORA, B_ROPE, 1536, 8, 8, 512, 3072):
        offs.append(offs[-1] + s)
    a_q, a_k, a_v, b_cq, b_ckv, b_kr, c_qkv, c_a, c_b, c_z, gates = (
        w[:, offs[i]:offs[i + 1]] for i in range(11))
    a_q = a_q.reshape(D_MODEL, A_HEADS, A_HEAD_DIM)[:, jnp.array(_A_HEAD_PERM)].reshape(D_MODEL, 512)
    small = jnp.concatenate([b_kr, c_a, c_b, jnp.zeros((D_MODEL, LANES - 48), F32)], axis=1)
    w_r = jnp.concatenate([a_q, a_k, a_v, b_cq, b_ckv, small, c_qkv, c_z, gates], axis=1).astype(BF16)

    pad_lane = lambda v: jnp.zeros((1, LANES), F32).at[0, SM_A:SM_A + 8].set(v.reshape(8))
    uq = b_w_uq[l].reshape(B_Q_LORA, B_HEADS, B_NOPE + B_ROPE)
    uq = jnp.pad(uq, ((0, 0), (0, 0), (0, LANES - B_NOPE - B_ROPE))).reshape(B_Q_LORA, 1024).astype(BF16)
    ukv = b_w_ukv[l].reshape(B_KV_LORA, B_HEADS, B_NOPE + B_V)
    ukk = jnp.pad(ukv[:, :, :B_NOPE], ((0, 0), (0, 0), (0, LANES - B_NOPE))).reshape(B_KV_LORA, 1024).astype(BF16)
    ukvv = ukv[:, :, B_NOPE:].reshape(B_KV_LORA, 512).astype(BF16)
    src = jnp.arange(LANES)[:, None]
    dst = jnp.arange(1024)[None, :]
    pkr = ((dst % LANES == B_NOPE + src) & (src < B_ROPE)).astype(BF16)

    wbr = w_branch[l]
    wbr_a = wbr[0].reshape(A_HEADS, A_HEAD_DIM, D_MODEL)[jnp.array(_A_HEAD_PERM)].reshape(BRANCH_W, D_MODEL)
    wbr = jnp.stack([wbr_a, wbr[1], wbr[2]]).astype(BF16)
    return {"w_in": w_r, "alog": pad_lane(c_a_log[l]), "dtb": pad_lane(c_dt_bias[l]),
            "gq": b_gq[l].reshape(1, -1), "gkv": b_gkv[l].reshape(1, -1),
            "wuq": uq, "wukk": ukk, "pkr": pkr, "wukv": ukvv,
            "wbr": wbr, "wout": w_out[l].astype(BF16), "wff1": w_ff1[l].astype(BF16),
            "wff2": w_ff2[l].astype(BF16)}


def kernel(x_prompt, x_sample, cache_a_k, cache_a_v, cache_b_ckv, cache_b_krope, state_c, c, c_ctx, norm1_g,
           norm2_g, w_ada, b_ada, w_in, a_sink, b_gq, b_w_uq, b_gkv, b_w_ukv, c_conv_w, c_a_log, c_dt_bias,
           c_onorm_g, w_branch, w_out, w_ff1, w_ff2, final_g):
    x_all = jnp.concatenate([x_prompt.reshape(T_CTX, D_MODEL), x_sample.reshape(T_LAT, D_MODEL)], axis=0)
    cond8 = jnp.concatenate([c_ctx[None, :], c, jnp.zeros((N_GROUPS - 1 - DEC_BATCH, D_MODEL), F32)], axis=0)
    mod = _adaln(cond8, w_ada, b_ada).reshape(DEPTH, N_GROUPS, 6, D_MODEL)
    tabs = _tables()
    fg = final_g.reshape(1, D_MODEL)

    ak_l, av_l, ckv_l, kr_l, sc_l = [], [], [], [], []
    for l in range(DEPTH):
        wts = _layer_weights(l, w_in, b_gq, b_w_uq, b_gkv, b_w_ukv, c_a_log, c_dt_bias, w_branch, w_out,
                             w_ff1, w_ff2)
        (aq, ak, av, bq, bk, bv, ckv, small, cqkv, gcol, grow, cz, gates) = _inproj(
            x_all, mod[l], norm1_g[l].reshape(1, D_MODEL), wts, tabs)

        sink = a_sink[l]
        oa_ctx = _attn_a_ctx(sink, aq, ak, av)
        oa_lat = _attn_a_lat(sink, aq, ak, av,
                             cache_a_k[:, l].reshape(DEC_BATCH, PAST_LEN, LANES),
                             cache_a_v[:, l].reshape(DEC_BATCH, PAST_LEN, LANES))

        kr_c = jnp.pad(cache_b_krope[:, l].reshape(DEC_BATCH * PAST_LEN, B_ROPE), ((0, 0), (0, LANES - B_ROPE)))
        k2, v2 = _mla_cache(cache_b_ckv[:, l].reshape(DEC_BATCH * PAST_LEN, B_KV_LORA), kr_c, wts)
        ob_ctx = _mla_ctx(bq, bk, bv)
        ob_lat = _mla_lat(bq, bk, bv, k2, v2)

        grow3 = grow.reshape(16, T_ALL // GDN_CHUNK, 1, GDN_CHUNK)
        ong = c_onorm_g[l].reshape(1, C_DV)
        oc_ctx, s_fin = _gdn(cqkv, c_conv_w[l], gcol, grow3, cz, ong, state_c, l, latent=False)
        (oc_lat,) = _gdn(cqkv, c_conv_w[l], gcol, grow3, cz, ong, state_c, l, latent=True)

        cat = lambda a, b: jnp.concatenate([a, b], axis=0)
        x_all = _merge_ffn(x_all, cat(oa_ctx, oa_lat), cat(ob_ctx, ob_lat), cat(oc_ctx, oc_lat), gates,
                           mod[l], norm2_g[l].reshape(1, D_MODEL), wts, fg, final=(l == DEPTH - 1))

        ak_l.append(ak[:T_CTX].reshape(BATCH, SEQ, A_KV_HEADS, A_HEAD_DIM))
        av_l.append(av[:T_CTX].reshape(BATCH, SEQ, A_KV_HEADS, A_HEAD_DIM))
        ckv_l.append(ckv[:T_CTX].reshape(BATCH, SEQ, B_KV_LORA))
        kr_l.append(small[:T_CTX, :B_ROPE].reshape(BATCH, SEQ, B_ROPE))
        sc_l.append(s_fin)

    y_prompt = x_all[:T_CTX].reshape(BATCH, SEQ, D_MODEL)
    y_sample = x_all[T_CTX:].reshape(DEC_BATCH, DEC_SEQ, D_MODEL)
    return (y_prompt, y_sample, jnp.stack(ak_l, axis=1), jnp.stack(av_l, axis=1), jnp.stack(ckv_l, axis=1),
            jnp.stack(kr_l, axis=1), jnp.stack(sc_l, axis=1))
```

```python
import functools
import math

import jax
import jax.numpy as jnp
from jax import lax
from jax.experimental import pallas as pl
from jax.experimental.pallas import tpu as pltpu

F32 = jnp.float32
BF16 = jnp.bfloat16

D_MODEL = 1024
BATCH = 16
SEQ = 256
DEPTH = 2
DEC_BATCH = 4
DEC_SEQ = 2048
PAST_LEN = 256
GRID_W = 64
ROPE_BASE = 10000.0
EPS = 1e-6
NEG_INF = -1e30

A_HEADS = 8
A_KV_HEADS = 2
A_GROUP = 4
A_HEAD_DIM = 64
A_WINDOW = 128
A_SCALE = A_HEAD_DIM ** -0.5
B_HEADS = 8
B_Q_LORA = 384
B_KV_LORA = 256
B_NOPE = 64
B_ROPE = 32
B_V = 64
MLA_SCALE = (B_NOPE + B_ROPE) ** -0.5
C_HEADS = 4
C_DK = 128
C_DV = 128
D_FF = 4 * D_MODEL
BRANCH_W = 512

LANES = 128

T_CTX = BATCH * SEQ
T_LAT = DEC_BATCH * DEC_SEQ
T_ALL = T_CTX + T_LAT
TM = 256
N_CTX_TILES = T_CTX // TM
LAT_TILES_PER_SEQ = DEC_SEQ // TM
N_TILES = T_ALL // TM
N_GROUPS = 8
GDN_CHUNK = 128
GDN_PREP_CHAINS = 8
GDN_HEADS_PER_STEP_CTX = 4
GDN_HEADS_PER_STEP_LAT = 2

O_AQ, O_AK, O_AV, O_BCQ, O_BCKV, O_SMALL, O_CQKV, O_CZ, O_GATES, W_IN_COLS = (
    0, 512, 640, 768, 1152, 1408, 1536, 3072, 3584, 6656)
SM_KR, SM_A, SM_B = 0, 32, 40

VMEM_LIMIT = 56 * 1024 * 1024


def _cparams(*sem):
    return pltpu.CompilerParams(dimension_semantics=sem, vmem_limit_bytes=VMEM_LIMIT)


def _resident(shape):
    nd = len(shape)
    return pl.BlockSpec(shape, lambda *_: (0,) * nd, pipeline_mode=pl.Buffered(1))


def _group_of_tile(i):
    return jnp.where(i < N_CTX_TILES, 0, 1 + (i - N_CTX_TILES) // LAT_TILES_PER_SEQ)


def _rope_block_of_tile(i):
    return jnp.where(i < N_CTX_TILES, 0, 1 + (i - N_CTX_TILES) % LAT_TILES_PER_SEQ)


def _stream_specs(width, n_arrays, tile0=0):
    if n_arrays == 1:
        return [pl.BlockSpec((TM, width), lambda i: (tile0 + i, 0))]
    return [pl.BlockSpec((TM, width), lambda i: (jnp.minimum(tile0 + i, N_CTX_TILES - 1), 0)),
            pl.BlockSpec((TM, width), lambda i: (jnp.maximum(tile0 + i - N_CTX_TILES, 0), 0))]


def _pick_stream(refs, tile0=0):
    if len(refs) == 1:
        return refs[0][...]
    return jnp.where(tile0 + pl.program_id(0) < N_CTX_TILES, refs[0][...], refs[1][...])


def _rms(x):
    return x * lax.rsqrt(jnp.mean(x * x, axis=-1, keepdims=True) + EPS)


def _sigmoid(x):
    return 1.0 / (1.0 + jnp.exp(-x))


def _silu(x):
    return x * _sigmoid(x)


def _softplus(x):
    return jnp.maximum(x, 0.0) + jnp.log(1.0 + jnp.exp(-jnp.abs(x)))


def _dot(a, b):
    return jnp.dot(a.astype(BF16), b.astype(BF16), preferred_element_type=F32)


def _dot_nt(a, b):
    return lax.dot_general(a.astype(BF16), b.astype(BF16), (((1,), (1,)), ((), ())),
                           preferred_element_type=F32)


def _swap_groups(x, q):
    w = x.shape[1]
    lane = lax.broadcasted_iota(jnp.int32, x.shape, 1)
    even = ((lane // q) % 2) == 0
    return jnp.where(even, pltpu.roll(x, w - q, 1), pltpu.roll(x, q, 1))


def _rope(x, cos, sin, q):
    return x * cos + _swap_groups(x, q) * sin


ADA_TN = 1536


def _adaln_kernel(c_ref, w_ref, b_ref, o_ref):
    c = c_ref[...]
    o_ref[0] = jnp.dot(_silu(c), w_ref[0], preferred_element_type=F32) + b_ref[0]


def _adaln(cond8, w_ada, b_ada):
    n = 6 * D_MODEL
    return pl.pallas_call(
        _adaln_kernel,
        grid=(DEPTH, n // ADA_TN),
        in_specs=[
            pl.BlockSpec((N_GROUPS, D_MODEL), lambda l, j: (0, 0)),
            pl.BlockSpec((1, D_MODEL, ADA_TN), lambda l, j: (l, 0, j)),
            pl.BlockSpec((1, 1, ADA_TN), lambda l, j: (l, 0, j)),
        ],
        out_specs=pl.BlockSpec((1, N_GROUPS, ADA_TN), lambda l, j: (l, 0, j)),
        out_shape=jax.ShapeDtypeStruct((DEPTH, N_GROUPS, n), F32),
        compiler_params=_cparams("arbitrary", "arbitrary"),
        name="adaln",
    )(cond8, w_ada, b_ada.reshape(DEPTH, 1, n))


def _chunk_scan(g, forward):
    n = g.shape[0]
    pos = lax.broadcasted_iota(jnp.int32, g.shape, 0) % GDN_CHUNK
    s = 1
    while s < GDN_CHUNK:
        if forward:
            g = g + jnp.where(pos >= s, pltpu.roll(g, s, 0), 0.0)
        else:
            g = g + jnp.where(pos < GDN_CHUNK - s, pltpu.roll(g, n - s, 0), 0.0)
        s *= 2
    return g


def _inproj_kernel(*refs, n_x):
    x_refs, refs = refs[:n_x], refs[n_x:]
    (mod_ref, n1g_ref, w_ref, ca_ref, sa_ref, cs_ref, ss_ref, cq_ref, sq_ref,
     alog_ref, dtb_ref, gq_ref, gkv_ref, wuq_ref, wukk_ref, pkr_ref, wukv_ref,
     aq_ref, ak_ref, av_ref, bq_ref, bk_ref, bv_ref, ckv_ref, small_ref,
     cqkv_ref, gcol_ref, grow_ref, cz_ref, gates_ref) = refs
    mod = mod_ref[0]
    h = _rms(_pick_stream(x_refs)) * n1g_ref[...] * (1.0 + mod[1:2]) + mod[0:1]
    hb = h.astype(BF16)

    def proj(lo, hi):
        return jnp.dot(hb, w_ref[:, lo:hi], preferred_element_type=F32)

    ca, sa = ca_ref[...], sa_ref[...]
    a_all = proj(O_AQ, O_BCQ)
    for c in range(4):
        q = a_all[:, c * LANES:(c + 1) * LANES]
        aq_ref[:, c * LANES:(c + 1) * LANES] = (_rope(q, ca, sa, 16) * A_SCALE).astype(BF16)
    ak_ref[...] = _rope(a_all[:, O_AK:O_AV], ca, sa, 16)
    av_ref[...] = a_all[:, O_AV:O_BCQ]

    b_all = proj(O_BCQ, O_CQKV)
    cq = (_rms(b_all[:, :B_Q_LORA]) * gq_ref[...]).astype(BF16)
    cqc, sqc = cq_ref[...], sq_ref[...]
    q_all = jnp.dot(cq, wuq_ref[...], preferred_element_type=F32)
    for hh in range(B_HEADS):
        qh = q_all[:, hh * LANES:(hh + 1) * LANES]
        bq_ref[:, hh * LANES:(hh + 1) * LANES] = _rope(qh, cqc, sqc, 8).astype(BF16)

    ckv = _rms(b_all[:, B_Q_LORA:B_Q_LORA + B_KV_LORA]) * gkv_ref[...]
    ckv_ref[...] = ckv
    ckvb = ckv.astype(BF16)
    small = b_all[:, B_Q_LORA + B_KV_LORA:]
    small_r = _rope(small, cs_ref[...], ss_ref[...], 8)
    small_ref[...] = small_r
    krb = small_r.astype(BF16)
    bk_ref[...] = (jnp.dot(ckvb, wukk_ref[...], preferred_element_type=F32)
                   + jnp.dot(krb, pkr_ref[...], preferred_element_type=F32)).astype(BF16)
    bv_ref[...] = jnp.dot(ckvb, wukv_ref[...], preferred_element_type=F32).astype(BF16)

    for c in range(3):
        cqkv_ref[:, c * 512:(c + 1) * 512] = proj(O_CQKV + c * 512, O_CQKV + (c + 1) * 512)
    cz_ref[...] = proj(O_CZ, O_GATES)
    for c in range(3):
        gates_ref[:, c * D_MODEL:(c + 1) * D_MODEL] = proj(O_GATES + c * D_MODEL, O_GATES + (c + 1) * D_MODEL)

    graw = -jnp.exp(alog_ref[...]) * _softplus(small + dtb_ref[...])
    lane = lax.broadcasted_iota(jnp.int32, small.shape, 1)
    gcol = jnp.where(lane < SM_A + C_HEADS, _chunk_scan(graw, True), _chunk_scan(graw, False))
    gcol = jnp.where((lane >= SM_A) & (lane < SM_B), gcol,
                     jnp.where((lane >= SM_B) & (lane < SM_B + 2 * C_HEADS), _sigmoid(small), 0.0))
    gcol_ref[...] = gcol
    grow_ref[...] = gcol.T[SM_A:SM_A + 16, :]


def _inproj(xs, mod_l, n1g, wts, tabs):
    tile = lambda w: pl.BlockSpec((TM, w), lambda i: (i, 0))
    rope_spec = pl.BlockSpec((TM, LANES), lambda i: (_rope_block_of_tile(i), 0))
    out_shapes = [
        jax.ShapeDtypeStruct((T_ALL, 512), BF16),
        jax.ShapeDtypeStruct((T_ALL, 128), F32),
        jax.ShapeDtypeStruct((T_ALL, 128), F32),
        jax.ShapeDtypeStruct((T_ALL, 1024), BF16),
        jax.ShapeDtypeStruct((T_ALL, 1024), BF16),
        jax.ShapeDtypeStruct((T_ALL, 512), BF16),
        jax.ShapeDtypeStruct((T_ALL, 256), F32),
        jax.ShapeDtypeStruct((T_ALL, 128), F32),
        jax.ShapeDtypeStruct((T_ALL, 1536), F32),
        jax.ShapeDtypeStruct((T_ALL, 128), F32),
        jax.ShapeDtypeStruct((16, T_ALL), F32),
        jax.ShapeDtypeStruct((T_ALL, 512), F32),
        jax.ShapeDtypeStruct((T_ALL, 3072), F32),
    ]
    out_specs = [tile(512), tile(128), tile(128), tile(1024), tile(1024), tile(512), tile(256), tile(128),
                 tile(1536), tile(128), pl.BlockSpec((16, TM), lambda i: (0, i)), tile(512), tile(3072)]
    in_specs = _stream_specs(D_MODEL, len(xs)) + [
        pl.BlockSpec((1, 6, D_MODEL), lambda i: (_group_of_tile(i), 0, 0)),
        _resident((1, D_MODEL)),
        _resident((D_MODEL, W_IN_COLS)),
        rope_spec, rope_spec, rope_spec, rope_spec, rope_spec, rope_spec,
        _resident((1, LANES)), _resident((1, LANES)),
        _resident((1, B_Q_LORA)), _resident((1, B_KV_LORA)),
        _resident((B_Q_LORA, 1024)), _resident((B_KV_LORA, 1024)), _resident((LANES, 1024)),
        _resident((B_KV_LORA, 512)),
    ]
    return pl.pallas_call(
        functools.partial(_inproj_kernel, n_x=len(xs)),
        grid=(N_TILES,),
        in_specs=in_specs,
        out_specs=out_specs,
        out_shape=out_shapes,
        compiler_params=_cparams("arbitrary"),
        name="inproj",
    )(*xs, mod_l, n1g, wts["w_in"], tabs["ca"], tabs["sa"], tabs["cs"], tabs["ss"], tabs["cq"], tabs["sq"],
      wts["alog"], wts["dtb"], wts["gq"], wts["gkv"], wts["wuq"], wts["wukk"], wts["pkr"], wts["wukv"])


def _mla_cache_kernel(ckv_ref, kr_ref, wukk_ref, pkr_ref, wukv_ref, k_ref, v_ref):
    ckvb = ckv_ref[...].astype(BF16)
    krb = kr_ref[...].astype(BF16)
    k_ref[...] = (jnp.dot(ckvb, wukk_ref[...], preferred_element_type=F32)
                  + jnp.dot(krb, pkr_ref[...], preferred_element_type=F32)).astype(BF16)
    v_ref[...] = jnp.dot(ckvb, wukv_ref[...], preferred_element_type=F32).astype(BF16)


def _mla_cache(ckv_c, kr_c, wts):
    n = ckv_c.shape[0]
    return pl.pallas_call(
        _mla_cache_kernel,
        grid=(n // PAST_LEN,),
        in_specs=[pl.BlockSpec((PAST_LEN, B_KV_LORA), lambda i: (i, 0)),
                  pl.BlockSpec((PAST_LEN, LANES), lambda i: (i, 0)),
                  _resident((B_KV_LORA, 1024)), _resident((LANES, 1024)), _resident((B_KV_LORA, 512))],
        out_specs=[pl.BlockSpec((PAST_LEN, 1024), lambda i: (i, 0)),
                   pl.BlockSpec((PAST_LEN, 512), lambda i: (i, 0))],
        out_shape=[jax.ShapeDtypeStruct((n, 1024), BF16), jax.ShapeDtypeStruct((n, 512), BF16)],
        compiler_params=_cparams("arbitrary"),
        name="mla_cache",
    )(ckv_c, kr_c, wts["wukk"], wts["pkr"], wts["wukv"])


TQ = 256
A_WIN_KEYS = TQ + 2 * A_WINDOW


def _softmax_pv(scores, values, extra_logit=None):
    m = scores[0].max(axis=-1, keepdims=True)
    for s in scores[1:]:
        m = jnp.maximum(m, s.max(axis=-1, keepdims=True))
    if extra_logit is not None:
        m = jnp.maximum(m, extra_logit)
    den = None
    acc = None
    for s, v in zip(scores, values):
        e = jnp.exp(s - m)
        d = e.sum(axis=-1, keepdims=True)
        den = d if den is None else den + d
        pv = jnp.dot(e.astype(BF16), v, preferred_element_type=F32)
        acc = pv if acc is None else acc + pv
    if extra_logit is not None:
        den = den + jnp.exp(extra_logit - m)
    return acc / den


def _attn_a_kernel(sink_ref, q_ref, k1_ref, v1_ref, *rest, latent):
    if latent:
        k2_ref, v2_ref, o_ref = rest
        qt = pl.program_id(1)
        t0 = qt * TQ
        start = pl.multiple_of(jnp.clip(t0 - A_WINDOW, 0, DEC_SEQ - A_WIN_KEYS), A_WINDOW)
        k1 = k1_ref[pl.ds(start, A_WIN_KEYS), :].astype(BF16)
        v1 = v1_ref[pl.ds(start, A_WIN_KEYS), :].astype(BF16)
        qpos = t0 + lax.broadcasted_iota(jnp.int32, (TQ, A_WIN_KEYS), 0)
        kpos = start + lax.broadcasted_iota(jnp.int32, (TQ, A_WIN_KEYS), 1)
        valid = jnp.abs(qpos - kpos) <= A_WINDOW
        k2 = k2_ref[0].astype(BF16)
        v2 = v2_ref[0].astype(BF16)
    else:
        (o_ref,) = rest
        k1 = k1_ref[...].astype(BF16)
        v1 = v1_ref[...].astype(BF16)
    lane = lax.broadcasted_iota(jnp.int32, (TQ, LANES), 1)
    low = lane < A_HEAD_DIM
    for g in range(A_GROUP):
        qc = q_ref[:, g * LANES:(g + 1) * LANES]
        outs = []
        for hk in range(A_KV_HEADS):
            qz = jnp.where(low if hk == 0 else ~low, qc, jnp.zeros_like(qc))
            s1 = lax.dot_general(qz, k1, (((1,), (1,)), ((), ())), preferred_element_type=F32)
            sink = sink_ref[hk * A_GROUP + g]
            if latent:
                s1 = jnp.where(valid, s1, NEG_INF)
                s2 = lax.dot_general(qz, k2, (((1,), (1,)), ((), ())), preferred_element_type=F32)
                outs.append(_softmax_pv([s1, s2], [v1, v2], sink))
            else:
                outs.append(_softmax_pv([s1], [v1], sink))
        o_ref[:, g * LANES:(g + 1) * LANES] = jnp.where(low, outs[0], outs[1]).astype(BF16)


def _attn_a_ctx(sink, aq, ak, av):
    nb = T_CTX // TQ
    return pl.pallas_call(
        functools.partial(_attn_a_kernel, latent=False),
        grid=(nb,),
        in_specs=[pl.BlockSpec(memory_space=pltpu.SMEM),
                  pl.BlockSpec((TQ, 512), lambda b: (b, 0)),
                  pl.BlockSpec((SEQ, LANES), lambda b: (b, 0)),
                  pl.BlockSpec((SEQ, LANES), lambda b: (b, 0))],
        out_specs=pl.BlockSpec((TQ, 512), lambda b: (b, 0)),
        out_shape=jax.ShapeDtypeStruct((T_CTX, 512), BF16),
        compiler_params=_cparams("arbitrary"),
        name="attn_a_ctx",
    )(sink, aq, ak, av)


def _attn_a_lat(sink, aq, ak, av, ck, cv):
    nq = DEC_SEQ // TQ
    ctx_tiles = T_CTX // TQ
    seq_blocks = T_CTX // DEC_SEQ
    return pl.pallas_call(
        functools.partial(_attn_a_kernel, latent=True),
        grid=(DEC_BATCH, nq),
        in_specs=[pl.BlockSpec(memory_space=pltpu.SMEM),
                  pl.BlockSpec((TQ, 512), lambda b, i: (ctx_tiles + b * nq + i, 0)),
                  pl.BlockSpec((DEC_SEQ, LANES), lambda b, i: (seq_blocks + b, 0)),
                  pl.BlockSpec((DEC_SEQ, LANES), lambda b, i: (seq_blocks + b, 0)),
                  pl.BlockSpec((1, PAST_LEN, LANES), lambda b, i: (b, 0, 0)),
                  pl.BlockSpec((1, PAST_LEN, LANES), lambda b, i: (b, 0, 0))],
        out_specs=pl.BlockSpec((TQ, 512), lambda b, i: (b * nq + i, 0)),
        out_shape=jax.ShapeDtypeStruct((T_LAT, 512), BF16),
        compiler_params=_cparams("arbitrary", "arbitrary"),
        name="attn_a_lat",
    )(sink, aq, ak, av, ck, cv)


def _mla_kernel(q_ref, k1_ref, v1_ref, *rest, latent):
    if latent:
        k2_ref, v2_ref, o_ref = rest
    else:
        (o_ref,) = rest
    lane = lax.broadcasted_iota(jnp.int32, (TQ, LANES), 1)
    low = lane < B_V
    nt = (((1,), (1,)), ((), ()))
    for j in range(B_HEADS // 2):
        outs = []
        for sub in range(2):
            hh = 2 * j + sub
            hs = slice(hh * LANES, (hh + 1) * LANES)
            vs = slice(j * LANES, (j + 1) * LANES)
            qh = q_ref[:, hs]
            s1 = lax.dot_general(qh, k1_ref[:, hs], nt, preferred_element_type=F32) * MLA_SCALE
            if latent:
                s2 = lax.dot_general(qh, k2_ref[:, hs], nt, preferred_element_type=F32) * MLA_SCALE
                outs.append(_softmax_pv([s1, s2], [v1_ref[:, vs], v2_ref[:, vs]]))
            else:
                outs.append(_softmax_pv([s1], [v1_ref[:, vs]]))
        o_ref[:, j * LANES:(j + 1) * LANES] = jnp.where(low, outs[0], outs[1]).astype(BF16)


def _mla_ctx(bq, bk, bv):
    nb = T_CTX // TQ
    return pl.pallas_call(
        functools.partial(_mla_kernel, latent=False),
        grid=(nb,),
        in_specs=[pl.BlockSpec((TQ, 1024), lambda b: (b, 0)),
                  pl.BlockSpec((SEQ, 1024), lambda b: (b, 0)),
                  pl.BlockSpec((SEQ, 512), lambda b: (b, 0))],
        out_specs=pl.BlockSpec((TQ, 512), lambda b: (b, 0)),
        out_shape=jax.ShapeDtypeStruct((T_CTX, 512), BF16),
        compiler_params=_cparams("arbitrary"),
        name="mla_ctx",
    )(bq, bk, bv)


def _mla_lat(bq, bk, bv, k2, v2):
    nq = DEC_SEQ // TQ
    ctx_tiles = T_CTX // TQ
    seq_blocks = T_CTX // DEC_SEQ
    return pl.pallas_call(
        functools.partial(_mla_kernel, latent=True),
        grid=(DEC_BATCH, nq),
        in_specs=[pl.BlockSpec((TQ, 1024), lambda b, i: (ctx_tiles + b * nq + i, 0)),
                  pl.BlockSpec((DEC_SEQ, 1024), lambda b, i: (seq_blocks + b, 0)),
                  pl.BlockSpec((DEC_SEQ, 512), lambda b, i: (seq_blocks + b, 0)),
                  pl.BlockSpec((PAST_LEN, 1024), lambda b, i: (b, 0)),
                  pl.BlockSpec((PAST_LEN, 512), lambda b, i: (b, 0))],
        out_specs=pl.BlockSpec((TQ, 512), lambda b, i: (b * nq + i, 0)),
        out_shape=jax.ShapeDtypeStruct((T_LAT, 512), BF16),
        compiler_params=_cparams("arbitrary", "arbitrary"),
        name="mla_lat",
    )(bq, bk, bv, k2, v2)


def _gdn_kernel(xq_ref, xk_ref, xv_ref, wq_ref, wk_ref, wv_ref, gcol_ref, grow_ref, cz_ref, ong_ref, *rest,
                n, hps, has_s0, emit_state):
    rest = list(rest)
    s0_ref = rest.pop(0) if has_s0 else None
    oc_ref = rest.pop(0)
    sfin_ref = rest.pop(0) if emit_state else None
    q_s, k_s, v_s, u_s, w_s, qk_s, qd_s, kdt_s, ge_s, st_s, o_s = rest
    nc = n // GDN_CHUNK
    head0 = pl.program_id(1) * hps
    C = GDN_CHUNK
    width = hps * LANES
    hl = lambda hh: slice(hh * LANES, (hh + 1) * LANES)
    chains = [(d, hh) for d in range(2) for hh in range(hps)]

    row_n = lax.broadcasted_iota(jnp.int32, (n, width), 0)

    def conv_silu(x_ref, w_ref):
        x = x_ref[...]
        w = w_ref[...]
        xp = jnp.where(row_n == 0, 0.0, pltpu.roll(x, 1, 0))
        xn = jnp.where(row_n == n - 1, 0.0, pltpu.roll(x, n - 1, 0))
        return _silu(xp * w[0:1] + x * w[1:2] + xn * w[2:3])

    def l2n(x):
        return x * lax.rsqrt(jnp.sum(x * x, axis=-1, keepdims=True) + EPS)

    qc = conv_silu(xq_ref, wq_ref)
    kc = conv_silu(xk_ref, wk_ref)
    for hh in range(hps):
        q_s[:, hl(hh)] = l2n(qc[:, hl(hh)]) * (C_DK ** -0.5)
        k_s[:, hl(hh)] = l2n(kc[:, hl(hh)])
    v_s[...] = conv_silu(xv_ref, wv_ref)
    o_s[...] = jnp.zeros_like(o_s)

    lane_c = lax.broadcasted_iota(jnp.int32, (C, LANES), 1)
    ri = lax.broadcasted_iota(jnp.int32, (C, C), 0)
    ci = lax.broadcasted_iota(jnp.int32, (C, C), 1)
    eye = (ri == ci).astype(F32)
    same_block = {}
    s = 2
    while s <= C:
        same_block[s] = (ri // s) == (ci // s)
        s *= 2

    def prep(it, carry):
        items = []
        for gi in range(group):
            r = it * group + gi
            rows = pl.ds(pl.multiple_of(r * C, C), C)
            gtile = gcol_ref[rows, :]
            for hh in range(hps):
                q = q_s[rows, hl(hh)]
                k = k_s[rows, hl(hh)]
                v = v_s[rows, hl(hh)]
                qk_raw = _dot_nt(q, k)
                for d in range(2):
                    slot = d * hps + hh
                    j = d * C_HEADS + head0 + hh
                    gc = jnp.sum(jnp.where(lane_c == SM_A + j, gtile, 0.0), axis=1, keepdims=True)
                    beta = jnp.sum(jnp.where(lane_c == SM_B + j, gtile, 0.0), axis=1, keepdims=True)
                    grow = grow_ref[j, r]
                    incl = (ri >= ci) if d == 0 else (ri <= ci)
                    strict = (ri > ci) if d == 0 else (ri < ci)
                    decay = jnp.where(incl, jnp.exp(jnp.where(incl, gc - grow, 0.0)), 0.0)
                    kb = k * beta
                    a = jnp.where(strict, _dot_nt(kb, k) * decay, 0.0)
                    g_end = gc[C - 1:C, :] if d == 0 else gc[0:1, :]
                    egc = jnp.exp(gc)
                    qk_s[slot, rows, :] = (qk_raw * decay).astype(BF16)
                    qd_s[slot, rows, :] = (q * egc).astype(BF16)
                    kdt_s[slot, rows, :] = (k * jnp.exp(g_end - gc)).T.astype(BF16)
                    ge_s[slot, r] = jnp.broadcast_to(jnp.exp(g_end), (8, LANES))
                    items.append((slot, rows, a, jnp.concatenate([v * beta, kb * egc], axis=1)))
        ts = [eye - jnp.where(same_block[2], a, 0.0) for (_, _, a, _) in items]
        s = 2
        while s < C:
            level = same_block[2 * s] & ~same_block[s]
            xs = [_dot(t, jnp.where(level, a, 0.0)) for t, (_, _, a, _) in zip(ts, items)]
            ts = [t - _dot(x, t) for t, x in zip(ts, xs)]
            s *= 2
        for t, (slot, rows, _, rhs) in zip(ts, items):
            uw = _dot(t, rhs)
            u_s[slot, rows, :] = uw[:, :C_DV]
            w_s[slot, rows, :] = uw[:, C_DV:].astype(BF16)
        return carry

    group = max(1, min(nc, GDN_PREP_CHAINS // (2 * hps)))
    lax.fori_loop(0, nc // group, prep, 0)

    for d, hh in chains:
        if has_s0:
            st_s[d * hps + hh] = s0_ref[0, 0, d, hh]
        else:
            st_s[d * hps + hh] = jnp.zeros((C_DK, C_DV), F32)

    def step(i, carry):
        rows = [pl.ds(pl.multiple_of(i * C, C), C), pl.ds(pl.multiple_of((nc - 1 - i) * C, C), C)]
        chunk = [i, nc - 1 - i]
        slots = [d * hps + hh for d, hh in chains]
        st = [st_s[c] for c in slots]
        stb = [s_.astype(BF16) for s_ in st]
        ws = [jnp.dot(w_s[c, rows[d], :], sb, preferred_element_type=F32)
              for c, (d, _), sb in zip(slots, chains, stb)]
        vnb = [(u_s[c, rows[d], :] - x).astype(BF16) for c, (d, _), x in zip(slots, chains, ws)]
        for c, (d, hh), sb, vb, s_ in zip(slots, chains, stb, vnb, st):
            o = (jnp.dot(qd_s[c, rows[d], :], sb, preferred_element_type=F32)
                 + jnp.dot(qk_s[c, rows[d], :], vb, preferred_element_type=F32))
            o_s[rows[d], hl(hh)] += o
            st_s[c] = s_ * ge_s[c, chunk[d]][0:1, :] + jnp.dot(kdt_s[c, rows[d], :], vb,
                                                                preferred_element_type=F32)
        return carry

    lax.fori_loop(0, nc, step, 0)
    if emit_state:
        for d, hh in chains:
            sfin_ref[0, d, hh] = st_s[d * hps + hh]

    for hh in range(hps):
        o = o_s[:, hl(hh)]
        oc_ref[:, hl(hh)] = (_rms(o) * ong_ref[...] * _silu(cz_ref[:, hl(hh)])).astype(BF16)


def _gdn(cqkv, conv_w, gcol, grow3, cz, ong, state_c, layer, *, latent):
    n = DEC_SEQ if latent else SEQ
    nseq = DEC_BATCH if latent else BATCH
    hps = GDN_HEADS_PER_STEP_LAT if latent else GDN_HEADS_PER_STEP_CTX
    seq0 = (T_CTX // n) if latent else 0
    nc = n // GDN_CHUNK
    width = hps * LANES
    sect = C_HEADS // hps
    col = lambda base: pl.BlockSpec((n, width), lambda b, h: (seq0 + b, base + h))
    wcol = lambda base: pl.BlockSpec((3, width), lambda b, h: (0, base + h))
    in_specs = [col(0), col(sect), col(2 * sect), wcol(0), wcol(sect), wcol(2 * sect),
                pl.BlockSpec((n, LANES), lambda b, h: (seq0 + b, 0)),
                pl.BlockSpec((16, nc, 1, LANES), lambda b, h: (0, seq0 + b, 0, 0)),
                pl.BlockSpec((n, width), lambda b, h: (seq0 + b, h)),
                pl.BlockSpec((1, LANES), lambda b, h: (0, 0))]
    args = [cqkv, cqkv, cqkv, conv_w, conv_w, conv_w, gcol, grow3, cz, ong]
    out_specs = [pl.BlockSpec((n, width), lambda b, h: (b, h))]
    out_shape = [jax.ShapeDtypeStruct((nseq * n, 512), BF16)]
    if latent:
        in_specs.append(pl.BlockSpec((1, 1, 2, hps, C_DK, C_DV), lambda b, h: (b, layer, 0, h, 0, 0)))
        args.append(state_c)
    else:
        out_specs.append(pl.BlockSpec((1, 2, hps, C_DK, C_DV), lambda b, h: (b, 0, h, 0, 0)))
        out_shape.append(jax.ShapeDtypeStruct((nseq, 2, C_HEADS, C_DK, C_DV), F32))
    nch = 2 * hps
    scratch = [pltpu.VMEM((n, width), F32), pltpu.VMEM((n, width), F32), pltpu.VMEM((n, width), F32),
               pltpu.VMEM((nch, n, LANES), F32), pltpu.VMEM((nch, n, LANES), BF16),
               pltpu.VMEM((nch, n, LANES), BF16), pltpu.VMEM((nch, n, LANES), BF16),
               pltpu.VMEM((nch, n, LANES), BF16), pltpu.VMEM((nch, nc, 8, LANES), F32),
               pltpu.VMEM((nch, C_DK, C_DV), F32), pltpu.VMEM((n, width), F32)]
    res = pl.pallas_call(
        functools.partial(_gdn_kernel, n=n, hps=hps, has_s0=latent, emit_state=not latent),
        grid=(nseq, C_HEADS // hps),
        in_specs=in_specs,
        out_specs=out_specs,
        out_shape=out_shape,
        scratch_shapes=scratch,
        compiler_params=_cparams("arbitrary", "arbitrary"),
        name="gdn_lat" if latent else "gdn_ctx",
    )(*args)
    return res


def _merge_ffn_kernel(*refs, n_x, tile0, final):
    x_refs, refs = refs[:n_x], refs[n_x:]
    branch_refs, refs = refs[:6], refs[6:]
    gates_ref, mod_ref, n2g_ref, wbr_ref, wout_ref, wff1_ref, wff2_ref, fg_ref, o_ref = refs
    mod = mod_ref[0]
    z = None
    for j in range(3):
        y = jnp.dot(_pick_stream(branch_refs[2 * j:2 * j + 2], tile0), wbr_ref[j], preferred_element_type=F32)
        t = _sigmoid(gates_ref[:, j * D_MODEL:(j + 1) * D_MODEL]) * y
        z = t if z is None else z + t
    x = (_pick_stream(x_refs, tile0)
         + mod[2:3] * jnp.dot(z.astype(BF16), wout_ref[...], preferred_element_type=F32))
    h = (_rms(x) * n2g_ref[...] * (1.0 + mod[4:5]) + mod[3:4]).astype(BF16)
    f = jnp.maximum(jnp.dot(h, wff1_ref[...], preferred_element_type=F32), 0.0)
    f = (f * f).astype(BF16)
    x = x + mod[5:6] * jnp.dot(f, wff2_ref[...], preferred_element_type=F32)
    if final:
        x = _rms(x) * fg_ref[...]
    o_ref[...] = x


def _merge_ffn(xs, branches, gates, mod_l, n2g, wts, final_g, *, tile0, n_tiles, final):
    in_specs = _stream_specs(D_MODEL, len(xs), tile0)
    for _ in range(3):
        in_specs += _stream_specs(BRANCH_W, 2, tile0)
    in_specs += [pl.BlockSpec((TM, 3072), lambda i: (tile0 + i, 0)),
                 pl.BlockSpec((1, 6, D_MODEL), lambda i: (_group_of_tile(tile0 + i), 0, 0)),
                 _resident((1, D_MODEL)),
                 _resident((3, BRANCH_W, D_MODEL)), _resident((D_MODEL, D_MODEL)),
                 _resident((D_MODEL, D_FF)), _resident((D_FF, D_MODEL)), _resident((1, D_MODEL))]
    return pl.pallas_call(
        functools.partial(_merge_ffn_kernel, n_x=len(xs), tile0=tile0, final=final),
        grid=(n_tiles,),
        in_specs=in_specs,
        out_specs=pl.BlockSpec((TM, D_MODEL), lambda i: (i, 0)),
        out_shape=jax.ShapeDtypeStruct((n_tiles * TM, D_MODEL), F32),
        compiler_params=_cparams("arbitrary"),
        name="merge_ffn",
    )(*xs, *branches, gates, mod_l, n2g, wts["wbr"], wts["wout"], wts["wff1"], wts["wff2"], final_g)


def _rope_tables(r):
    half, quarter = r // 2, r // 4
    t = jnp.arange(DEC_SEQ)
    inv_freq = jnp.power(ROPE_BASE, -jnp.arange(quarter, dtype=F32) / quarter)
    cs, ss = [], []
    for pos in (t // GRID_W, t % GRID_W):
        ang = pos.astype(F32)[:, None] * inv_freq[None, :]
        c, s = jnp.cos(ang), jnp.sin(ang)
        cs += [c, c]
        ss += [-s, s]
    return jnp.concatenate(cs, axis=1), jnp.concatenate(ss, axis=1)


def _tables():
    def with_identity(c, s):
        one = jnp.ones((TM, LANES), F32)
        return jnp.concatenate([one, c], axis=0), jnp.concatenate([0.0 * one, s], axis=0)

    c64, s64 = _rope_tables(A_HEAD_DIM)
    ca, sa = with_identity(jnp.tile(c64, (1, 2)), jnp.tile(s64, (1, 2)))
    c32, s32 = _rope_tables(B_ROPE)
    ones = lambda w: jnp.ones((DEC_SEQ, w), F32)
    zeros = lambda w: jnp.zeros((DEC_SEQ, w), F32)
    cs, ss = with_identity(jnp.concatenate([c32, ones(96)], axis=1), jnp.concatenate([s32, zeros(96)], axis=1))
    cq, sq = with_identity(jnp.concatenate([ones(64), c32, ones(32)], axis=1),
                           jnp.concatenate([zeros(64), s32, zeros(32)], axis=1))
    return {"ca": ca, "sa": sa, "cs": cs, "ss": ss, "cq": cq, "sq": sq}


_A_HEAD_PERM = [hk * A_GROUP + g for g in range(A_GROUP) for hk in range(A_KV_HEADS)]


def _layer_weights(l, w_in, b_gq, b_w_uq, b_gkv, b_w_ukv, c_a_log, c_dt_bias, w_branch, w_out, w_ff1, w_ff2):
    w = w_in[l]
    offs = [0]
    for s in (512, 128, 128, B_Q_LORA, B_KV_LORA, B_ROPE, 1536, 8, 8, 512, 3072):
        offs.append(offs[-1] + s)
    a_q, a_k, a_v, b_cq, b_ckv, b_kr, c_qkv, c_a, c_b, c_z, gates = (
        w[:, offs[i]:offs[i + 1]] for i in range(11))
    a_q = a_q.reshape(D_MODEL, A_HEADS, A_HEAD_DIM)[:, jnp.array(_A_HEAD_PERM)].reshape(D_MODEL, 512)
    small = jnp.concatenate([b_kr, c_a, c_b, jnp.zeros((D_MODEL, LANES - 48), F32)], axis=1)
    w_r = jnp.concatenate([a_q, a_k, a_v, b_cq, b_ckv, small, c_qkv, c_z, gates], axis=1).astype(BF16)

    pad_lane = lambda v: jnp.zeros((1, LANES), F32).at[0, SM_A:SM_A + 8].set(v.reshape(8))
    uq = b_w_uq[l].reshape(B_Q_LORA, B_HEADS, B_NOPE + B_ROPE)
    uq = jnp.pad(uq, ((0, 0), (0, 0), (0, LANES - B_NOPE - B_ROPE))).reshape(B_Q_LORA, 1024).astype(BF16)
    ukv = b_w_ukv[l].reshape(B_KV_LORA, B_HEADS, B_NOPE + B_V)
    ukk = jnp.pad(ukv[:, :, :B_NOPE], ((0, 0), (0, 0), (0, LANES - B_NOPE))).reshape(B_KV_LORA, 1024).astype(BF16)
    ukvv = ukv[:, :, B_NOPE:].reshape(B_KV_LORA, 512).astype(BF16)
    src = jnp.arange(LANES)[:, None]
    dst = jnp.arange(1024)[None, :]
    pkr = ((dst % LANES == B_NOPE + src) & (src < B_ROPE)).astype(BF16)

    wbr = w_branch[l]
    wbr_a = wbr[0].reshape(A_HEADS, A_HEAD_DIM, D_MODEL)[jnp.array(_A_HEAD_PERM)].reshape(BRANCH_W, D_MODEL)
    wbr = jnp.stack([wbr_a, wbr[1], wbr[2]]).astype(BF16)
    return {"w_in": w_r, "alog": pad_lane(c_a_log[l]), "dtb": pad_lane(c_dt_bias[l]),
            "gq": b_gq[l].reshape(1, -1), "gkv": b_gkv[l].reshape(1, -1),
            "wuq": uq, "wukk": ukk, "pkr": pkr, "wukv": ukvv,
            "wbr": wbr, "wout": w_out[l].astype(BF16), "wff1": w_ff1[l].astype(BF16),
            "wff2": w_ff2[l].astype(BF16)}


def kernel(x_prompt, x_sample, cache_a_k, cache_a_v, cache_b_ckv, cache_b_krope, state_c, c, c_ctx, norm1_g,
           norm2_g, w_ada, b_ada, w_in, a_sink, b_gq, b_w_uq, b_gkv, b_w_ukv, c_conv_w, c_a_log, c_dt_bias,
           c_onorm_g, w_branch, w_out, w_ff1, w_ff2, final_g):
    xs = (x_prompt.reshape(T_CTX, D_MODEL), x_sample.reshape(T_LAT, D_MODEL))
    cond8 = jnp.concatenate([c_ctx[None, :], c, jnp.zeros((N_GROUPS - 1 - DEC_BATCH, D_MODEL), F32)], axis=0)
    mod = _adaln(cond8, w_ada, b_ada).reshape(DEPTH, N_GROUPS, 6, D_MODEL)
    tabs = _tables()
    fg = final_g.reshape(1, D_MODEL)

    ak_l, av_l, ckv_l, kr_l, sc_l = [], [], [], [], []
    for l in range(DEPTH):
        wts = _layer_weights(l, w_in, b_gq, b_w_uq, b_gkv, b_w_ukv, c_a_log, c_dt_bias, w_branch, w_out,
                             w_ff1, w_ff2)
        (aq, ak, av, bq, bk, bv, ckv, small, cqkv, gcol, grow, cz, gates) = _inproj(
            xs, mod[l], norm1_g[l].reshape(1, D_MODEL), wts, tabs)

        sink = a_sink[l]
        oa_ctx = _attn_a_ctx(sink, aq, ak, av)
        oa_lat = _attn_a_lat(sink, aq, ak, av,
                             cache_a_k[:, l].reshape(DEC_BATCH, PAST_LEN, LANES),
                             cache_a_v[:, l].reshape(DEC_BATCH, PAST_LEN, LANES))

        kr_c = jnp.pad(cache_b_krope[:, l].reshape(DEC_BATCH * PAST_LEN, B_ROPE), ((0, 0), (0, LANES - B_ROPE)))
        k2, v2 = _mla_cache(cache_b_ckv[:, l].reshape(DEC_BATCH * PAST_LEN, B_KV_LORA), kr_c, wts)
        ob_ctx = _mla_ctx(bq, bk, bv)
        ob_lat = _mla_lat(bq, bk, bv, k2, v2)

        grow3 = grow.reshape(16, T_ALL // GDN_CHUNK, 1, GDN_CHUNK)
        ong = c_onorm_g[l].reshape(1, C_DV)
        oc_ctx, s_fin = _gdn(cqkv, c_conv_w[l], gcol, grow3, cz, ong, state_c, l, latent=False)
        (oc_lat,) = _gdn(cqkv, c_conv_w[l], gcol, grow3, cz, ong, state_c, l, latent=True)

        branches = (oa_ctx, oa_lat, ob_ctx, ob_lat, oc_ctx, oc_lat)
        merge = functools.partial(_merge_ffn, xs, branches, gates, mod[l], norm2_g[l].reshape(1, D_MODEL), wts, fg)
        if l < DEPTH - 1:
            xs = (merge(tile0=0, n_tiles=N_TILES, final=False),)
        else:
            xs = (merge(tile0=0, n_tiles=N_CTX_TILES, final=True),
                  merge(tile0=N_CTX_TILES, n_tiles=N_TILES - N_CTX_TILES, final=True))

        ak_l.append(ak[:T_CTX].reshape(BATCH, SEQ, A_KV_HEADS, A_HEAD_DIM))
        av_l.append(av[:T_CTX].reshape(BATCH, SEQ, A_KV_HEADS, A_HEAD_DIM))
        ckv_l.append(ckv[:T_CTX].reshape(BATCH, SEQ, B_KV_LORA))
        kr_l.append(small[:T_CTX, :B_ROPE].reshape(BATCH, SEQ, B_ROPE))
        sc_l.append(s_fin)

    y_prompt = xs[0].reshape(BATCH, SEQ, D_MODEL)
    y_sample = xs[1].reshape(DEC_BATCH, DEC_SEQ, D_MODEL)
    return (y_prompt, y_sample, jnp.stack(ak_l, axis=1), jnp.stack(av_l, axis=1), jnp.stack(ckv_l, axis=1),
            jnp.stack(kr_l, axis=1), jnp.stack(sc_l, axis=1))
```

```python
import functools
import math

import jax
import jax.numpy as jnp
from jax import lax
from jax.experimental import pallas as pl
from jax.experimental.pallas import tpu as pltpu

F32 = jnp.float32
BF16 = jnp.bfloat16

D_MODEL = 1024
BATCH = 16
SEQ = 256
DEPTH = 2
DEC_BATCH = 4
DEC_SEQ = 2048
PAST_LEN = 256
GRID_W = 64
ROPE_BASE = 10000.0
EPS = 1e-6
NEG_INF = -1e30

A_HEADS = 8
A_KV_HEADS = 2
A_GROUP = 4
A_HEAD_DIM = 64
A_WINDOW = 128
A_SCALE = A_HEAD_DIM ** -0.5
B_HEADS = 8
B_Q_LORA = 384
B_KV_LORA = 256
B_NOPE = 64
B_ROPE = 32
B_V = 64
MLA_SCALE = (B_NOPE + B_ROPE) ** -0.5
LOG2E = math.log2(math.e)
C_HEADS = 4
C_DK = 128
C_DV = 128
D_FF = 4 * D_MODEL
BRANCH_W = 512

LANES = 128

T_CTX = BATCH * SEQ
T_LAT = DEC_BATCH * DEC_SEQ
T_ALL = T_CTX + T_LAT
TM = 256
N_CTX_TILES = T_CTX // TM
LAT_TILES_PER_SEQ = DEC_SEQ // TM
N_TILES = T_ALL // TM
N_GROUPS = 8
GDN_CHUNK = 128
GDN_PREP_CHAINS = 8
GDN_HEADS_PER_STEP_CTX = 4
GDN_HEADS_PER_STEP_LAT = 2

O_AQ, O_AK, O_AV, O_BCQ, O_BCKV, O_SMALL, O_CQKV, O_CZ, O_GATES, W_IN_COLS = (
    0, 512, 640, 768, 1152, 1408, 1536, 3072, 3584, 6656)
SM_KR, SM_A, SM_B = 0, 32, 40

VMEM_LIMIT = 56 * 1024 * 1024


def _cparams(*sem):
    return pltpu.CompilerParams(dimension_semantics=sem, vmem_limit_bytes=VMEM_LIMIT)


def _resident(shape):
    nd = len(shape)
    return pl.BlockSpec(shape, lambda *_: (0,) * nd, pipeline_mode=pl.Buffered(1))


def _resident_layer(shape, layer):
    nd = len(shape)
    return pl.BlockSpec((None,) + tuple(shape), lambda *_: (layer,) + (0,) * nd, pipeline_mode=pl.Buffered(1))


def _group_of_tile(i):
    return jnp.where(i < N_CTX_TILES, 0, 1 + (i - N_CTX_TILES) // LAT_TILES_PER_SEQ)


def _rope_block_of_tile(i):
    return jnp.where(i < N_CTX_TILES, 0, 1 + (i - N_CTX_TILES) % LAT_TILES_PER_SEQ)


def _mod_spec(layer, tile0=0):
    return pl.BlockSpec((None, 1, 6, D_MODEL), lambda i: (layer, _group_of_tile(tile0 + i), 0, 0))


def _stream_specs(width, n_arrays, tile0=0):
    if n_arrays == 1:
        return [pl.BlockSpec((TM, width), lambda i: (tile0 + i, 0))]
    return [pl.BlockSpec((TM, width), lambda i: (jnp.minimum(tile0 + i, N_CTX_TILES - 1), 0)),
            pl.BlockSpec((TM, width), lambda i: (jnp.maximum(tile0 + i - N_CTX_TILES, 0), 0))]


def _pick_stream(refs, tile0=0):
    if len(refs) == 1:
        return refs[0][...]
    return jnp.where(tile0 + pl.program_id(0) < N_CTX_TILES, refs[0][...], refs[1][...])


def _rms(x):
    return x * lax.rsqrt(jnp.mean(x * x, axis=-1, keepdims=True) + EPS)


def _sigmoid(x):
    return 1.0 / (1.0 + jnp.exp(-x))


def _silu(x):
    return x * _sigmoid(x)


def _softplus(x):
    return jnp.maximum(x, 0.0) + jnp.log(1.0 + jnp.exp(-jnp.abs(x)))


def _dot(a, b):
    return jnp.dot(a.astype(BF16), b.astype(BF16), preferred_element_type=F32)


def _dot_nt(a, b):
    return lax.dot_general(a.astype(BF16), b.astype(BF16), (((1,), (1,)), ((), ())),
                           preferred_element_type=F32)


def _swap_groups(x, q):
    w = x.shape[1]
    lane = lax.broadcasted_iota(jnp.int32, x.shape, 1)
    even = ((lane // q) % 2) == 0
    return jnp.where(even, pltpu.roll(x, w - q, 1), pltpu.roll(x, q, 1))


def _rope(x, cos, sin, q):
    return x * cos + _swap_groups(x, q) * sin


ADA_TN = 1536


def _adaln_kernel(c_ref, w_ref, b_ref, o_ref):
    c = c_ref[...]
    o_ref[0] = jnp.dot(_silu(c), w_ref[0], preferred_element_type=F32) + b_ref[0]


def _adaln(cond8, w_ada, b_ada):
    n = 6 * D_MODEL
    return pl.pallas_call(
        _adaln_kernel,
        grid=(DEPTH, n // ADA_TN),
        in_specs=[
            pl.BlockSpec((N_GROUPS, D_MODEL), lambda l, j: (0, 0)),
            pl.BlockSpec((1, D_MODEL, ADA_TN), lambda l, j: (l, 0, j)),
            pl.BlockSpec((1, 1, ADA_TN), lambda l, j: (l, 0, j)),
        ],
        out_specs=pl.BlockSpec((1, N_GROUPS, ADA_TN), lambda l, j: (l, 0, j)),
        out_shape=jax.ShapeDtypeStruct((DEPTH, N_GROUPS, n), F32),
        compiler_params=_cparams("arbitrary", "arbitrary"),
        name="adaln",
    )(cond8, w_ada, b_ada.reshape(DEPTH, 1, n))


def _chunk_scan(g, forward):
    n = g.shape[0]
    pos = lax.broadcasted_iota(jnp.int32, g.shape, 0) % GDN_CHUNK
    s = 1
    while s < GDN_CHUNK:
        if forward:
            g = g + jnp.where(pos >= s, pltpu.roll(g, s, 0), 0.0)
        else:
            g = g + jnp.where(pos < GDN_CHUNK - s, pltpu.roll(g, n - s, 0), 0.0)
        s *= 2
    return g


def _inproj_kernel(*refs, n_x):
    x_refs, refs = refs[:n_x], refs[n_x:]
    (mod_ref, n1g_ref, w_ref, ca_ref, sa_ref, cs_ref, ss_ref, cq_ref, sq_ref,
     alog_ref, dtb_ref, gq_ref, gkv_ref, wuq_ref, wukk_ref, pkr_ref, wukv_ref,
     aq_ref, ak_ref, av_ref, bq_ref, bk_ref, bv_ref, ckv_ref, small_ref,
     cqkv_ref, gcol_ref, grow_ref, cz_ref, gates_ref) = refs
    mod = mod_ref[0]
    h = _rms(_pick_stream(x_refs)) * n1g_ref[...] * (1.0 + mod[1:2]) + mod[0:1]
    hb = h.astype(BF16)

    def proj(lo, hi):
        return jnp.dot(hb, w_ref[:, lo:hi], preferred_element_type=F32)

    ca, sa = ca_ref[...], sa_ref[...]
    a_all = proj(O_AQ, O_BCQ)
    for c in range(4):
        q = a_all[:, c * LANES:(c + 1) * LANES]
        aq_ref[:, c * LANES:(c + 1) * LANES] = (_rope(q, ca, sa, 16) * (A_SCALE * LOG2E)).astype(BF16)
    ak_ref[...] = _rope(a_all[:, O_AK:O_AV], ca, sa, 16)
    av_ref[...] = a_all[:, O_AV:O_BCQ]

    b_all = proj(O_BCQ, O_CQKV)
    cq = (_rms(b_all[:, :B_Q_LORA]) * gq_ref[...]).astype(BF16)
    cqc, sqc = cq_ref[...], sq_ref[...]
    q_all = jnp.dot(cq, wuq_ref[...], preferred_element_type=F32)
    for hh in range(B_HEADS):
        qh = q_all[:, hh * LANES:(hh + 1) * LANES]
        bq_ref[:, hh * LANES:(hh + 1) * LANES] = (_rope(qh, cqc, sqc, 8) * (MLA_SCALE * LOG2E)).astype(BF16)

    ckv = _rms(b_all[:, B_Q_LORA:B_Q_LORA + B_KV_LORA]) * gkv_ref[...]
    ckv_ref[...] = ckv
    ckvb = ckv.astype(BF16)
    small = b_all[:, B_Q_LORA + B_KV_LORA:]
    small_r = _rope(small, cs_ref[...], ss_ref[...], 8)
    small_ref[...] = small_r
    krb = small_r.astype(BF16)
    bk_ref[...] = (jnp.dot(ckvb, wukk_ref[...], preferred_element_type=F32)
                   + jnp.dot(krb, pkr_ref[...], preferred_element_type=F32)).astype(BF16)
    bv_ref[...] = jnp.dot(ckvb, wukv_ref[...], preferred_element_type=F32).astype(BF16)

    for c in range(3):
        cqkv_ref[:, c * 512:(c + 1) * 512] = proj(O_CQKV + c * 512, O_CQKV + (c + 1) * 512)
    cz_ref[...] = proj(O_CZ, O_GATES)
    for c in range(3):
        gates_ref[:, c * D_MODEL:(c + 1) * D_MODEL] = proj(O_GATES + c * D_MODEL, O_GATES + (c + 1) * D_MODEL)

    graw = -jnp.exp(alog_ref[...]) * _softplus(small + dtb_ref[...])
    lane = lax.broadcasted_iota(jnp.int32, small.shape, 1)
    gcol = jnp.where(lane < SM_A + C_HEADS, _chunk_scan(graw, True), _chunk_scan(graw, False))
    gcol = jnp.where((lane >= SM_A) & (lane < SM_B), gcol,
                     jnp.where((lane >= SM_B) & (lane < SM_B + 2 * C_HEADS), _sigmoid(small), 0.0))
    gcol_ref[...] = gcol
    grow_ref[...] = gcol.T[SM_A:SM_A + 16, :]


def _inproj(xs, mod, wts, tabs, layer):
    tile = lambda w: pl.BlockSpec((TM, w), lambda i: (i, 0))
    rope_spec = pl.BlockSpec((TM, LANES), lambda i: (_rope_block_of_tile(i), 0))
    res = lambda shape: _resident_layer(shape, layer)
    out_shapes = [
        jax.ShapeDtypeStruct((T_ALL, 512), BF16),
        jax.ShapeDtypeStruct((T_ALL, 128), F32),
        jax.ShapeDtypeStruct((T_ALL, 128), F32),
        jax.ShapeDtypeStruct((T_ALL, 1024), BF16),
        jax.ShapeDtypeStruct((T_ALL, 1024), BF16),
        jax.ShapeDtypeStruct((T_ALL, 512), BF16),
        jax.ShapeDtypeStruct((T_ALL, 256), F32),
        jax.ShapeDtypeStruct((T_ALL, 128), F32),
        jax.ShapeDtypeStruct((T_ALL, 1536), F32),
        jax.ShapeDtypeStruct((T_ALL, 128), F32),
        jax.ShapeDtypeStruct((16, T_ALL), F32),
        jax.ShapeDtypeStruct((T_ALL, 512), F32),
        jax.ShapeDtypeStruct((T_ALL, 3072), F32),
    ]
    out_specs = [tile(512), tile(128), tile(128), tile(1024), tile(1024), tile(512), tile(256), tile(128),
                 tile(1536), tile(128), pl.BlockSpec((16, TM), lambda i: (0, i)), tile(512), tile(3072)]
    in_specs = _stream_specs(D_MODEL, len(xs)) + [
        _mod_spec(layer),
        res((1, D_MODEL)),
        res((D_MODEL, W_IN_COLS)),
        rope_spec, rope_spec, rope_spec, rope_spec, rope_spec, rope_spec,
        res((1, LANES)), res((1, LANES)),
        res((1, B_Q_LORA)), res((1, B_KV_LORA)),
        res((B_Q_LORA, 1024)), res((B_KV_LORA, 1024)), _resident((LANES, 1024)),
        res((B_KV_LORA, 512)),
    ]
    return pl.pallas_call(
        functools.partial(_inproj_kernel, n_x=len(xs)),
        grid=(N_TILES,),
        in_specs=in_specs,
        out_specs=out_specs,
        out_shape=out_shapes,
        compiler_params=_cparams("arbitrary"),
        name="inproj",
    )(*xs, mod, wts["n1g"], wts["w_in"], tabs["ca"], tabs["sa"], tabs["cs"], tabs["ss"], tabs["cq"], tabs["sq"],
      wts["alog"], wts["dtb"], wts["gq"], wts["gkv"], wts["wuq"], wts["wukk"], wts["pkr"], wts["wukv"])


def _mla_cache_kernel(ckv_ref, kr_ref, wukk_ref, pkr_ref, wukv_ref, k_ref, v_ref):
    ckvb = ckv_ref[...].astype(BF16)
    krb = kr_ref[...].astype(BF16)
    k_ref[...] = (jnp.dot(ckvb, wukk_ref[...], preferred_element_type=F32)
                  + jnp.dot(krb, pkr_ref[...], preferred_element_type=F32)).astype(BF16)
    v_ref[...] = jnp.dot(ckvb, wukv_ref[...], preferred_element_type=F32).astype(BF16)


def _mla_cache(cache_ckv, cache_kr, wts, layer):
    n = DEC_BATCH * PAST_LEN
    res = lambda shape: _resident_layer(shape, layer)
    return pl.pallas_call(
        _mla_cache_kernel,
        grid=(DEC_BATCH,),
        in_specs=[pl.BlockSpec((None, None, PAST_LEN, B_KV_LORA), lambda i: (i, layer, 0, 0)),
                  pl.BlockSpec((None, None, PAST_LEN, LANES), lambda i: (i, layer, 0, 0)),
                  res((B_KV_LORA, 1024)), _resident((LANES, 1024)), res((B_KV_LORA, 512))],
        out_specs=[pl.BlockSpec((PAST_LEN, 1024), lambda i: (i, 0)),
                   pl.BlockSpec((PAST_LEN, 512), lambda i: (i, 0))],
        out_shape=[jax.ShapeDtypeStruct((n, 1024), BF16), jax.ShapeDtypeStruct((n, 512), BF16)],
        compiler_params=_cparams("arbitrary"),
        name="mla_cache",
    )(cache_ckv, cache_kr, wts["wukk"], wts["pkr"], wts["wukv"])


TQ = 256
A_WIN_KEYS = TQ + 2 * A_WINDOW
_NT = (((1,), (1,)), ((), ()))


def _softmax_pv(scores, values, extra_logit=None):
    m = scores[0].max(axis=-1, keepdims=True)
    for s in scores[1:]:
        m = jnp.maximum(m, s.max(axis=-1, keepdims=True))
    if extra_logit is not None:
        m = jnp.maximum(m, extra_logit)
    den = None
    acc = None
    for s, v in zip(scores, values):
        e = jnp.exp2(s - m)
        d = e.sum(axis=-1, keepdims=True)
        den = d if den is None else den + d
        pv = jnp.dot(e.astype(BF16), v, preferred_element_type=F32)
        acc = pv if acc is None else acc + pv
    if extra_logit is not None:
        den = den + jnp.exp2(extra_logit - m)
    return acc / den


def _attn_a_kernel(sink_ref, q_ref, k1_ref, v1_ref, *rest, latent, layer):
    if latent:
        k2_ref, v2_ref, o_ref = rest
        qt = pl.program_id(1)
        t0 = qt * TQ
        start = pl.multiple_of(jnp.clip(t0 - A_WINDOW, 0, DEC_SEQ - A_WIN_KEYS), A_WINDOW)
        k1 = k1_ref[pl.ds(start, A_WIN_KEYS), :].astype(BF16)
        v1 = v1_ref[pl.ds(start, A_WIN_KEYS), :].astype(BF16)
        qpos = t0 + lax.broadcasted_iota(jnp.int32, (TQ, A_WIN_KEYS), 0)
        kpos = start + lax.broadcasted_iota(jnp.int32, (TQ, A_WIN_KEYS), 1)
        valid = jnp.abs(qpos - kpos) <= A_WINDOW
        k2 = k2_ref[...].astype(BF16)
        v2 = v2_ref[...].astype(BF16)
    else:
        (o_ref,) = rest
        k1 = k1_ref[...].astype(BF16)
        v1 = v1_ref[...].astype(BF16)
    lane = lax.broadcasted_iota(jnp.int32, (TQ, LANES), 1)
    low = lane < A_HEAD_DIM
    for g in range(A_GROUP):
        qc = q_ref[:, g * LANES:(g + 1) * LANES]
        outs = []
        for hk in range(A_KV_HEADS):
            qz = jnp.where(low if hk == 0 else ~low, qc, jnp.zeros_like(qc))
            s1 = lax.dot_general(qz, k1, _NT, preferred_element_type=F32)
            sink = sink_ref[layer, hk * A_GROUP + g] * LOG2E
            if latent:
                s1 = jnp.where(valid, s1, NEG_INF)
                s2 = lax.dot_general(qz, k2, _NT, preferred_element_type=F32)
                outs.append(_softmax_pv([s1, s2], [v1, v2], sink))
            else:
                outs.append(_softmax_pv([s1], [v1], sink))
        o_ref[:, g * LANES:(g + 1) * LANES] = jnp.where(low, outs[0], outs[1]).astype(BF16)


def _attn_a_ctx(sink, aq, ak, av, layer):
    nb = T_CTX // TQ
    return pl.pallas_call(
        functools.partial(_attn_a_kernel, latent=False, layer=layer),
        grid=(nb,),
        in_specs=[pl.BlockSpec(memory_space=pltpu.SMEM),
                  pl.BlockSpec((TQ, 512), lambda b: (b, 0)),
                  pl.BlockSpec((SEQ, LANES), lambda b: (b, 0)),
                  pl.BlockSpec((SEQ, LANES), lambda b: (b, 0))],
        out_specs=pl.BlockSpec((TQ, 512), lambda b: (b, 0)),
        out_shape=jax.ShapeDtypeStruct((T_CTX, 512), BF16),
        compiler_params=_cparams("arbitrary"),
        name="attn_a_ctx",
    )(sink, aq, ak, av)


def _attn_a_lat(sink, aq, ak, av, ck, cv, layer):
    nq = DEC_SEQ // TQ
    ctx_tiles = T_CTX // TQ
    seq_blocks = T_CTX // DEC_SEQ
    cache_spec = pl.BlockSpec((None, None, PAST_LEN, LANES), lambda b, i: (b, layer, 0, 0))
    return pl.pallas_call(
        functools.partial(_attn_a_kernel, latent=True, layer=layer),
        grid=(DEC_BATCH, nq),
        in_specs=[pl.BlockSpec(memory_space=pltpu.SMEM),
                  pl.BlockSpec((TQ, 512), lambda b, i: (ctx_tiles + b * nq + i, 0)),
                  pl.BlockSpec((DEC_SEQ, LANES), lambda b, i: (seq_blocks + b, 0)),
                  pl.BlockSpec((DEC_SEQ, LANES), lambda b, i: (seq_blocks + b, 0)),
                  cache_spec, cache_spec],
        out_specs=pl.BlockSpec((TQ, 512), lambda b, i: (b * nq + i, 0)),
        out_shape=jax.ShapeDtypeStruct((T_LAT, 512), BF16),
        compiler_params=_cparams("arbitrary", "arbitrary"),
        name="attn_a_lat",
    )(sink, aq, ak, av, ck, cv)


def _mla_kernel(q_ref, k1_ref, v1_ref, *rest, latent):
    if latent:
        k2_ref, v2_ref, o_ref = rest
    else:
        (o_ref,) = rest
    lane = lax.broadcasted_iota(jnp.int32, (TQ, LANES), 1)
    low = lane < B_V
    for j in range(B_HEADS // 2):
        outs = []
        for sub in range(2):
            hh = 2 * j + sub
            hs = slice(hh * LANES, (hh + 1) * LANES)
            vs = slice(j * LANES, (j + 1) * LANES)
            qh = q_ref[:, hs]
            s1 = lax.dot_general(qh, k1_ref[:, hs], _NT, preferred_element_type=F32)
            if latent:
                s2 = lax.dot_general(qh, k2_ref[:, hs], _NT, preferred_element_type=F32)
                outs.append(_softmax_pv([s1, s2], [v1_ref[:, vs], v2_ref[:, vs]]))
            else:
                outs.append(_softmax_pv([s1], [v1_ref[:, vs]]))
        o_ref[:, j * LANES:(j + 1) * LANES] = jnp.where(low, outs[0], outs[1]).astype(BF16)


def _mla_ctx(bq, bk, bv):
    nb = T_CTX // TQ
    return pl.pallas_call(
        functools.partial(_mla_kernel, latent=False),
        grid=(nb,),
        in_specs=[pl.BlockSpec((TQ, 1024), lambda b: (b, 0)),
                  pl.BlockSpec((SEQ, 1024), lambda b: (b, 0)),
                  pl.BlockSpec((SEQ, 512), lambda b: (b, 0))],
        out_specs=pl.BlockSpec((TQ, 512), lambda b: (b, 0)),
        out_shape=jax.ShapeDtypeStruct((T_CTX, 512), BF16),
        compiler_params=_cparams("arbitrary"),
        name="mla_ctx",
    )(bq, bk, bv)


def _mla_lat(bq, bk, bv, k2, v2):
    nq = DEC_SEQ // TQ
    ctx_tiles = T_CTX // TQ
    seq_blocks = T_CTX // DEC_SEQ
    return pl.pallas_call(
        functools.partial(_mla_kernel, latent=True),
        grid=(DEC_BATCH, nq),
        in_specs=[pl.BlockSpec((TQ, 1024), lambda b, i: (ctx_tiles + b * nq + i, 0)),
                  pl.BlockSpec((DEC_SEQ, 1024), lambda b, i: (seq_blocks + b, 0)),
                  pl.BlockSpec((DEC_SEQ, 512), lambda b, i: (seq_blocks + b, 0)),
                  pl.BlockSpec((PAST_LEN, 1024), lambda b, i: (b, 0)),
                  pl.BlockSpec((PAST_LEN, 512), lambda b, i: (b, 0))],
        out_specs=pl.BlockSpec((TQ, 512), lambda b, i: (b * nq + i, 0)),
        out_shape=jax.ShapeDtypeStruct((T_LAT, 512), BF16),
        compiler_params=_cparams("arbitrary", "arbitrary"),
        name="mla_lat",
    )(bq, bk, bv, k2, v2)


def _gdn_kernel(xq_ref, xk_ref, xv_ref, wq_ref, wk_ref, wv_ref, gcol_ref, grow_ref, cz_ref, ong_ref, *rest,
                n, hps, has_s0, emit_state):
    rest = list(rest)
    s0_ref = rest.pop(0) if has_s0 else None
    oc_ref = rest.pop(0)
    sfin_ref = rest.pop(0) if emit_state else None
    q_s, k_s, v_s, u_s, w_s, qk_s, qd_s, kdt_s, ge_s, st_s, o_s = rest
    nc = n // GDN_CHUNK
    head0 = pl.program_id(1) * hps
    C = GDN_CHUNK
    width = hps * LANES
    hl = lambda hh: slice(hh * LANES, (hh + 1) * LANES)
    chains = [(d, hh) for d in range(2) for hh in range(hps)]

    row_n = lax.broadcasted_iota(jnp.int32, (n, width), 0)

    def conv_silu(x_ref, w_ref):
        x = x_ref[...]
        w = w_ref[...]
        xp = jnp.where(row_n == 0, 0.0, pltpu.roll(x, 1, 0))
        xn = jnp.where(row_n == n - 1, 0.0, pltpu.roll(x, n - 1, 0))
        return _silu(xp * w[0:1] + x * w[1:2] + xn * w[2:3])

    def l2n(x):
        return x * lax.rsqrt(jnp.sum(x * x, axis=-1, keepdims=True) + EPS)

    qc = conv_silu(xq_ref, wq_ref)
    kc = conv_silu(xk_ref, wk_ref)
    for hh in range(hps):
        q_s[:, hl(hh)] = l2n(qc[:, hl(hh)]) * (C_DK ** -0.5)
        k_s[:, hl(hh)] = l2n(kc[:, hl(hh)])
    v_s[...] = conv_silu(xv_ref, wv_ref)
    o_s[...] = jnp.zeros_like(o_s)

    lane_c = lax.broadcasted_iota(jnp.int32, (C, LANES), 1)
    ri = lax.broadcasted_iota(jnp.int32, (C, C), 0)
    ci = lax.broadcasted_iota(jnp.int32, (C, C), 1)
    eye = (ri == ci).astype(F32)
    same_block = {}
    s = 2
    while s <= C:
        same_block[s] = (ri // s) == (ci // s)
        s *= 2

    def prep(it, carry):
        items = []
        for gi in range(group):
            r = it * group + gi
            rows = pl.ds(pl.multiple_of(r * C, C), C)
            gtile = gcol_ref[rows, :]
            for hh in range(hps):
                q = q_s[rows, hl(hh)]
                k = k_s[rows, hl(hh)]
                v = v_s[rows, hl(hh)]
                qk_raw = _dot_nt(q, k)
                for d in range(2):
                    slot = d * hps + hh
                    j = d * C_HEADS + head0 + hh
                    gc = jnp.sum(jnp.where(lane_c == SM_A + j, gtile, 0.0), axis=1, keepdims=True)
                    beta = jnp.sum(jnp.where(lane_c == SM_B + j, gtile, 0.0), axis=1, keepdims=True)
                    grow = grow_ref[j, r]
                    incl = (ri >= ci) if d == 0 else (ri <= ci)
                    strict = (ri > ci) if d == 0 else (ri < ci)
                    decay = jnp.where(incl, jnp.exp(jnp.where(incl, gc - grow, 0.0)), 0.0)
                    kb = k * beta
                    a = jnp.where(strict, _dot_nt(kb, k) * decay, 0.0)
                    g_end = gc[C - 1:C, :] if d == 0 else gc[0:1, :]
                    egc = jnp.exp(gc)
                    qk_s[slot, rows, :] = (qk_raw * decay).astype(BF16)
                    qd_s[slot, rows, :] = (q * egc).astype(BF16)
                    kdt_s[slot, rows, :] = (k * jnp.exp(g_end - gc)).T.astype(BF16)
                    ge_s[slot, r] = jnp.broadcast_to(jnp.exp(g_end), (8, LANES))
                    items.append((slot, rows, a, jnp.concatenate([v * beta, kb * egc], axis=1)))
        ts = [eye - jnp.where(same_block[2], a, 0.0) for (_, _, a, _) in items]
        s = 2
        while s < C:
            level = same_block[2 * s] & ~same_block[s]
            xs = [_dot(t, jnp.where(level, a, 0.0)) for t, (_, _, a, _) in zip(ts, items)]
            ts = [t - _dot(x, t) for t, x in zip(ts, xs)]
            s *= 2
        for t, (slot, rows, _, rhs) in zip(ts, items):
            uw = _dot(t, rhs)
            u_s[slot, rows, :] = uw[:, :C_DV]
            w_s[slot, rows, :] = uw[:, C_DV:].astype(BF16)
        return carry

    group = max(1, min(nc, GDN_PREP_CHAINS // (2 * hps)))
    lax.fori_loop(0, nc // group, prep, 0)

    for d, hh in chains:
        if has_s0:
            st_s[d * hps + hh] = s0_ref[d, hh]
        else:
            st_s[d * hps + hh] = jnp.zeros((C_DK, C_DV), F32)

    def step(i, carry):
        rows = [pl.ds(pl.multiple_of(i * C, C), C), pl.ds(pl.multiple_of((nc - 1 - i) * C, C), C)]
        chunk = [i, nc - 1 - i]
        slots = [d * hps + hh for d, hh in chains]
        st = [st_s[c] for c in slots]
        stb = [s_.astype(BF16) for s_ in st]
        ws = [jnp.dot(w_s[c, rows[d], :], sb, preferred_element_type=F32)
              for c, (d, _), sb in zip(slots, chains, stb)]
        vnb = [(u_s[c, rows[d], :] - x).astype(BF16) for c, (d, _), x in zip(slots, chains, ws)]
        for c, (d, hh), sb, vb, s_ in zip(slots, chains, stb, vnb, st):
            o = (jnp.dot(qd_s[c, rows[d], :], sb, preferred_element_type=F32)
                 + jnp.dot(qk_s[c, rows[d], :], vb, preferred_element_type=F32))
            o_s[rows[d], hl(hh)] += o
            st_s[c] = s_ * ge_s[c, chunk[d]][0:1, :] + jnp.dot(kdt_s[c, rows[d], :], vb,
                                                                preferred_element_type=F32)
        return carry

    lax.fori_loop(0, nc, step, 0)
    if emit_state:
        for d, hh in chains:
            sfin_ref[0, d, hh] = st_s[d * hps + hh]

    for hh in range(hps):
        o = o_s[:, hl(hh)]
        oc_ref[:, hl(hh)] = (_rms(o) * ong_ref[...] * _silu(cz_ref[:, hl(hh)])).astype(BF16)


def _gdn(cqkv, conv_w, gcol, grow3, cz, ong, state_c, layer, *, latent):
    n = DEC_SEQ if latent else SEQ
    nseq = DEC_BATCH if latent else BATCH
    hps = GDN_HEADS_PER_STEP_LAT if latent else GDN_HEADS_PER_STEP_CTX
    seq0 = (T_CTX // n) if latent else 0
    nc = n // GDN_CHUNK
    width = hps * LANES
    sect = C_HEADS // hps
    col = lambda base: pl.BlockSpec((n, width), lambda b, h: (seq0 + b, base + h))
    wcol = lambda base: pl.BlockSpec((None, 3, width), lambda b, h: (layer, 0, base + h))
    in_specs = [col(0), col(sect), col(2 * sect), wcol(0), wcol(sect), wcol(2 * sect),
                pl.BlockSpec((n, LANES), lambda b, h: (seq0 + b, 0)),
                pl.BlockSpec((16, nc, 1, LANES), lambda b, h: (0, seq0 + b, 0, 0)),
                pl.BlockSpec((n, width), lambda b, h: (seq0 + b, h)),
                pl.BlockSpec((None, 1, LANES), lambda b, h: (layer, 0, 0))]
    args = [cqkv, cqkv, cqkv, conv_w, conv_w, conv_w, gcol, grow3, cz, ong]
    out_specs = [pl.BlockSpec((n, width), lambda b, h: (b, h))]
    out_shape = [jax.ShapeDtypeStruct((nseq * n, 512), BF16)]
    if latent:
        in_specs.append(pl.BlockSpec((None, None, 2, hps, C_DK, C_DV), lambda b, h: (b, layer, 0, h, 0, 0)))
        args.append(state_c)
    else:
        out_specs.append(pl.BlockSpec((1, 2, hps, C_DK, C_DV), lambda b, h: (b, 0, h, 0, 0)))
        out_shape.append(jax.ShapeDtypeStruct((nseq, 2, C_HEADS, C_DK, C_DV), F32))
    nch = 2 * hps
    scratch = [pltpu.VMEM((n, width), F32), pltpu.VMEM((n, width), F32), pltpu.VMEM((n, width), F32),
               pltpu.VMEM((nch, n, LANES), F32), pltpu.VMEM((nch, n, LANES), BF16),
               pltpu.VMEM((nch, n, LANES), BF16), pltpu.VMEM((nch, n, LANES), BF16),
               pltpu.VMEM((nch, n, LANES), BF16), pltpu.VMEM((nch, nc, 8, LANES), F32),
               pltpu.VMEM((nch, C_DK, C_DV), F32), pltpu.VMEM((n, width), F32)]
    res = pl.pallas_call(
        functools.partial(_gdn_kernel, n=n, hps=hps, has_s0=latent, emit_state=not latent),
        grid=(nseq, C_HEADS // hps),
        in_specs=in_specs,
        out_specs=out_specs,
        out_shape=out_shape,
        scratch_shapes=scratch,
        compiler_params=_cparams("arbitrary", "arbitrary"),
        name="gdn_lat" if latent else "gdn_ctx",
    )(*args)
    return res


def _merge_ffn_kernel(*refs, n_x, tile0, final):
    x_refs, refs = refs[:n_x], refs[n_x:]
    branch_refs, refs = refs[:6], refs[6:]
    gates_ref, mod_ref, n2g_ref, wbr_ref, wout_ref, wff1_ref, wff2_ref, fg_ref, o_ref = refs
    mod = mod_ref[0]
    z = None
    for j in range(3):
        y = jnp.dot(_pick_stream(branch_refs[2 * j:2 * j + 2], tile0), wbr_ref[j], preferred_element_type=F32)
        t = _sigmoid(gates_ref[:, j * D_MODEL:(j + 1) * D_MODEL]) * y
        z = t if z is None else z + t
    x = (_pick_stream(x_refs, tile0)
         + mod[2:3] * jnp.dot(z.astype(BF16), wout_ref[...], preferred_element_type=F32))
    h = (_rms(x) * n2g_ref[...] * (1.0 + mod[4:5]) + mod[3:4]).astype(BF16)
    f = jnp.maximum(jnp.dot(h, wff1_ref[...], preferred_element_type=F32), 0.0)
    f = (f * f).astype(BF16)
    x = x + mod[5:6] * jnp.dot(f, wff2_ref[...], preferred_element_type=F32)
    if final:
        x = _rms(x) * fg_ref[...]
    o_ref[...] = x


def _merge_ffn(xs, branches, gates, mod, wts, final_g, layer, *, tile0, n_tiles, final):
    res = lambda shape: _resident_layer(shape, layer)
    in_specs = _stream_specs(D_MODEL, len(xs), tile0)
    for _ in range(3):
        in_specs += _stream_specs(BRANCH_W, 2, tile0)
    in_specs += [pl.BlockSpec((TM, 3072), lambda i: (tile0 + i, 0)),
                 _mod_spec(layer, tile0),
                 res((1, D_MODEL)),
                 res((3, BRANCH_W, D_MODEL)), res((D_MODEL, D_MODEL)),
                 res((D_MODEL, D_FF)), res((D_FF, D_MODEL)), _resident((1, D_MODEL))]
    return pl.pallas_call(
        functools.partial(_merge_ffn_kernel, n_x=len(xs), tile0=tile0, final=final),
        grid=(n_tiles,),
        in_specs=in_specs,
        out_specs=pl.BlockSpec((TM, D_MODEL), lambda i: (i, 0)),
        out_shape=jax.ShapeDtypeStruct((n_tiles * TM, D_MODEL), F32),
        compiler_params=_cparams("arbitrary"),
        name="merge_ffn",
    )(*xs, *branches, gates, mod, wts["n2g"], wts["wbr"], wts["wout"], wts["wff1"], wts["wff2"], final_g)


def _rope_tables(r):
    half, quarter = r // 2, r // 4
    t = jnp.arange(DEC_SEQ)
    inv_freq = jnp.power(ROPE_BASE, -jnp.arange(quarter, dtype=F32) / quarter)
    cs, ss = [], []
    for pos in (t // GRID_W, t % GRID_W):
        ang = pos.astype(F32)[:, None] * inv_freq[None, :]
        c, s = jnp.cos(ang), jnp.sin(ang)
        cs += [c, c]
        ss += [-s, s]
    return jnp.concatenate(cs, axis=1), jnp.concatenate(ss, axis=1)


def _tables():
    def with_identity(c, s):
        one = jnp.ones((TM, LANES), F32)
        return jnp.concatenate([one, c], axis=0), jnp.concatenate([0.0 * one, s], axis=0)

    c64, s64 = _rope_tables(A_HEAD_DIM)
    ca, sa = with_identity(jnp.tile(c64, (1, 2)), jnp.tile(s64, (1, 2)))
    c32, s32 = _rope_tables(B_ROPE)
    ones = lambda w: jnp.ones((DEC_SEQ, w), F32)
    zeros = lambda w: jnp.zeros((DEC_SEQ, w), F32)
    cs, ss = with_identity(jnp.concatenate([c32, ones(96)], axis=1), jnp.concatenate([s32, zeros(96)], axis=1))
    cq, sq = with_identity(jnp.concatenate([ones(64), c32, ones(32)], axis=1),
                           jnp.concatenate([zeros(64), s32, zeros(32)], axis=1))
    return {"ca": ca, "sa": sa, "cs": cs, "ss": ss, "cq": cq, "sq": sq}


_A_HEAD_PERM = [hk * A_GROUP + g for g in range(A_GROUP) for hk in range(A_KV_HEADS)]


def _prep_weights(norm1_g, norm2_g, w_in, b_gq, b_w_uq, b_gkv, b_w_ukv, c_a_log, c_dt_bias, c_onorm_g,
                  w_branch, w_out, w_ff1, w_ff2):
    perm = jnp.array(_A_HEAD_PERM)
    offs = [0]
    for s in (512, 128, 128, B_Q_LORA, B_KV_LORA, B_ROPE, 1536, 8, 8, 512, 3072):
        offs.append(offs[-1] + s)
    a_q, a_k, a_v, b_cq, b_ckv, b_kr, c_qkv, c_a, c_b, c_z, gates = (
        w_in[:, :, offs[i]:offs[i + 1]] for i in range(11))
    a_q = a_q.reshape(DEPTH, D_MODEL, A_HEADS, A_HEAD_DIM)[:, :, perm].reshape(DEPTH, D_MODEL, 512)
    small = jnp.concatenate([b_kr, c_a, c_b, jnp.zeros((DEPTH, D_MODEL, LANES - 48), F32)], axis=2)
    w_r = jnp.concatenate([a_q, a_k, a_v, b_cq, b_ckv, small, c_qkv, c_z, gates], axis=2).astype(BF16)

    pad_lane = lambda v: jnp.zeros((DEPTH, 1, LANES), F32).at[:, 0, SM_A:SM_A + 8].set(v.reshape(DEPTH, 8))
    uq = b_w_uq.reshape(DEPTH, B_Q_LORA, B_HEADS, B_NOPE + B_ROPE)
    uq = jnp.pad(uq, ((0, 0), (0, 0), (0, 0), (0, LANES - B_NOPE - B_ROPE)))
    uq = uq.reshape(DEPTH, B_Q_LORA, 1024).astype(BF16)
    ukv = b_w_ukv.reshape(DEPTH, B_KV_LORA, B_HEADS, B_NOPE + B_V)
    ukk = jnp.pad(ukv[..., :B_NOPE], ((0, 0), (0, 0), (0, 0), (0, LANES - B_NOPE)))
    ukk = ukk.reshape(DEPTH, B_KV_LORA, 1024).astype(BF16)
    ukvv = ukv[..., B_NOPE:].reshape(DEPTH, B_KV_LORA, 512).astype(BF16)
    src = jnp.arange(LANES)[:, None]
    dst = jnp.arange(1024)[None, :]
    pkr = ((dst % LANES == B_NOPE + src) & (src < B_ROPE)).astype(BF16)

    wbr_a = w_branch[:, 0].reshape(DEPTH, A_HEADS, A_HEAD_DIM, D_MODEL)[:, perm].reshape(DEPTH, BRANCH_W, D_MODEL)
    wbr = jnp.stack([wbr_a, w_branch[:, 1], w_branch[:, 2]], axis=1).astype(BF16)
    return {"w_in": w_r, "alog": pad_lane(c_a_log), "dtb": pad_lane(c_dt_bias),
            "n1g": norm1_g.reshape(DEPTH, 1, D_MODEL), "n2g": norm2_g.reshape(DEPTH, 1, D_MODEL),
            "gq": b_gq.reshape(DEPTH, 1, B_Q_LORA), "gkv": b_gkv.reshape(DEPTH, 1, B_KV_LORA),
            "ong": c_onorm_g.reshape(DEPTH, 1, C_DV),
            "wuq": uq, "wukk": ukk, "pkr": pkr, "wukv": ukvv,
            "wbr": wbr, "wout": w_out.astype(BF16), "wff1": w_ff1.astype(BF16), "wff2": w_ff2.astype(BF16)}


def kernel(x_prompt, x_sample, cache_a_k, cache_a_v, cache_b_ckv, cache_b_krope, state_c, c, c_ctx, norm1_g,
           norm2_g, w_ada, b_ada, w_in, a_sink, b_gq, b_w_uq, b_gkv, b_w_ukv, c_conv_w, c_a_log, c_dt_bias,
           c_onorm_g, w_branch, w_out, w_ff1, w_ff2, final_g):
    xs = (x_prompt.reshape(T_CTX, D_MODEL), x_sample.reshape(T_LAT, D_MODEL))
    cond8 = jnp.concatenate([c_ctx[None, :], c, jnp.zeros((N_GROUPS - 1 - DEC_BATCH, D_MODEL), F32)], axis=0)
    mod = _adaln(cond8, w_ada, b_ada).reshape(DEPTH, N_GROUPS, 6, D_MODEL)
    tabs = _tables()
    fg = final_g.reshape(1, D_MODEL)
    wts = _prep_weights(norm1_g, norm2_g, w_in, b_gq, b_w_uq, b_gkv, b_w_ukv, c_a_log, c_dt_bias, c_onorm_g,
                        w_branch, w_out, w_ff1, w_ff2)
    ck = cache_a_k.reshape(DEC_BATCH, DEPTH, PAST_LEN, LANES)
    cv = cache_a_v.reshape(DEC_BATCH, DEPTH, PAST_LEN, LANES)
    cache_kr = jnp.pad(cache_b_krope, ((0, 0), (0, 0), (0, 0), (0, LANES - B_ROPE)))

    ak_l, av_l, ckv_l, kr_l, sc_l = [], [], [], [], []
    for l in range(DEPTH):
        (aq, ak, av, bq, bk, bv, ckv, small, cqkv, gcol, grow, cz, gates) = _inproj(xs, mod, wts, tabs, l)

        oa_ctx = _attn_a_ctx(a_sink, aq, ak, av, l)
        oa_lat = _attn_a_lat(a_sink, aq, ak, av, ck, cv, l)

        k2, v2 = _mla_cache(cache_b_ckv, cache_kr, wts, l)
        ob_ctx = _mla_ctx(bq, bk, bv)
        ob_lat = _mla_lat(bq, bk, bv, k2, v2)

        grow3 = grow.reshape(16, T_ALL // GDN_CHUNK, 1, GDN_CHUNK)
        oc_ctx, s_fin = _gdn(cqkv, c_conv_w, gcol, grow3, cz, wts["ong"], state_c, l, latent=False)
        (oc_lat,) = _gdn(cqkv, c_conv_w, gcol, grow3, cz, wts["ong"], state_c, l, latent=True)

        branches = (oa_ctx, oa_lat, ob_ctx, ob_lat, oc_ctx, oc_lat)
        merge = functools.partial(_merge_ffn, xs, branches, gates, mod, wts, fg, l)
        if l < DEPTH - 1:
            xs = (merge(tile0=0, n_tiles=N_TILES, final=False),)
        else:
            xs = (merge(tile0=0, n_tiles=N_CTX_TILES, final=True),
                  merge(tile0=N_CTX_TILES, n_tiles=N_TILES - N_CTX_TILES, final=True))

        ak_l.append(ak[:T_CTX].reshape(BATCH, SEQ, A_KV_HEADS, A_HEAD_DIM))
        av_l.append(av[:T_CTX].reshape(BATCH, SEQ, A_KV_HEADS, A_HEAD_DIM))
        ckv_l.append(ckv[:T_CTX].reshape(BATCH, SEQ, B_KV_LORA))
        kr_l.append(small[:T_CTX, :B_ROPE].reshape(BATCH, SEQ, B_ROPE))
        sc_l.append(s_fin)

    y_prompt = xs[0].reshape(BATCH, SEQ, D_MODEL)
    y_sample = xs[1].reshape(DEC_BATCH, DEC_SEQ, D_MODEL)
    return (y_prompt, y_sample, jnp.stack(ak_l, axis=1), jnp.stack(av_l, axis=1), jnp.stack(ckv_l, axis=1),
            jnp.stack(kr_l, axis=1), jnp.stack(sc_l, axis=1))
```

```python
import functools
import math

import jax
import jax.numpy as jnp
from jax import lax
from jax.experimental import pallas as pl
from jax.experimental.pallas import tpu as pltpu

F32 = jnp.float32
BF16 = jnp.bfloat16

D_MODEL = 1024
BATCH = 16
SEQ = 256
DEPTH = 2
DEC_BATCH = 4
DEC_SEQ = 2048
PAST_LEN = 256
GRID_W = 64
ROPE_BASE = 10000.0
EPS = 1e-6
NEG_INF = -1e30

A_HEADS = 8
A_KV_HEADS = 2
A_GROUP = 4
A_HEAD_DIM = 64
A_WINDOW = 128
A_SCALE = A_HEAD_DIM ** -0.5
B_HEADS = 8
B_Q_LORA = 384
B_KV_LORA = 256
B_NOPE = 64
B_ROPE = 32
B_V = 64
MLA_SCALE = (B_NOPE + B_ROPE) ** -0.5
LOG2E = math.log2(math.e)
C_HEADS = 4
C_DK = 128
C_DV = 128
D_FF = 4 * D_MODEL
BRANCH_W = 512

LANES = 128

T_CTX = BATCH * SEQ
T_LAT = DEC_BATCH * DEC_SEQ
T_ALL = T_CTX + T_LAT
TM = 256
N_CTX_TILES = T_CTX // TM
LAT_TILES_PER_SEQ = DEC_SEQ // TM
N_TILES = T_ALL // TM
N_GROUPS = 8
GDN_CHUNK = 128
GDN_PREP_CHAINS = 16
GDN_HEADS_PER_STEP_CTX = 4
GDN_HEADS_PER_STEP_LAT = 2

O_AQ, O_AK, O_AV, O_BCQ, O_BCKV, O_SMALL, O_CQKV, O_CZ, O_GATES, W_IN_COLS = (
    0, 512, 640, 768, 1152, 1408, 1536, 3072, 3584, 6656)
SM_KR, SM_A, SM_B = 0, 32, 40

VMEM_LIMIT = 56 * 1024 * 1024


def _cparams(*sem):
    return pltpu.CompilerParams(dimension_semantics=sem, vmem_limit_bytes=VMEM_LIMIT)


def _resident(shape):
    nd = len(shape)
    return pl.BlockSpec(shape, lambda *_: (0,) * nd, pipeline_mode=pl.Buffered(1))


def _resident_layer(shape, layer):
    nd = len(shape)
    return pl.BlockSpec((None,) + tuple(shape), lambda *_: (layer,) + (0,) * nd, pipeline_mode=pl.Buffered(1))


def _group_of_tile(i):
    return jnp.where(i < N_CTX_TILES, 0, 1 + (i - N_CTX_TILES) // LAT_TILES_PER_SEQ)


def _rope_block_of_tile(i):
    return jnp.where(i < N_CTX_TILES, 0, 1 + (i - N_CTX_TILES) % LAT_TILES_PER_SEQ)


def _mod_spec(layer, tile0=0):
    return pl.BlockSpec((None, 1, 6, D_MODEL), lambda i: (layer, _group_of_tile(tile0 + i), 0, 0))


def _stream_specs(width, n_arrays, tile0=0):
    if n_arrays == 1:
        return [pl.BlockSpec((TM, width), lambda i: (tile0 + i, 0))]
    return [pl.BlockSpec((TM, width), lambda i: (jnp.minimum(tile0 + i, N_CTX_TILES - 1), 0)),
            pl.BlockSpec((TM, width), lambda i: (jnp.maximum(tile0 + i - N_CTX_TILES, 0), 0))]


def _pick_stream(refs, tile0=0):
    if len(refs) == 1:
        return refs[0][...]
    return jnp.where(tile0 + pl.program_id(0) < N_CTX_TILES, refs[0][...], refs[1][...])


def _rms(x):
    return x * lax.rsqrt(jnp.mean(x * x, axis=-1, keepdims=True) + EPS)


def _sigmoid(x):
    return 1.0 / (1.0 + jnp.exp(-x))


def _silu(x):
    return x * _sigmoid(x)


def _softplus(x):
    return jnp.maximum(x, 0.0) + jnp.log(1.0 + jnp.exp(-jnp.abs(x)))


def _dot(a, b):
    return jnp.dot(a.astype(BF16), b.astype(BF16), preferred_element_type=F32)


def _dot_nt(a, b):
    return lax.dot_general(a.astype(BF16), b.astype(BF16), (((1,), (1,)), ((), ())),
                           preferred_element_type=F32)


def _swap_groups(x, q):
    w = x.shape[1]
    lane = lax.broadcasted_iota(jnp.int32, x.shape, 1)
    even = ((lane // q) % 2) == 0
    return jnp.where(even, pltpu.roll(x, w - q, 1), pltpu.roll(x, q, 1))


def _rope(x, cos, sin, q):
    return x * cos + _swap_groups(x, q) * sin


ADA_TN = 1536


def _adaln_kernel(c_ref, w_ref, b_ref, o_ref):
    c = c_ref[...]
    o_ref[0] = jnp.dot(_silu(c), w_ref[0], preferred_element_type=F32) + b_ref[0]


def _adaln(cond8, w_ada, b_ada):
    n = 6 * D_MODEL
    return pl.pallas_call(
        _adaln_kernel,
        grid=(DEPTH, n // ADA_TN),
        in_specs=[
            pl.BlockSpec((N_GROUPS, D_MODEL), lambda l, j: (0, 0)),
            pl.BlockSpec((1, D_MODEL, ADA_TN), lambda l, j: (l, 0, j)),
            pl.BlockSpec((1, 1, ADA_TN), lambda l, j: (l, 0, j)),
        ],
        out_specs=pl.BlockSpec((1, N_GROUPS, ADA_TN), lambda l, j: (l, 0, j)),
        out_shape=jax.ShapeDtypeStruct((DEPTH, N_GROUPS, n), F32),
        compiler_params=_cparams("arbitrary", "arbitrary"),
        name="adaln",
    )(cond8, w_ada, b_ada.reshape(DEPTH, 1, n))


def _chunk_scan(g, forward):
    n = g.shape[0]
    pos = lax.broadcasted_iota(jnp.int32, g.shape, 0) % GDN_CHUNK
    s = 1
    while s < GDN_CHUNK:
        if forward:
            g = g + jnp.where(pos >= s, pltpu.roll(g, s, 0), 0.0)
        else:
            g = g + jnp.where(pos < GDN_CHUNK - s, pltpu.roll(g, n - s, 0), 0.0)
        s *= 2
    return g


def _inproj_kernel(*refs, n_x):
    x_refs, refs = refs[:n_x], refs[n_x:]
    (mod_ref, n1g_ref, w_ref, ca_ref, sa_ref, cs_ref, ss_ref, cq_ref, sq_ref,
     alog_ref, dtb_ref, gq_ref, gkv_ref, wuq_ref, wukk_ref, pkr_ref, wukv_ref,
     aq_ref, ak_ref, av_ref, bq_ref, bk_ref, bv_ref, ckv_ref, small_ref,
     cqkv_ref, gcol_ref, grow_ref, cz_ref, gates_ref) = refs
    mod = mod_ref[0]
    h = _rms(_pick_stream(x_refs)) * n1g_ref[...] * (1.0 + mod[1:2]) + mod[0:1]
    hb = h.astype(BF16)

    def proj(lo, hi):
        return jnp.dot(hb, w_ref[:, lo:hi], preferred_element_type=F32)

    ca, sa = ca_ref[...], sa_ref[...]
    a_all = proj(O_AQ, O_BCQ)
    for c in range(4):
        q = a_all[:, c * LANES:(c + 1) * LANES]
        aq_ref[:, c * LANES:(c + 1) * LANES] = (_rope(q, ca, sa, 16) * (A_SCALE * LOG2E)).astype(BF16)
    ak_ref[...] = _rope(a_all[:, O_AK:O_AV], ca, sa, 16)
    av_ref[...] = a_all[:, O_AV:O_BCQ]

    b_all = proj(O_BCQ, O_CQKV)
    cq = (_rms(b_all[:, :B_Q_LORA]) * gq_ref[...]).astype(BF16)
    cqc, sqc = cq_ref[...], sq_ref[...]
    q_all = jnp.dot(cq, wuq_ref[...], preferred_element_type=F32)
    for hh in range(B_HEADS):
        qh = q_all[:, hh * LANES:(hh + 1) * LANES]
        bq_ref[:, hh * LANES:(hh + 1) * LANES] = (_rope(qh, cqc, sqc, 8) * (MLA_SCALE * LOG2E)).astype(BF16)

    ckv = _rms(b_all[:, B_Q_LORA:B_Q_LORA + B_KV_LORA]) * gkv_ref[...]
    ckv_ref[...] = ckv
    ckvb = ckv.astype(BF16)
    small = b_all[:, B_Q_LORA + B_KV_LORA:]
    small_r = _rope(small, cs_ref[...], ss_ref[...], 8)
    small_ref[...] = small_r
    krb = small_r.astype(BF16)
    bk_ref[...] = (jnp.dot(ckvb, wukk_ref[...], preferred_element_type=F32)
                   + jnp.dot(krb, pkr_ref[...], preferred_element_type=F32)).astype(BF16)
    bv_ref[...] = jnp.dot(ckvb, wukv_ref[...], preferred_element_type=F32).astype(BF16)

    for c in range(3):
        cqkv_ref[:, c * 512:(c + 1) * 512] = proj(O_CQKV + c * 512, O_CQKV + (c + 1) * 512)
    cz_ref[...] = proj(O_CZ, O_GATES)
    for c in range(3):
        gates_ref[:, c * D_MODEL:(c + 1) * D_MODEL] = proj(O_GATES + c * D_MODEL, O_GATES + (c + 1) * D_MODEL)

    graw = -jnp.exp(alog_ref[...]) * _softplus(small + dtb_ref[...])
    lane = lax.broadcasted_iota(jnp.int32, small.shape, 1)
    gcol = jnp.where(lane < SM_A + C_HEADS, _chunk_scan(graw, True), _chunk_scan(graw, False))
    gcol = jnp.where((lane >= SM_A) & (lane < SM_B), gcol,
                     jnp.where((lane >= SM_B) & (lane < SM_B + 2 * C_HEADS), _sigmoid(small), 0.0))
    gcol_ref[...] = gcol
    grow_ref[...] = gcol.T[SM_A:SM_A + 16, :]


def _inproj(xs, mod, wts, tabs, layer):
    tile = lambda w: pl.BlockSpec((TM, w), lambda i: (i, 0))
    rope_spec = pl.BlockSpec((TM, LANES), lambda i: (_rope_block_of_tile(i), 0))
    res = lambda shape: _resident_layer(shape, layer)
    out_shapes = [
        jax.ShapeDtypeStruct((T_ALL, 512), BF16),
        jax.ShapeDtypeStruct((T_ALL, 128), F32),
        jax.ShapeDtypeStruct((T_ALL, 128), F32),
        jax.ShapeDtypeStruct((T_ALL, 1024), BF16),
        jax.ShapeDtypeStruct((T_ALL, 1024), BF16),
        jax.ShapeDtypeStruct((T_ALL, 512), BF16),
        jax.ShapeDtypeStruct((T_ALL, 256), F32),
        jax.ShapeDtypeStruct((T_ALL, 128), F32),
        jax.ShapeDtypeStruct((T_ALL, 1536), F32),
        jax.ShapeDtypeStruct((T_ALL, 128), F32),
        jax.ShapeDtypeStruct((16, T_ALL), F32),
        jax.ShapeDtypeStruct((T_ALL, 512), F32),
        jax.ShapeDtypeStruct((T_ALL, 3072), F32),
    ]
    out_specs = [tile(512), tile(128), tile(128), tile(1024), tile(1024), tile(512), tile(256), tile(128),
                 tile(1536), tile(128), pl.BlockSpec((16, TM), lambda i: (0, i)), tile(512), tile(3072)]
    in_specs = _stream_specs(D_MODEL, len(xs)) + [
        _mod_spec(layer),
        res((1, D_MODEL)),
        res((D_MODEL, W_IN_COLS)),
        rope_spec, rope_spec, rope_spec, rope_spec, rope_spec, rope_spec,
        res((1, LANES)), res((1, LANES)),
        res((1, B_Q_LORA)), res((1, B_KV_LORA)),
        res((B_Q_LORA, 1024)), res((B_KV_LORA, 1024)), _resident((LANES, 1024)),
        res((B_KV_LORA, 512)),
    ]
    return pl.pallas_call(
        functools.partial(_inproj_kernel, n_x=len(xs)),
        grid=(N_TILES,),
        in_specs=in_specs,
        out_specs=out_specs,
        out_shape=out_shapes,
        compiler_params=_cparams("arbitrary"),
        name="inproj",
    )(*xs, mod, wts["n1g"], wts["w_in"], tabs["ca"], tabs["sa"], tabs["cs"], tabs["ss"], tabs["cq"], tabs["sq"],
      wts["alog"], wts["dtb"], wts["gq"], wts["gkv"], wts["wuq"], wts["wukk"], wts["pkr"], wts["wukv"])


def _mla_cache_kernel(ckv_ref, kr_ref, wukk_ref, pkr_ref, wukv_ref, k_ref, v_ref):
    ckvb = ckv_ref[...].astype(BF16)
    krb = kr_ref[...].astype(BF16)
    k_ref[...] = (jnp.dot(ckvb, wukk_ref[...], preferred_element_type=F32)
                  + jnp.dot(krb, pkr_ref[...], preferred_element_type=F32)).astype(BF16)
    v_ref[...] = jnp.dot(ckvb, wukv_ref[...], preferred_element_type=F32).astype(BF16)


def _mla_cache(cache_ckv, cache_kr, wts, layer):
    n = DEC_BATCH * PAST_LEN
    res = lambda shape: _resident_layer(shape, layer)
    return pl.pallas_call(
        _mla_cache_kernel,
        grid=(DEC_BATCH,),
        in_specs=[pl.BlockSpec((None, None, PAST_LEN, B_KV_LORA), lambda i: (i, layer, 0, 0)),
                  pl.BlockSpec((None, None, PAST_LEN, LANES), lambda i: (i, layer, 0, 0)),
                  res((B_KV_LORA, 1024)), _resident((LANES, 1024)), res((B_KV_LORA, 512))],
        out_specs=[pl.BlockSpec((PAST_LEN, 1024), lambda i: (i, 0)),
                   pl.BlockSpec((PAST_LEN, 512), lambda i: (i, 0))],
        out_shape=[jax.ShapeDtypeStruct((n, 1024), BF16), jax.ShapeDtypeStruct((n, 512), BF16)],
        compiler_params=_cparams("arbitrary"),
        name="mla_cache",
    )(cache_ckv, cache_kr, wts["wukk"], wts["pkr"], wts["wukv"])


TQ = 256
A_WIN_KEYS = TQ + 2 * A_WINDOW
_NT = (((1,), (1,)), ((), ()))


def _softmax_pv(scores, values, extra_logit=None):
    m = scores[0].max(axis=-1, keepdims=True)
    for s in scores[1:]:
        m = jnp.maximum(m, s.max(axis=-1, keepdims=True))
    if extra_logit is not None:
        m = jnp.maximum(m, extra_logit)
    den = None
    acc = None
    for s, v in zip(scores, values):
        e = jnp.exp2(s - m)
        d = e.sum(axis=-1, keepdims=True)
        den = d if den is None else den + d
        pv = jnp.dot(e.astype(BF16), v, preferred_element_type=F32)
        acc = pv if acc is None else acc + pv
    if extra_logit is not None:
        den = den + jnp.exp2(extra_logit - m)
    return acc / den


def _attn_a_kernel(sink_ref, q_ref, k1_ref, v1_ref, *rest, latent, layer):
    if latent:
        k2_ref, v2_ref, o_ref = rest
        qt = pl.program_id(1)
        t0 = qt * TQ
        start = pl.multiple_of(jnp.clip(t0 - A_WINDOW, 0, DEC_SEQ - A_WIN_KEYS), A_WINDOW)
        k1 = k1_ref[pl.ds(start, A_WIN_KEYS), :].astype(BF16)
        v1 = v1_ref[pl.ds(start, A_WIN_KEYS), :].astype(BF16)
        qpos = t0 + lax.broadcasted_iota(jnp.int32, (TQ, A_WIN_KEYS), 0)
        kpos = start + lax.broadcasted_iota(jnp.int32, (TQ, A_WIN_KEYS), 1)
        valid = jnp.abs(qpos - kpos) <= A_WINDOW
        k2 = k2_ref[...].astype(BF16)
        v2 = v2_ref[...].astype(BF16)
    else:
        (o_ref,) = rest
        k1 = k1_ref[...].astype(BF16)
        v1 = v1_ref[...].astype(BF16)
    lane = lax.broadcasted_iota(jnp.int32, (TQ, LANES), 1)
    low = lane < A_HEAD_DIM
    for g in range(A_GROUP):
        qc = q_ref[:, g * LANES:(g + 1) * LANES]
        outs = []
        for hk in range(A_KV_HEADS):
            qz = jnp.where(low if hk == 0 else ~low, qc, jnp.zeros_like(qc))
            s1 = lax.dot_general(qz, k1, _NT, preferred_element_type=F32)
            sink = sink_ref[layer, hk * A_GROUP + g] * LOG2E
            if latent:
                s1 = jnp.where(valid, s1, NEG_INF)
                s2 = lax.dot_general(qz, k2, _NT, preferred_element_type=F32)
                outs.append(_softmax_pv([s1, s2], [v1, v2], sink))
            else:
                outs.append(_softmax_pv([s1], [v1], sink))
        o_ref[:, g * LANES:(g + 1) * LANES] = jnp.where(low, outs[0], outs[1]).astype(BF16)


def _attn_a_ctx(sink, aq, ak, av, layer):
    nb = T_CTX // TQ
    return pl.pallas_call(
        functools.partial(_attn_a_kernel, latent=False, layer=layer),
        grid=(nb,),
        in_specs=[pl.BlockSpec(memory_space=pltpu.SMEM),
                  pl.BlockSpec((TQ, 512), lambda b: (b, 0)),
                  pl.BlockSpec((SEQ, LANES), lambda b: (b, 0)),
                  pl.BlockSpec((SEQ, LANES), lambda b: (b, 0))],
        out_specs=pl.BlockSpec((TQ, 512), lambda b: (b, 0)),
        out_shape=jax.ShapeDtypeStruct((T_CTX, 512), BF16),
        compiler_params=_cparams("arbitrary"),
        name="attn_a_ctx",
    )(sink, aq, ak, av)


def _attn_a_lat(sink, aq, ak, av, ck, cv, layer):
    nq = DEC_SEQ // TQ
    ctx_tiles = T_CTX // TQ
    seq_blocks = T_CTX // DEC_SEQ
    cache_spec = pl.BlockSpec((None, None, PAST_LEN, LANES), lambda b, i: (b, layer, 0, 0))
    return pl.pallas_call(
        functools.partial(_attn_a_kernel, latent=True, layer=layer),
        grid=(DEC_BATCH, nq),
        in_specs=[pl.BlockSpec(memory_space=pltpu.SMEM),
                  pl.BlockSpec((TQ, 512), lambda b, i: (ctx_tiles + b * nq + i, 0)),
                  pl.BlockSpec((DEC_SEQ, LANES), lambda b, i: (seq_blocks + b, 0)),
                  pl.BlockSpec((DEC_SEQ, LANES), lambda b, i: (seq_blocks + b, 0)),
                  cache_spec, cache_spec],
        out_specs=pl.BlockSpec((TQ, 512), lambda b, i: (b * nq + i, 0)),
        out_shape=jax.ShapeDtypeStruct((T_LAT, 512), BF16),
        compiler_params=_cparams("arbitrary", "arbitrary"),
        name="attn_a_lat",
    )(sink, aq, ak, av, ck, cv)


def _mla_kernel(q_ref, k1_ref, v1_ref, *rest, latent):
    if latent:
        k2_ref, v2_ref, o_ref = rest
    else:
        (o_ref,) = rest
    lane = lax.broadcasted_iota(jnp.int32, (TQ, LANES), 1)
    low = lane < B_V
    for j in range(B_HEADS // 2):
        outs = []
        for sub in range(2):
            hh = 2 * j + sub
            hs = slice(hh * LANES, (hh + 1) * LANES)
            vs = slice(j * LANES, (j + 1) * LANES)
            qh = q_ref[:, hs]
            s1 = lax.dot_general(qh, k1_ref[:, hs], _NT, preferred_element_type=F32)
            if latent:
                s2 = lax.dot_general(qh, k2_ref[:, hs], _NT, preferred_element_type=F32)
                outs.append(_softmax_pv([s1, s2], [v1_ref[:, vs], v2_ref[:, vs]]))
            else:
                outs.append(_softmax_pv([s1], [v1_ref[:, vs]]))
        o_ref[:, j * LANES:(j + 1) * LANES] = jnp.where(low, outs[0], outs[1]).astype(BF16)


def _mla_ctx(bq, bk, bv):
    nb = T_CTX // TQ
    return pl.pallas_call(
        functools.partial(_mla_kernel, latent=False),
        grid=(nb,),
        in_specs=[pl.BlockSpec((TQ, 1024), lambda b: (b, 0)),
                  pl.BlockSpec((SEQ, 1024), lambda b: (b, 0)),
                  pl.BlockSpec((SEQ, 512), lambda b: (b, 0))],
        out_specs=pl.BlockSpec((TQ, 512), lambda b: (b, 0)),
        out_shape=jax.ShapeDtypeStruct((T_CTX, 512), BF16),
        compiler_params=_cparams("arbitrary"),
        name="mla_ctx",
    )(bq, bk, bv)


def _mla_lat(bq, bk, bv, k2, v2):
    nq = DEC_SEQ // TQ
    ctx_tiles = T_CTX // TQ
    seq_blocks = T_CTX // DEC_SEQ
    return pl.pallas_call(
        functools.partial(_mla_kernel, latent=True),
        grid=(DEC_BATCH, nq),
        in_specs=[pl.BlockSpec((TQ, 1024), lambda b, i: (ctx_tiles + b * nq + i, 0)),
                  pl.BlockSpec((DEC_SEQ, 1024), lambda b, i: (seq_blocks + b, 0)),
                  pl.BlockSpec((DEC_SEQ, 512), lambda b, i: (seq_blocks + b, 0)),
                  pl.BlockSpec((PAST_LEN, 1024), lambda b, i: (b, 0)),
                  pl.BlockSpec((PAST_LEN, 512), lambda b, i: (b, 0))],
        out_specs=pl.BlockSpec((TQ, 512), lambda b, i: (b * nq + i, 0)),
        out_shape=jax.ShapeDtypeStruct((T_LAT, 512), BF16),
        compiler_params=_cparams("arbitrary", "arbitrary"),
        name="mla_lat",
    )(bq, bk, bv, k2, v2)


def _gdn_kernel(xq_ref, xk_ref, xv_ref, wq_ref, wk_ref, wv_ref, gcol_ref, grow_ref, cz_ref, ong_ref, *rest,
                n, hps, has_s0, emit_state):
    rest = list(rest)
    s0_ref = rest.pop(0) if has_s0 else None
    oc_ref = rest.pop(0)
    sfin_ref = rest.pop(0) if emit_state else None
    q_s, k_s, v_s, u_s, w_s, qk_s, qd_s, kdt_s, ge_s, st_s, o_s = rest
    nc = n // GDN_CHUNK
    head0 = pl.program_id(1) * hps
    C = GDN_CHUNK
    width = hps * LANES
    hl = lambda hh: slice(hh * LANES, (hh + 1) * LANES)
    chains = [(d, hh) for d in range(2) for hh in range(hps)]

    row_n = lax.broadcasted_iota(jnp.int32, (n, width), 0)

    def conv_silu(x_ref, w_ref):
        x = x_ref[...]
        w = w_ref[...]
        xp = jnp.where(row_n == 0, 0.0, pltpu.roll(x, 1, 0))
        xn = jnp.where(row_n == n - 1, 0.0, pltpu.roll(x, n - 1, 0))
        return _silu(xp * w[0:1] + x * w[1:2] + xn * w[2:3])

    def l2n(x):
        return x * lax.rsqrt(jnp.sum(x * x, axis=-1, keepdims=True) + EPS)

    qc = conv_silu(xq_ref, wq_ref)
    kc = conv_silu(xk_ref, wk_ref)
    for hh in range(hps):
        q_s[:, hl(hh)] = l2n(qc[:, hl(hh)]) * (C_DK ** -0.5)
        k_s[:, hl(hh)] = l2n(kc[:, hl(hh)])
    v_s[...] = conv_silu(xv_ref, wv_ref)
    o_s[...] = jnp.zeros_like(o_s)

    lane_c = lax.broadcasted_iota(jnp.int32, (C, LANES), 1)
    ri = lax.broadcasted_iota(jnp.int32, (C, C), 0)
    ci = lax.broadcasted_iota(jnp.int32, (C, C), 1)
    eye = (ri == ci).astype(F32)
    same_block = {}
    s = 2
    while s <= C:
        same_block[s] = (ri // s) == (ci // s)
        s *= 2

    def prep(it, carry):
        items = []
        for gi in range(group):
            r = it * group + gi
            rows = pl.ds(pl.multiple_of(r * C, C), C)
            gtile = gcol_ref[rows, :]
            for hh in range(hps):
                q = q_s[rows, hl(hh)]
                k = k_s[rows, hl(hh)]
                v = v_s[rows, hl(hh)]
                qk_raw = _dot_nt(q, k)
                for d in range(2):
                    slot = d * hps + hh
                    j = d * C_HEADS + head0 + hh
                    gc = jnp.sum(jnp.where(lane_c == SM_A + j, gtile, 0.0), axis=1, keepdims=True)
                    beta = jnp.sum(jnp.where(lane_c == SM_B + j, gtile, 0.0), axis=1, keepdims=True)
                    grow = grow_ref[j, r]
                    incl = (ri >= ci) if d == 0 else (ri <= ci)
                    strict = (ri > ci) if d == 0 else (ri < ci)
                    decay = jnp.where(incl, jnp.exp(jnp.where(incl, gc - grow, 0.0)), 0.0)
                    kb = k * beta
                    a = jnp.where(strict, _dot_nt(kb, k) * decay, 0.0)
                    g_end = gc[C - 1:C, :] if d == 0 else gc[0:1, :]
                    egc = jnp.exp(gc)
                    qk_s[slot, rows, :] = (qk_raw * decay).astype(BF16)
                    qd_s[slot, rows, :] = (q * egc).astype(BF16)
                    kdt_s[slot, rows, :] = (k * jnp.exp(g_end - gc)).T.astype(BF16)
                    ge_s[slot, r] = jnp.broadcast_to(jnp.exp(g_end), (8, LANES))
                    items.append((slot, rows, a, jnp.concatenate([v * beta, kb * egc], axis=1), d))
        ts = [eye - jnp.where(same_block[2], a, 0.0) for (_, _, a, _, _) in items]
        s = 2
        while s < C:
            level = same_block[2 * s] & ~same_block[s]
            es = [jnp.where(level, a, 0.0) for (_, _, a, _, _) in items]
            if s < 8:
                xs = [_dot(t, e) for t, e in zip(ts, es)]
                ts = [t - _dot(x, t) for t, x in zip(ts, xs)]
            else:
                def active(t, d):
                    return jnp.concatenate([t[b * s:(b + 1) * s] for b in range(1 - d, C // s, 2)], axis=0)

                def merged(t, upd, d):
                    return jnp.concatenate(
                        [upd[(b // 2) * s:(b // 2 + 1) * s] if b % 2 == 1 - d else t[b * s:(b + 1) * s]
                         for b in range(C // s)], axis=0)

                halves = [active(t, it_[4]) for t, it_ in zip(ts, items)]
                xs = [_dot(h, e) for h, e in zip(halves, es)]
                upds = [h - _dot(x, t) for h, x, t in zip(halves, xs, ts)]
                ts = [merged(t, u, it_[4]) for t, u, it_ in zip(ts, upds, items)]
            s *= 2
        for t, (slot, rows, _, rhs, _) in zip(ts, items):
            uw = _dot(t, rhs)
            u_s[slot, rows, :] = uw[:, :C_DV]
            w_s[slot, rows, :] = uw[:, C_DV:].astype(BF16)
        return carry

    group = max(1, min(nc, GDN_PREP_CHAINS // (2 * hps)))
    lax.fori_loop(0, nc // group, prep, 0)

    for d, hh in chains:
        if has_s0:
            st_s[d * hps + hh] = s0_ref[d, hh]
        else:
            st_s[d * hps + hh] = jnp.zeros((C_DK, C_DV), F32)

    def step(i, carry):
        rows = [pl.ds(pl.multiple_of(i * C, C), C), pl.ds(pl.multiple_of((nc - 1 - i) * C, C), C)]
        chunk = [i, nc - 1 - i]
        slots = [d * hps + hh for d, hh in chains]
        st = [st_s[c] for c in slots]
        stb = [s_.astype(BF16) for s_ in st]
        ws = [jnp.dot(w_s[c, rows[d], :], sb, preferred_element_type=F32)
              for c, (d, _), sb in zip(slots, chains, stb)]
        vnb = [(u_s[c, rows[d], :] - x).astype(BF16) for c, (d, _), x in zip(slots, chains, ws)]
        for c, (d, hh), sb, vb, s_ in zip(slots, chains, stb, vnb, st):
            o = (jnp.dot(qd_s[c, rows[d], :], sb, preferred_element_type=F32)
                 + jnp.dot(qk_s[c, rows[d], :], vb, preferred_element_type=F32))
            o_s[rows[d], hl(hh)] += o
            st_s[c] = s_ * ge_s[c, chunk[d]][0:1, :] + jnp.dot(kdt_s[c, rows[d], :], vb,
                                                                preferred_element_type=F32)
        return carry

    lax.fori_loop(0, nc, step, 0)
    if emit_state:
        for d, hh in chains:
            sfin_ref[0, d, hh] = st_s[d * hps + hh]

    for hh in range(hps):
        o = o_s[:, hl(hh)]
        oc_ref[:, hl(hh)] = (_rms(o) * ong_ref[...] * _silu(cz_ref[:, hl(hh)])).astype(BF16)


def _gdn(cqkv, conv_w, gcol, grow3, cz, ong, state_c, layer, *, latent):
    n = DEC_SEQ if latent else SEQ
    nseq = DEC_BATCH if latent else BATCH
    hps = GDN_HEADS_PER_STEP_LAT if latent else GDN_HEADS_PER_STEP_CTX
    seq0 = (T_CTX // n) if latent else 0
    nc = n // GDN_CHUNK
    width = hps * LANES
    sect = C_HEADS // hps
    col = lambda base: pl.BlockSpec((n, width), lambda b, h: (seq0 + b, base + h))
    wcol = lambda base: pl.BlockSpec((None, 3, width), lambda b, h: (layer, 0, base + h))
    in_specs = [col(0), col(sect), col(2 * sect), wcol(0), wcol(sect), wcol(2 * sect),
                pl.BlockSpec((n, LANES), lambda b, h: (seq0 + b, 0)),
                pl.BlockSpec((16, nc, 1, LANES), lambda b, h: (0, seq0 + b, 0, 0)),
                pl.BlockSpec((n, width), lambda b, h: (seq0 + b, h)),
                pl.BlockSpec((None, 1, LANES), lambda b, h: (layer, 0, 0))]
    args = [cqkv, cqkv, cqkv, conv_w, conv_w, conv_w, gcol, grow3, cz, ong]
    out_specs = [pl.BlockSpec((n, width), lambda b, h: (b, h))]
    out_shape = [jax.ShapeDtypeStruct((nseq * n, 512), BF16)]
    if latent:
        in_specs.append(pl.BlockSpec((None, None, 2, hps, C_DK, C_DV), lambda b, h: (b, layer, 0, h, 0, 0)))
        args.append(state_c)
    else:
        out_specs.append(pl.BlockSpec((1, 2, hps, C_DK, C_DV), lambda b, h: (b, 0, h, 0, 0)))
        out_shape.append(jax.ShapeDtypeStruct((nseq, 2, C_HEADS, C_DK, C_DV), F32))
    nch = 2 * hps
    scratch = [pltpu.VMEM((n, width), F32), pltpu.VMEM((n, width), F32), pltpu.VMEM((n, width), F32),
               pltpu.VMEM((nch, n, LANES), F32), pltpu.VMEM((nch, n, LANES), BF16),
               pltpu.VMEM((nch, n, LANES), BF16), pltpu.VMEM((nch, n, LANES), BF16),
               pltpu.VMEM((nch, n, LANES), BF16), pltpu.VMEM((nch, nc, 8, LANES), F32),
               pltpu.VMEM((nch, C_DK, C_DV), F32), pltpu.VMEM((n, width), F32)]
    res = pl.pallas_call(
        functools.partial(_gdn_kernel, n=n, hps=hps, has_s0=latent, emit_state=not latent),
        grid=(nseq, C_HEADS // hps),
        in_specs=in_specs,
        out_specs=out_specs,
        out_shape=out_shape,
        scratch_shapes=scratch,
        compiler_params=_cparams("arbitrary", "arbitrary"),
        name="gdn_lat" if latent else "gdn_ctx",
    )(*args)
    return res


def _merge_ffn_kernel(*refs, n_x, tile0, final):
    x_refs, refs = refs[:n_x], refs[n_x:]
    branch_refs, refs = refs[:6], refs[6:]
    gates_ref, mod_ref, n2g_ref, wbr_ref, wout_ref, wff1_ref, wff2_ref, fg_ref, o_ref = refs
    mod = mod_ref[0]
    z = None
    for j in range(3):
        y = jnp.dot(_pick_stream(branch_refs[2 * j:2 * j + 2], tile0), wbr_ref[j], preferred_element_type=F32)
        t = _sigmoid(gates_ref[:, j * D_MODEL:(j + 1) * D_MODEL]) * y
        z = t if z is None else z + t
    x = (_pick_stream(x_refs, tile0)
         + mod[2:3] * jnp.dot(z.astype(BF16), wout_ref[...], preferred_element_type=F32))
    h = (_rms(x) * n2g_ref[...] * (1.0 + mod[4:5]) + mod[3:4]).astype(BF16)
    f = jnp.maximum(jnp.dot(h, wff1_ref[...], preferred_element_type=F32), 0.0)
    f = (f * f).astype(BF16)
    x = x + mod[5:6] * jnp.dot(f, wff2_ref[...], preferred_element_type=F32)
    if final:
        x = _rms(x) * fg_ref[...]
    o_ref[...] = x


def _merge_ffn(xs, branches, gates, mod, wts, final_g, layer, *, tile0, n_tiles, final):
    res = lambda shape: _resident_layer(shape, layer)
    in_specs = _stream_specs(D_MODEL, len(xs), tile0)
    for _ in range(3):
        in_specs += _stream_specs(BRANCH_W, 2, tile0)
    in_specs += [pl.BlockSpec((TM, 3072), lambda i: (tile0 + i, 0)),
                 _mod_spec(layer, tile0),
                 res((1, D_MODEL)),
                 res((3, BRANCH_W, D_MODEL)), res((D_MODEL, D_MODEL)),
                 res((D_MODEL, D_FF)), res((D_FF, D_MODEL)), _resident((1, D_MODEL))]
    return pl.pallas_call(
        functools.partial(_merge_ffn_kernel, n_x=len(xs), tile0=tile0, final=final),
        grid=(n_tiles,),
        in_specs=in_specs,
        out_specs=pl.BlockSpec((TM, D_MODEL), lambda i: (i, 0)),
        out_shape=jax.ShapeDtypeStruct((n_tiles * TM, D_MODEL), F32),
        compiler_params=_cparams("arbitrary"),
        name="merge_ffn",
    )(*xs, *branches, gates, mod, wts["n2g"], wts["wbr"], wts["wout"], wts["wff1"], wts["wff2"], final_g)


def _rope_tables(r):
    half, quarter = r // 2, r // 4
    t = jnp.arange(DEC_SEQ)
    inv_freq = jnp.power(ROPE_BASE, -jnp.arange(quarter, dtype=F32) / quarter)
    cs, ss = [], []
    for pos in (t // GRID_W, t % GRID_W):
        ang = pos.astype(F32)[:, None] * inv_freq[None, :]
        c, s = jnp.cos(ang), jnp.sin(ang)
        cs += [c, c]
        ss += [-s, s]
    return jnp.concatenate(cs, axis=1), jnp.concatenate(ss, axis=1)


def _tables():
    def with_identity(c, s):
        one = jnp.ones((TM, LANES), F32)
        return jnp.concatenate([one, c], axis=0), jnp.concatenate([0.0 * one, s], axis=0)

    c64, s64 = _rope_tables(A_HEAD_DIM)
    ca, sa = with_identity(jnp.tile(c64, (1, 2)), jnp.tile(s64, (1, 2)))
    c32, s32 = _rope_tables(B_ROPE)
    ones = lambda w: jnp.ones((DEC_SEQ, w), F32)
    zeros = lambda w: jnp.zeros((DEC_SEQ, w), F32)
    cs, ss = with_identity(jnp.concatenate([c32, ones(96)], axis=1), jnp.concatenate([s32, zeros(96)], axis=1))
    cq, sq = with_identity(jnp.concatenate([ones(64), c32, ones(32)], axis=1),
                           jnp.concatenate([zeros(64), s32, zeros(32)], axis=1))
    return {"ca": ca, "sa": sa, "cs": cs, "ss": ss, "cq": cq, "sq": sq}


_A_HEAD_PERM = [hk * A_GROUP + g for g in range(A_GROUP) for hk in range(A_KV_HEADS)]


def _prep_weights(norm1_g, norm2_g, w_in, b_gq, b_w_uq, b_gkv, b_w_ukv, c_a_log, c_dt_bias, c_onorm_g,
                  w_branch, w_out, w_ff1, w_ff2):
    perm = jnp.array(_A_HEAD_PERM)
    offs = [0]
    for s in (512, 128, 128, B_Q_LORA, B_KV_LORA, B_ROPE, 1536, 8, 8, 512, 3072):
        offs.append(offs[-1] + s)
    a_q, a_k, a_v, b_cq, b_ckv, b_kr, c_qkv, c_a, c_b, c_z, gates = (
        w_in[:, :, offs[i]:offs[i + 1]] for i in range(11))
    a_q = a_q.reshape(DEPTH, D_MODEL, A_HEADS, A_HEAD_DIM)[:, :, perm].reshape(DEPTH, D_MODEL, 512)
    small = jnp.concatenate([b_kr, c_a, c_b, jnp.zeros((DEPTH, D_MODEL, LANES - 48), F32)], axis=2)
    w_r = jnp.concatenate([a_q, a_k, a_v, b_cq, b_ckv, small, c_qkv, c_z, gates], axis=2).astype(BF16)

    pad_lane = lambda v: jnp.zeros((DEPTH, 1, LANES), F32).at[:, 0, SM_A:SM_A + 8].set(v.reshape(DEPTH, 8))
    uq = b_w_uq.reshape(DEPTH, B_Q_LORA, B_HEADS, B_NOPE + B_ROPE)
    uq = jnp.pad(uq, ((0, 0), (0, 0), (0, 0), (0, LANES - B_NOPE - B_ROPE)))
    uq = uq.reshape(DEPTH, B_Q_LORA, 1024).astype(BF16)
    ukv = b_w_ukv.reshape(DEPTH, B_KV_LORA, B_HEADS, B_NOPE + B_V)
    ukk = jnp.pad(ukv[..., :B_NOPE], ((0, 0), (0, 0), (0, 0), (0, LANES - B_NOPE)))
    ukk = ukk.reshape(DEPTH, B_KV_LORA, 1024).astype(BF16)
    ukvv = ukv[..., B_NOPE:].reshape(DEPTH, B_KV_LORA, 512).astype(BF16)
    src = jnp.arange(LANES)[:, None]
    dst = jnp.arange(1024)[None, :]
    pkr = ((dst % LANES == B_NOPE + src) & (src < B_ROPE)).astype(BF16)

    wbr_a = w_branch[:, 0].reshape(DEPTH, A_HEADS, A_HEAD_DIM, D_MODEL)[:, perm].reshape(DEPTH, BRANCH_W, D_MODEL)
    wbr = jnp.stack([wbr_a, w_branch[:, 1], w_branch[:, 2]], axis=1).astype(BF16)
    return {"w_in": w_r, "alog": pad_lane(c_a_log), "dtb": pad_lane(c_dt_bias),
            "n1g": norm1_g.reshape(DEPTH, 1, D_MODEL), "n2g": norm2_g.reshape(DEPTH, 1, D_MODEL),
            "gq": b_gq.reshape(DEPTH, 1, B_Q_LORA), "gkv": b_gkv.reshape(DEPTH, 1, B_KV_LORA),
            "ong": c_onorm_g.reshape(DEPTH, 1, C_DV),
            "wuq": uq, "wukk": ukk, "pkr": pkr, "wukv": ukvv,
            "wbr": wbr, "wout": w_out.astype(BF16), "wff1": w_ff1.astype(BF16), "wff2": w_ff2.astype(BF16)}


def kernel(x_prompt, x_sample, cache_a_k, cache_a_v, cache_b_ckv, cache_b_krope, state_c, c, c_ctx, norm1_g,
           norm2_g, w_ada, b_ada, w_in, a_sink, b_gq, b_w_uq, b_gkv, b_w_ukv, c_conv_w, c_a_log, c_dt_bias,
           c_onorm_g, w_branch, w_out, w_ff1, w_ff2, final_g):
    xs = (x_prompt.reshape(T_CTX, D_MODEL), x_sample.reshape(T_LAT, D_MODEL))
    cond8 = jnp.concatenate([c_ctx[None, :], c, jnp.zeros((N_GROUPS - 1 - DEC_BATCH, D_MODEL), F32)], axis=0)
    mod = _adaln(cond8, w_ada, b_ada).reshape(DEPTH, N_GROUPS, 6, D_MODEL)
    tabs = _tables()
    fg = final_g.reshape(1, D_MODEL)
    wts = _prep_weights(norm1_g, norm2_g, w_in, b_gq, b_w_uq, b_gkv, b_w_ukv, c_a_log, c_dt_bias, c_onorm_g,
                        w_branch, w_out, w_ff1, w_ff2)
    ck = cache_a_k.reshape(DEC_BATCH, DEPTH, PAST_LEN, LANES)
    cv = cache_a_v.reshape(DEC_BATCH, DEPTH, PAST_LEN, LANES)
    cache_kr = jnp.pad(cache_b_krope, ((0, 0), (0, 0), (0, 0), (0, LANES - B_ROPE)))

    ak_l, av_l, ckv_l, kr_l, sc_l = [], [], [], [], []
    for l in range(DEPTH):
        (aq, ak, av, bq, bk, bv, ckv, small, cqkv, gcol, grow, cz, gates) = _inproj(xs, mod, wts, tabs, l)

        oa_ctx = _attn_a_ctx(a_sink, aq, ak, av, l)
        oa_lat = _attn_a_lat(a_sink, aq, ak, av, ck, cv, l)

        k2, v2 = _mla_cache(cache_b_ckv, cache_kr, wts, l)
        ob_ctx = _mla_ctx(bq, bk, bv)
        ob_lat = _mla_lat(bq, bk, bv, k2, v2)

        grow3 = grow.reshape(16, T_ALL // GDN_CHUNK, 1, GDN_CHUNK)
        oc_ctx, s_fin = _gdn(cqkv, c_conv_w, gcol, grow3, cz, wts["ong"], state_c, l, latent=False)
        (oc_lat,) = _gdn(cqkv, c_conv_w, gcol, grow3, cz, wts["ong"], state_c, l, latent=True)

        branches = (oa_ctx, oa_lat, ob_ctx, ob_lat, oc_ctx, oc_lat)
        merge = functools.partial(_merge_ffn, xs, branches, gates, mod, wts, fg, l)
        if l < DEPTH - 1:
            xs = (merge(tile0=0, n_tiles=N_TILES, final=False),)
        else:
            xs = (merge(tile0=0, n_tiles=N_CTX_TILES, final=True),
                  merge(tile0=N_CTX_TILES, n_tiles=N_TILES - N_CTX_TILES, final=True))

        ak_l.append(ak[:T_CTX].reshape(BATCH, SEQ, A_KV_HEADS, A_HEAD_DIM))
        av_l.append(av[:T_CTX].reshape(BATCH, SEQ, A_KV_HEADS, A_HEAD_DIM))
        ckv_l.append(ckv[:T_CTX].reshape(BATCH, SEQ, B_KV_LORA))
        kr_l.append(small[:T_CTX, :B_ROPE].reshape(BATCH, SEQ, B_ROPE))
        sc_l.append(s_fin)

    y_prompt = xs[0].reshape(BATCH, SEQ, D_MODEL)
    y_sample = xs[1].reshape(DEC_BATCH, DEC_SEQ, D_MODEL)
    return (y_prompt, y_sample, jnp.stack(ak_l, axis=1), jnp.stack(av_l, axis=1), jnp.stack(ckv_l, axis=1),
            jnp.stack(kr_l, axis=1), jnp.stack(sc_l, axis=1))
```

```python
import functools
import math

import jax
import jax.numpy as jnp
from jax import lax
from jax.experimental import pallas as pl
from jax.experimental.pallas import tpu as pltpu

F32 = jnp.float32
BF16 = jnp.bfloat16

D_MODEL = 1024
BATCH = 16
SEQ = 256
DEPTH = 2
DEC_BATCH = 4
DEC_SEQ = 2048
PAST_LEN = 256
GRID_W = 64
ROPE_BASE = 10000.0
EPS = 1e-6
NEG_INF = -1e30

A_HEADS = 8
A_KV_HEADS = 2
A_GROUP = 4
A_HEAD_DIM = 64
A_WINDOW = 128
A_SCALE = A_HEAD_DIM ** -0.5
B_HEADS = 8
B_Q_LORA = 384
B_KV_LORA = 256
B_NOPE = 64
B_ROPE = 32
B_V = 64
MLA_SCALE = (B_NOPE + B_ROPE) ** -0.5
LOG2E = math.log2(math.e)
C_HEADS = 4
C_DK = 128
C_DV = 128
D_FF = 4 * D_MODEL
BRANCH_W = 512

LANES = 128

T_CTX = BATCH * SEQ
T_LAT = DEC_BATCH * DEC_SEQ
T_ALL = T_CTX + T_LAT
TM = 256
N_CTX_TILES = T_CTX // TM
LAT_TILES_PER_SEQ = DEC_SEQ // TM
N_TILES = T_ALL // TM
N_GROUPS = 8
GDN_CHUNK = 128
GDN_PREP_CHAINS = 16
GDN_HEADS_PER_STEP_CTX = 4
GDN_HEADS_PER_STEP_LAT = 2

W_IN_SPLITS = (512, 128, 128, B_Q_LORA, B_KV_LORA, B_ROPE, 1536, 8, 8, 512, 3072)
O_AQ, O_AK, O_AV, O_BCQ, O_BCKV, O_SMALL, W_HEAD_COLS = 0, 512, 640, 768, 1152, 1408, 1536
W_TAIL_COLS = 512 + 3 * D_MODEL
SM_KR, SM_A, SM_B = 0, 32, 40

VMEM_LIMIT = 56 * 1024 * 1024


def _cparams(*sem):
    return pltpu.CompilerParams(dimension_semantics=sem, vmem_limit_bytes=VMEM_LIMIT)


def _resident(shape):
    nd = len(shape)
    return pl.BlockSpec(shape, lambda *_: (0,) * nd, pipeline_mode=pl.Buffered(1))


def _resident_layer(shape, layer):
    nd = len(shape)
    return pl.BlockSpec((None,) + tuple(shape), lambda *_: (layer,) + (0,) * nd, pipeline_mode=pl.Buffered(1))


def _group_of_tile(i):
    return jnp.where(i < N_CTX_TILES, 0, 1 + (i - N_CTX_TILES) // LAT_TILES_PER_SEQ)


def _rope_block_of_tile(i):
    return jnp.where(i < N_CTX_TILES, 0, 1 + (i - N_CTX_TILES) % LAT_TILES_PER_SEQ)


def _mod_spec(layer, tile0=0):
    return pl.BlockSpec((None, 1, 6, D_MODEL), lambda i: (layer, _group_of_tile(tile0 + i), 0, 0))


def _stream_specs(width, n_arrays, tile0=0):
    if n_arrays == 1:
        return [pl.BlockSpec((TM, width), lambda i: (tile0 + i, 0))]
    return [pl.BlockSpec((TM, width), lambda i: (jnp.minimum(tile0 + i, N_CTX_TILES - 1), 0)),
            pl.BlockSpec((TM, width), lambda i: (jnp.maximum(tile0 + i - N_CTX_TILES, 0), 0))]


def _pick_stream(refs, tile0=0):
    if len(refs) == 1:
        return refs[0][...]
    return jnp.where(tile0 + pl.program_id(0) < N_CTX_TILES, refs[0][...], refs[1][...])


def _rms(x):
    return x * lax.rsqrt(jnp.mean(x * x, axis=-1, keepdims=True) + EPS)


def _sigmoid(x):
    return 0.5 * jnp.tanh(0.5 * x) + 0.5


def _silu(x):
    h = 0.5 * x
    return h * jnp.tanh(h) + h


def _softplus(x):
    return jnp.maximum(x, 0.0) + jnp.log(1.0 + jnp.exp(-jnp.abs(x)))


def _dot(a, b):
    return jnp.dot(a.astype(BF16), b.astype(BF16), preferred_element_type=F32)


def _dot_nt(a, b):
    return lax.dot_general(a.astype(BF16), b.astype(BF16), (((1,), (1,)), ((), ())),
                           preferred_element_type=F32)


def _swap_groups(x, q):
    w = x.shape[1]
    lane = lax.broadcasted_iota(jnp.int32, x.shape, 1)
    even = ((lane // q) % 2) == 0
    return jnp.where(even, pltpu.roll(x, w - q, 1), pltpu.roll(x, q, 1))


def _rope(x, cos, sin, q):
    return x * cos + _swap_groups(x, q) * sin


ADA_TN = 1536


def _adaln_kernel(c_ref, w_ref, b_ref, o_ref):
    c = c_ref[...]
    o_ref[0] = jnp.dot(_silu(c), w_ref[0], preferred_element_type=F32) + b_ref[0]


def _adaln(cond8, w_ada, b_ada):
    n = 6 * D_MODEL
    return pl.pallas_call(
        _adaln_kernel,
        grid=(DEPTH, n // ADA_TN),
        in_specs=[
            pl.BlockSpec((N_GROUPS, D_MODEL), lambda l, j: (0, 0)),
            pl.BlockSpec((1, D_MODEL, ADA_TN), lambda l, j: (l, 0, j)),
            pl.BlockSpec((1, 1, ADA_TN), lambda l, j: (l, 0, j)),
        ],
        out_specs=pl.BlockSpec((1, N_GROUPS, ADA_TN), lambda l, j: (l, 0, j)),
        out_shape=jax.ShapeDtypeStruct((DEPTH, N_GROUPS, n), F32),
        compiler_params=_cparams("arbitrary", "arbitrary"),
        name="adaln",
    )(cond8, w_ada, b_ada.reshape(DEPTH, 1, n))


def _chunk_scan(g, forward):
    n = g.shape[0]
    pos = lax.broadcasted_iota(jnp.int32, g.shape, 0) % GDN_CHUNK
    s = 1
    while s < GDN_CHUNK:
        if forward:
            g = g + jnp.where(pos >= s, pltpu.roll(g, s, 0), 0.0)
        else:
            g = g + jnp.where(pos < GDN_CHUNK - s, pltpu.roll(g, n - s, 0), 0.0)
        s *= 2
    return g


def _inproj_kernel(*refs, n_x):
    x_refs, refs = refs[:n_x], refs[n_x:]
    (mod_ref, n1g_ref, w_head_ref, w_cqkv_ref, w_tail_ref,
     ca_ref, sa_ref, cs_ref, ss_ref, cq_ref, sq_ref,
     alog_ref, dtb_ref, gq_ref, gkv_ref, wuq_ref, wukk_ref, pkr_ref, wukv_ref,
     aq_ref, ak_ref, av_ref, bq_ref, bk_ref, bv_ref, ckv_ref, small_ref,
     cqkv_ref, gcol_ref, grow_ref, cz_ref, gates_ref) = refs
    mod = mod_ref[0]
    h = _rms(_pick_stream(x_refs)) * n1g_ref[...] * (1.0 + mod[1:2]) + mod[0:1]
    hb = h.astype(BF16)

    def proj(w_ref, lo, hi):
        return jnp.dot(hb, w_ref[:, lo:hi], preferred_element_type=F32)

    ca, sa = ca_ref[...], sa_ref[...]
    a_all = proj(w_head_ref, O_AQ, O_BCQ)
    for c in range(4):
        q = a_all[:, c * LANES:(c + 1) * LANES]
        aq_ref[:, c * LANES:(c + 1) * LANES] = (_rope(q, ca, sa, 16) * (A_SCALE * LOG2E)).astype(BF16)
    ak_ref[...] = _rope(a_all[:, O_AK:O_AV], ca, sa, 16)
    av_ref[...] = a_all[:, O_AV:O_BCQ]

    b_all = proj(w_head_ref, O_BCQ, W_HEAD_COLS)
    cq = (_rms(b_all[:, :B_Q_LORA]) * gq_ref[...]).astype(BF16)
    cqc, sqc = cq_ref[...], sq_ref[...]
    q_all = jnp.dot(cq, wuq_ref[...], preferred_element_type=F32)
    for hh in range(B_HEADS):
        qh = q_all[:, hh * LANES:(hh + 1) * LANES]
        bq_ref[:, hh * LANES:(hh + 1) * LANES] = (_rope(qh, cqc, sqc, 8) * (MLA_SCALE * LOG2E)).astype(BF16)

    ckv = _rms(b_all[:, B_Q_LORA:B_Q_LORA + B_KV_LORA]) * gkv_ref[...]
    ckv_ref[...] = ckv
    ckvb = ckv.astype(BF16)
    small = b_all[:, O_SMALL - O_BCQ:]
    small_r = _rope(small, cs_ref[...], ss_ref[...], 8)
    small_ref[...] = small_r
    krb = small_r.astype(BF16)
    bk_ref[...] = (jnp.dot(ckvb, wukk_ref[...], preferred_element_type=F32)
                   + jnp.dot(krb, pkr_ref[...], preferred_element_type=F32)).astype(BF16)
    bv_ref[...] = jnp.dot(ckvb, wukv_ref[...], preferred_element_type=F32).astype(BF16)

    for c in range(3):
        cqkv_ref[:, c * 512:(c + 1) * 512] = proj(w_cqkv_ref, c * 512, (c + 1) * 512)
    cz_ref[...] = proj(w_tail_ref, 0, 512)
    for c in range(3):
        gates_ref[:, c * D_MODEL:(c + 1) * D_MODEL] = proj(w_tail_ref, 512 + c * D_MODEL, 512 + (c + 1) * D_MODEL)

    graw = -jnp.exp(alog_ref[...]) * _softplus(small + dtb_ref[...])
    lane = lax.broadcasted_iota(jnp.int32, small.shape, 1)
    gcol = jnp.where(lane < SM_A + C_HEADS, _chunk_scan(graw, True), _chunk_scan(graw, False))
    gcol = jnp.where((lane >= SM_A) & (lane < SM_B), gcol,
                     jnp.where((lane >= SM_B) & (lane < SM_B + 2 * C_HEADS), _sigmoid(small), 0.0))
    gcol_ref[...] = gcol
    grow_ref[...] = gcol.T[SM_A:SM_A + 16, :]


def _inproj(xs, mod, wts, tabs, layer):
    tile = lambda w: pl.BlockSpec((TM, w), lambda i: (i, 0))
    rope_spec = pl.BlockSpec((TM, LANES), lambda i: (_rope_block_of_tile(i), 0))
    res = lambda shape: _resident_layer(shape, layer)
    out_shapes = [
        jax.ShapeDtypeStruct((T_ALL, 512), BF16),
        jax.ShapeDtypeStruct((T_ALL, 128), F32),
        jax.ShapeDtypeStruct((T_ALL, 128), F32),
        jax.ShapeDtypeStruct((T_ALL, 1024), BF16),
        jax.ShapeDtypeStruct((T_ALL, 1024), BF16),
        jax.ShapeDtypeStruct((T_ALL, 512), BF16),
        jax.ShapeDtypeStruct((T_ALL, 256), F32),
        jax.ShapeDtypeStruct((T_ALL, 128), F32),
        jax.ShapeDtypeStruct((T_ALL, 1536), F32),
        jax.ShapeDtypeStruct((T_ALL, 128), F32),
        jax.ShapeDtypeStruct((16, T_ALL), F32),
        jax.ShapeDtypeStruct((T_ALL, 512), F32),
        jax.ShapeDtypeStruct((T_ALL, 3072), F32),
    ]
    out_specs = [tile(512), tile(128), tile(128), tile(1024), tile(1024), tile(512), tile(256), tile(128),
                 tile(1536), tile(128), pl.BlockSpec((16, TM), lambda i: (0, i)), tile(512), tile(3072)]
    in_specs = _stream_specs(D_MODEL, len(xs)) + [
        _mod_spec(layer),
        res((1, D_MODEL)),
        res((D_MODEL, W_HEAD_COLS)), res((D_MODEL, 1536)), res((D_MODEL, W_TAIL_COLS)),
        rope_spec, rope_spec, rope_spec, rope_spec, rope_spec, rope_spec,
        res((1, LANES)), res((1, LANES)),
        res((1, B_Q_LORA)), res((1, B_KV_LORA)),
        res((B_Q_LORA, 1024)), res((B_KV_LORA, 1024)), _resident((LANES, 1024)),
        res((B_KV_LORA, 512)),
    ]
    return pl.pallas_call(
        functools.partial(_inproj_kernel, n_x=len(xs)),
        grid=(N_TILES,),
        in_specs=in_specs,
        out_specs=out_specs,
        out_shape=out_shapes,
        compiler_params=_cparams("arbitrary"),
        name="inproj",
    )(*xs, mod, wts["n1g"], wts["w_head"], wts["w_cqkv"], wts["w_tail"],
      tabs["ca"], tabs["sa"], tabs["cs"], tabs["ss"], tabs["cq"], tabs["sq"],
      wts["alog"], wts["dtb"], wts["gq"], wts["gkv"], wts["wuq"], wts["wukk"], wts["pkr"], wts["wukv"])


def _mla_cache_kernel(ckv_ref, kr_ref, wukk_ref, pkr_ref, wukv_ref, k_ref, v_ref):
    ckvb = ckv_ref[...].astype(BF16)
    krb = kr_ref[...].astype(BF16)
    k_ref[...] = (jnp.dot(ckvb, wukk_ref[...], preferred_element_type=F32)
                  + jnp.dot(krb, pkr_ref[...], preferred_element_type=F32)).astype(BF16)
    v_ref[...] = jnp.dot(ckvb, wukv_ref[...], preferred_element_type=F32).astype(BF16)


def _mla_cache(cache_ckv, cache_kr, wts, layer):
    n = DEC_BATCH * PAST_LEN
    res = lambda shape: _resident_layer(shape, layer)
    return pl.pallas_call(
        _mla_cache_kernel,
        grid=(DEC_BATCH,),
        in_specs=[pl.BlockSpec((None, None, PAST_LEN, B_KV_LORA), lambda i: (i, layer, 0, 0)),
                  pl.BlockSpec((None, None, PAST_LEN, LANES), lambda i: (i, layer, 0, 0)),
                  res((B_KV_LORA, 1024)), _resident((LANES, 1024)), res((B_KV_LORA, 512))],
        out_specs=[pl.BlockSpec((PAST_LEN, 1024), lambda i: (i, 0)),
                   pl.BlockSpec((PAST_LEN, 512), lambda i: (i, 0))],
        out_shape=[jax.ShapeDtypeStruct((n, 1024), BF16), jax.ShapeDtypeStruct((n, 512), BF16)],
        compiler_params=_cparams("arbitrary"),
        name="mla_cache",
    )(cache_ckv, cache_kr, wts["wukk"], wts["pkr"], wts["wukv"])


TQ = 256
A_WIN_KEYS = TQ + 2 * A_WINDOW
_NT = (((1,), (1,)), ((), ()))


def _softmax_pv(scores, values, extra_logit=None):
    m = scores[0].max(axis=-1, keepdims=True)
    for s in scores[1:]:
        m = jnp.maximum(m, s.max(axis=-1, keepdims=True))
    if extra_logit is not None:
        m = jnp.maximum(m, extra_logit)
    den = None
    acc = None
    for s, v in zip(scores, values):
        e = jnp.exp2(s - m)
        d = e.sum(axis=-1, keepdims=True)
        den = d if den is None else den + d
        pv = jnp.dot(e.astype(BF16), v, preferred_element_type=F32)
        acc = pv if acc is None else acc + pv
    if extra_logit is not None:
        den = den + jnp.exp2(extra_logit - m)
    return acc / den


def _attn_a_kernel(sink_ref, q_ref, k1_ref, v1_ref, *rest, latent, layer):
    if latent:
        k2_ref, v2_ref, o_ref = rest
        qt = pl.program_id(1)
        t0 = qt * TQ
        start = pl.multiple_of(jnp.clip(t0 - A_WINDOW, 0, DEC_SEQ - A_WIN_KEYS), A_WINDOW)
        k1 = k1_ref[pl.ds(start, A_WIN_KEYS), :]
        v1 = v1_ref[pl.ds(start, A_WIN_KEYS), :]
        qpos = t0 + lax.broadcasted_iota(jnp.int32, (TQ, A_WIN_KEYS), 0)
        kpos = start + lax.broadcasted_iota(jnp.int32, (TQ, A_WIN_KEYS), 1)
        valid = jnp.abs(qpos - kpos) <= A_WINDOW
    else:
        (o_ref,) = rest
        k1 = k1_ref[...]
        v1 = v1_ref[...]
    both = lambda x: (x.astype(BF16), pltpu.roll(x, A_HEAD_DIM, 1).astype(BF16))
    k1, v1 = both(k1), both(v1)
    if latent:
        k2, v2 = both(k2_ref[...]), both(v2_ref[...])
    lane = lax.broadcasted_iota(jnp.int32, (TQ, LANES), 1)
    low = lane < A_HEAD_DIM
    for c in range(A_HEADS // 2):
        qc = q_ref[:, c * LANES:(c + 1) * LANES]
        hk = (2 * c) // A_GROUP
        outs = []
        for p in range(2):
            sel = 0 if p == hk else 1
            qz = jnp.where(low if p == 0 else ~low, qc, jnp.zeros_like(qc))
            s1 = lax.dot_general(qz, k1[sel], _NT, preferred_element_type=F32)
            sink = sink_ref[layer, 2 * c + p] * LOG2E
            if latent:
                s1 = jnp.where(valid, s1, NEG_INF)
                s2 = lax.dot_general(qz, k2[sel], _NT, preferred_element_type=F32)
                outs.append(_softmax_pv([s1, s2], [v1[sel], v2[sel]], sink))
            else:
                outs.append(_softmax_pv([s1], [v1[sel]], sink))
        o_ref[:, c * LANES:(c + 1) * LANES] = jnp.where(low, outs[0], outs[1]).astype(BF16)


def _attn_a_ctx(sink, aq, ak, av, layer):
    nb = T_CTX // TQ
    return pl.pallas_call(
        functools.partial(_attn_a_kernel, latent=False, layer=layer),
        grid=(nb,),
        in_specs=[pl.BlockSpec(memory_space=pltpu.SMEM),
                  pl.BlockSpec((TQ, 512), lambda b: (b, 0)),
                  pl.BlockSpec((SEQ, LANES), lambda b: (b, 0)),
                  pl.BlockSpec((SEQ, LANES), lambda b: (b, 0))],
        out_specs=pl.BlockSpec((TQ, 512), lambda b: (b, 0)),
        out_shape=jax.ShapeDtypeStruct((T_CTX, 512), BF16),
        compiler_params=_cparams("arbitrary"),
        name="attn_a_ctx",
    )(sink, aq, ak, av)


def _attn_a_lat(sink, aq, ak, av, ck, cv, layer):
    nq = DEC_SEQ // TQ
    ctx_tiles = T_CTX // TQ
    seq_blocks = T_CTX // DEC_SEQ
    cache_spec = pl.BlockSpec((None, None, PAST_LEN, LANES), lambda b, i: (b, layer, 0, 0))
    return pl.pallas_call(
        functools.partial(_attn_a_kernel, latent=True, layer=layer),
        grid=(DEC_BATCH, nq),
        in_specs=[pl.BlockSpec(memory_space=pltpu.SMEM),
                  pl.BlockSpec((TQ, 512), lambda b, i: (ctx_tiles + b * nq + i, 0)),
                  pl.BlockSpec((DEC_SEQ, LANES), lambda b, i: (seq_blocks + b, 0)),
                  pl.BlockSpec((DEC_SEQ, LANES), lambda b, i: (seq_blocks + b, 0)),
                  cache_spec, cache_spec],
        out_specs=pl.BlockSpec((TQ, 512), lambda b, i: (b * nq + i, 0)),
        out_shape=jax.ShapeDtypeStruct((T_LAT, 512), BF16),
        compiler_params=_cparams("arbitrary", "arbitrary"),
        name="attn_a_lat",
    )(sink, aq, ak, av, ck, cv)


def _mla_kernel(q_ref, k1_ref, v1_ref, *rest, latent):
    if latent:
        k2_ref, v2_ref, o_ref = rest
    else:
        (o_ref,) = rest
    lane = lax.broadcasted_iota(jnp.int32, (TQ, LANES), 1)
    low = lane < B_V
    for j in range(B_HEADS // 2):
        outs = []
        for sub in range(2):
            hh = 2 * j + sub
            hs = slice(hh * LANES, (hh + 1) * LANES)
            vs = slice(j * LANES, (j + 1) * LANES)
            qh = q_ref[:, hs]
            s1 = lax.dot_general(qh, k1_ref[:, hs], _NT, preferred_element_type=F32)
            if latent:
                s2 = lax.dot_general(qh, k2_ref[:, hs], _NT, preferred_element_type=F32)
                outs.append(_softmax_pv([s1, s2], [v1_ref[:, vs], v2_ref[:, vs]]))
            else:
                outs.append(_softmax_pv([s1], [v1_ref[:, vs]]))
        o_ref[:, j * LANES:(j + 1) * LANES] = jnp.where(low, outs[0], outs[1]).astype(BF16)


def _mla_ctx(bq, bk, bv):
    nb = T_CTX // TQ
    return pl.pallas_call(
        functools.partial(_mla_kernel, latent=False),
        grid=(nb,),
        in_specs=[pl.BlockSpec((TQ, 1024), lambda b: (b, 0)),
                  pl.BlockSpec((SEQ, 1024), lambda b: (b, 0)),
                  pl.BlockSpec((SEQ, 512), lambda b: (b, 0))],
        out_specs=pl.BlockSpec((TQ, 512), lambda b: (b, 0)),
        out_shape=jax.ShapeDtypeStruct((T_CTX, 512), BF16),
        compiler_params=_cparams("arbitrary"),
        name="mla_ctx",
    )(bq, bk, bv)


def _mla_lat(bq, bk, bv, k2, v2):
    nq = DEC_SEQ // TQ
    ctx_tiles = T_CTX // TQ
    seq_blocks = T_CTX // DEC_SEQ
    return pl.pallas_call(
        functools.partial(_mla_kernel, latent=True),
        grid=(DEC_BATCH, nq),
        in_specs=[pl.BlockSpec((TQ, 1024), lambda b, i: (ctx_tiles + b * nq + i, 0)),
                  pl.BlockSpec((DEC_SEQ, 1024), lambda b, i: (seq_blocks + b, 0)),
                  pl.BlockSpec((DEC_SEQ, 512), lambda b, i: (seq_blocks + b, 0)),
                  pl.BlockSpec((PAST_LEN, 1024), lambda b, i: (b, 0)),
                  pl.BlockSpec((PAST_LEN, 512), lambda b, i: (b, 0))],
        out_specs=pl.BlockSpec((TQ, 512), lambda b, i: (b * nq + i, 0)),
        out_shape=jax.ShapeDtypeStruct((T_LAT, 512), BF16),
        compiler_params=_cparams("arbitrary", "arbitrary"),
        name="mla_lat",
    )(bq, bk, bv, k2, v2)


def _gdn_kernel(xq_ref, xk_ref, xv_ref, wq_ref, wk_ref, wv_ref, gcol_ref, grow_ref, cz_ref, ong_ref, *rest,
                n, hps, has_s0, emit_state):
    rest = list(rest)
    s0_ref = rest.pop(0) if has_s0 else None
    oc_ref = rest.pop(0)
    sfin_ref = rest.pop(0) if emit_state else None
    q_s, k_s, v_s, u_s, w_s, qk_s, qd_s, kdt_s, ge_s, st_s, o_s = rest
    nc = n // GDN_CHUNK
    head0 = pl.program_id(1) * hps
    C = GDN_CHUNK
    width = hps * LANES
    hl = lambda hh: slice(hh * LANES, (hh + 1) * LANES)
    chains = [(d, hh) for d in range(2) for hh in range(hps)]

    row8 = lax.broadcasted_iota(jnp.int32, (8, width), 0)

    def conv_silu(x_ref, w_ref):
        x = x_ref[...]
        w = w_ref[...]
        y = pltpu.roll(x, 1, 0) * w[0:1] + x * w[1:2] + pltpu.roll(x, n - 1, 0) * w[2:3]
        top = y[0:8] - jnp.where(row8 == 0, x[n - 1:n] * w[0:1], 0.0)
        bot = y[n - 8:n] - jnp.where(row8 == 7, x[0:1] * w[2:3], 0.0)
        return _silu(jnp.concatenate([top, y[8:n - 8], bot], axis=0))

    def l2n(x):
        return x * lax.rsqrt(jnp.sum(x * x, axis=-1, keepdims=True) + EPS)

    qc = conv_silu(xq_ref, wq_ref)
    kc = conv_silu(xk_ref, wk_ref)
    for hh in range(hps):
        q_s[:, hl(hh)] = l2n(qc[:, hl(hh)]) * (C_DK ** -0.5)
        k_s[:, hl(hh)] = l2n(kc[:, hl(hh)])
    v_s[...] = conv_silu(xv_ref, wv_ref)
    o_s[...] = jnp.zeros_like(o_s)

    lane_c = lax.broadcasted_iota(jnp.int32, (C, LANES), 1)
    ri = lax.broadcasted_iota(jnp.int32, (C, C), 0)
    ci = lax.broadcasted_iota(jnp.int32, (C, C), 1)
    eye = (ri == ci).astype(F32)
    same_block = {}
    s = 2
    while s <= C:
        same_block[s] = (ri // s) == (ci // s)
        s *= 2

    def prep(it, carry):
        items = []
        for gi in range(group):
            r = it * group + gi
            rows = pl.ds(pl.multiple_of(r * C, C), C)
            gtile = gcol_ref[rows, :]
            for hh in range(hps):
                q = q_s[rows, hl(hh)]
                k = k_s[rows, hl(hh)]
                v = v_s[rows, hl(hh)]
                qk_raw = _dot_nt(q, k)
                for d in range(2):
                    slot = d * hps + hh
                    j = d * C_HEADS + head0 + hh
                    gc = jnp.sum(jnp.where(lane_c == SM_A + j, gtile, 0.0), axis=1, keepdims=True)
                    beta = jnp.sum(jnp.where(lane_c == SM_B + j, gtile, 0.0), axis=1, keepdims=True)
                    grow = grow_ref[j, r]
                    incl = (ri >= ci) if d == 0 else (ri <= ci)
                    strict = (ri > ci) if d == 0 else (ri < ci)
                    decay = jnp.where(incl, jnp.exp(jnp.where(incl, gc - grow, 0.0)), 0.0)
                    kb = k * beta
                    a = jnp.where(strict, _dot_nt(kb, k) * decay, 0.0)
                    g_end = gc[C - 1:C, :] if d == 0 else gc[0:1, :]
                    egc = jnp.exp(gc)
                    qk_s[slot, rows, :] = (qk_raw * decay).astype(BF16)
                    qd_s[slot, rows, :] = (q * egc).astype(BF16)
                    kdt_s[slot, rows, :] = (k * jnp.exp(g_end - gc)).T.astype(BF16)
                    ge_s[slot, r] = jnp.broadcast_to(jnp.exp(g_end), (8, LANES))
                    items.append((slot, rows, a, jnp.concatenate([v * beta, kb * egc], axis=1), d))
        ts = [eye - jnp.where(same_block[2], a, 0.0) for (_, _, a, _, _) in items]
        s = 2
        while s < C:
            level = same_block[2 * s] & ~same_block[s]
            es = [jnp.where(level, a, 0.0) for (_, _, a, _, _) in items]
            if s < 8:
                xs = [_dot(t, e) for t, e in zip(ts, es)]
                ts = [t - _dot(x, t) for t, x in zip(ts, xs)]
            else:
                def active(t, d):
                    return jnp.concatenate([t[b * s:(b + 1) * s] for b in range(1 - d, C // s, 2)], axis=0)

                def merged(t, upd, d):
                    return jnp.concatenate(
                        [upd[(b // 2) * s:(b // 2 + 1) * s] if b % 2 == 1 - d else t[b * s:(b + 1) * s]
                         for b in range(C // s)], axis=0)

                halves = [active(t, it_[4]) for t, it_ in zip(ts, items)]
                xs = [_dot(h, e) for h, e in zip(halves, es)]
                upds = [h - _dot(x, t) for h, x, t in zip(halves, xs, ts)]
                ts = [merged(t, u, it_[4]) for t, u, it_ in zip(ts, upds, items)]
            s *= 2
        for t, (slot, rows, _, rhs, _) in zip(ts, items):
            uw = _dot(t, rhs)
            u_s[slot, rows, :] = uw[:, :C_DV]
            w_s[slot, rows, :] = uw[:, C_DV:].astype(BF16)
        return carry

    group = max(1, min(nc, GDN_PREP_CHAINS // (2 * hps)))
    lax.fori_loop(0, nc // group, prep, 0)

    for d, hh in chains:
        if has_s0:
            st_s[d * hps + hh] = s0_ref[d, hh]
        else:
            st_s[d * hps + hh] = jnp.zeros((C_DK, C_DV), F32)

    def step(i, carry):
        rows = [pl.ds(pl.multiple_of(i * C, C), C), pl.ds(pl.multiple_of((nc - 1 - i) * C, C), C)]
        chunk = [i, nc - 1 - i]
        slots = [d * hps + hh for d, hh in chains]
        st = [st_s[c] for c in slots]
        stb = [s_.astype(BF16) for s_ in st]
        ws = [jnp.dot(w_s[c, rows[d], :], sb, preferred_element_type=F32)
              for c, (d, _), sb in zip(slots, chains, stb)]
        vnb = [(u_s[c, rows[d], :] - x).astype(BF16) for c, (d, _), x in zip(slots, chains, ws)]
        for c, (d, hh), sb, vb, s_ in zip(slots, chains, stb, vnb, st):
            o = (jnp.dot(qd_s[c, rows[d], :], sb, preferred_element_type=F32)
                 + jnp.dot(qk_s[c, rows[d], :], vb, preferred_element_type=F32))
            o_s[rows[d], hl(hh)] += o
            st_s[c] = s_ * ge_s[c, chunk[d]][0:1, :] + jnp.dot(kdt_s[c, rows[d], :], vb,
                                                                preferred_element_type=F32)
        return carry

    lax.fori_loop(0, nc, step, 0)
    if emit_state:
        for d, hh in chains:
            sfin_ref[0, d, hh] = st_s[d * hps + hh]

    for hh in range(hps):
        o = o_s[:, hl(hh)]
        oc_ref[:, hl(hh)] = (_rms(o) * ong_ref[...] * _silu(cz_ref[:, hl(hh)])).astype(BF16)


def _gdn(cqkv, conv_w, gcol, grow3, cz, ong, state_c, layer, *, latent):
    n = DEC_SEQ if latent else SEQ
    nseq = DEC_BATCH if latent else BATCH
    hps = GDN_HEADS_PER_STEP_LAT if latent else GDN_HEADS_PER_STEP_CTX
    seq0 = (T_CTX // n) if latent else 0
    nc = n // GDN_CHUNK
    width = hps * LANES
    sect = C_HEADS // hps
    col = lambda base: pl.BlockSpec((n, width), lambda b, h: (seq0 + b, base + h))
    wcol = lambda base: pl.BlockSpec((None, 3, width), lambda b, h: (layer, 0, base + h))
    in_specs = [col(0), col(sect), col(2 * sect), wcol(0), wcol(sect), wcol(2 * sect),
                pl.BlockSpec((n, LANES), lambda b, h: (seq0 + b, 0)),
                pl.BlockSpec((16, nc, 1, LANES), lambda b, h: (0, seq0 + b, 0, 0)),
                pl.BlockSpec((n, width), lambda b, h: (seq0 + b, h)),
                pl.BlockSpec((None, 1, LANES), lambda b, h: (layer, 0, 0))]
    args = [cqkv, cqkv, cqkv, conv_w, conv_w, conv_w, gcol, grow3, cz, ong]
    out_specs = [pl.BlockSpec((n, width), lambda b, h: (b, h))]
    out_shape = [jax.ShapeDtypeStruct((nseq * n, 512), BF16)]
    if latent:
        in_specs.append(pl.BlockSpec((None, None, 2, hps, C_DK, C_DV), lambda b, h: (b, layer, 0, h, 0, 0)))
        args.append(state_c)
    else:
        out_specs.append(pl.BlockSpec((1, 2, hps, C_DK, C_DV), lambda b, h: (b, 0, h, 0, 0)))
        out_shape.append(jax.ShapeDtypeStruct((nseq, 2, C_HEADS, C_DK, C_DV), F32))
    nch = 2 * hps
    scratch = [pltpu.VMEM((n, width), F32), pltpu.VMEM((n, width), F32), pltpu.VMEM((n, width), F32),
               pltpu.VMEM((nch, n, LANES), F32), pltpu.VMEM((nch, n, LANES), BF16),
               pltpu.VMEM((nch, n, LANES), BF16), pltpu.VMEM((nch, n, LANES), BF16),
               pltpu.VMEM((nch, n, LANES), BF16), pltpu.VMEM((nch, nc, 8, LANES), F32),
               pltpu.VMEM((nch, C_DK, C_DV), F32), pltpu.VMEM((n, width), F32)]
    res = pl.pallas_call(
        functools.partial(_gdn_kernel, n=n, hps=hps, has_s0=latent, emit_state=not latent),
        grid=(nseq, C_HEADS // hps),
        in_specs=in_specs,
        out_specs=out_specs,
        out_shape=out_shape,
        scratch_shapes=scratch,
        compiler_params=_cparams("arbitrary", "arbitrary"),
        name="gdn_lat" if latent else "gdn_ctx",
    )(*args)
    return res


def _merge_ffn_kernel(*refs, n_x, tile0, final):
    x_refs, refs = refs[:n_x], refs[n_x:]
    branch_refs, refs = refs[:6], refs[6:]
    gates_ref, mod_ref, n2g_ref, wbr_ref, wout_ref, wff1_ref, wff2_ref, fg_ref, o_ref = refs
    mod = mod_ref[0]
    z = None
    for j in range(3):
        y = jnp.dot(_pick_stream(branch_refs[2 * j:2 * j + 2], tile0), wbr_ref[j], preferred_element_type=F32)
        t = _sigmoid(gates_ref[:, j * D_MODEL:(j + 1) * D_MODEL]) * y
        z = t if z is None else z + t
    x = (_pick_stream(x_refs, tile0)
         + mod[2:3] * jnp.dot(z.astype(BF16), wout_ref[...], preferred_element_type=F32))
    h = (_rms(x) * n2g_ref[...] * (1.0 + mod[4:5]) + mod[3:4]).astype(BF16)
    f = jnp.maximum(jnp.dot(h, wff1_ref[...], preferred_element_type=F32), 0.0)
    f = (f * f).astype(BF16)
    x = x + mod[5:6] * jnp.dot(f, wff2_ref[...], preferred_element_type=F32)
    if final:
        x = _rms(x) * fg_ref[...]
    o_ref[...] = x


def _merge_ffn(xs, branches, gates, mod, wts, final_g, layer, *, tile0, n_tiles, final):
    res = lambda shape: _resident_layer(shape, layer)
    in_specs = _stream_specs(D_MODEL, len(xs), tile0)
    for _ in range(3):
        in_specs += _stream_specs(BRANCH_W, 2, tile0)
    in_specs += [pl.BlockSpec((TM, 3072), lambda i: (tile0 + i, 0)),
                 _mod_spec(layer, tile0),
                 res((1, D_MODEL)),
                 res((3, BRANCH_W, D_MODEL)), res((D_MODEL, D_MODEL)),
                 res((D_MODEL, D_FF)), res((D_FF, D_MODEL)), _resident((1, D_MODEL))]
    return pl.pallas_call(
        functools.partial(_merge_ffn_kernel, n_x=len(xs), tile0=tile0, final=final),
        grid=(n_tiles,),
        in_specs=in_specs,
        out_specs=pl.BlockSpec((TM, D_MODEL), lambda i: (i, 0)),
        out_shape=jax.ShapeDtypeStruct((n_tiles * TM, D_MODEL), F32),
        compiler_params=_cparams("arbitrary"),
        name="merge_ffn",
    )(*xs, *branches, gates, mod, wts["n2g"], wts["wbr"], wts["wout"], wts["wff1"], wts["wff2"], final_g)


def _rope_tables(r):
    half, quarter = r // 2, r // 4
    t = jnp.arange(DEC_SEQ)
    inv_freq = jnp.power(ROPE_BASE, -jnp.arange(quarter, dtype=F32) / quarter)
    cs, ss = [], []
    for pos in (t // GRID_W, t % GRID_W):
        ang = pos.astype(F32)[:, None] * inv_freq[None, :]
        c, s = jnp.cos(ang), jnp.sin(ang)
        cs += [c, c]
        ss += [-s, s]
    return jnp.concatenate(cs, axis=1), jnp.concatenate(ss, axis=1)


def _tables():
    def with_identity(c, s):
        one = jnp.ones((TM, LANES), F32)
        return jnp.concatenate([one, c], axis=0), jnp.concatenate([0.0 * one, s], axis=0)

    c64, s64 = _rope_tables(A_HEAD_DIM)
    ca, sa = with_identity(jnp.tile(c64, (1, 2)), jnp.tile(s64, (1, 2)))
    c32, s32 = _rope_tables(B_ROPE)
    ones = lambda w: jnp.ones((DEC_SEQ, w), F32)
    zeros = lambda w: jnp.zeros((DEC_SEQ, w), F32)
    cs, ss = with_identity(jnp.concatenate([c32, ones(96)], axis=1), jnp.concatenate([s32, zeros(96)], axis=1))
    cq, sq = with_identity(jnp.concatenate([ones(64), c32, ones(32)], axis=1),
                           jnp.concatenate([zeros(64), s32, zeros(32)], axis=1))
    return {"ca": ca, "sa": sa, "cs": cs, "ss": ss, "cq": cq, "sq": sq}


def _prep_weights(norm1_g, norm2_g, w_in, b_gq, b_w_uq, b_gkv, b_w_ukv, c_a_log, c_dt_bias, c_onorm_g,
                  w_branch, w_out, w_ff1, w_ff2):
    offs = [0]
    for width in W_IN_SPLITS:
        offs.append(offs[-1] + width)
    piece = lambda i, j: w_in[:, :, offs[i]:offs[j]]
    w_head = jnp.concatenate([piece(0, 6), piece(7, 9), jnp.zeros((DEPTH, D_MODEL, LANES - 48), F32)],
                             axis=2).astype(BF16)
    w_cqkv = piece(6, 7).astype(BF16)
    w_tail = piece(9, 11).astype(BF16)

    pad_lane = lambda v: jnp.zeros((DEPTH, 1, LANES), F32).at[:, 0, SM_A:SM_A + 8].set(v.reshape(DEPTH, 8))
    uq = b_w_uq.reshape(DEPTH, B_Q_LORA, B_HEADS, B_NOPE + B_ROPE)
    uq = jnp.pad(uq, ((0, 0), (0, 0), (0, 0), (0, LANES - B_NOPE - B_ROPE)))
    uq = uq.reshape(DEPTH, B_Q_LORA, 1024).astype(BF16)
    ukv = b_w_ukv.reshape(DEPTH, B_KV_LORA, B_HEADS, B_NOPE + B_V)
    ukk = jnp.pad(ukv[..., :B_NOPE], ((0, 0), (0, 0), (0, 0), (0, LANES - B_NOPE)))
    ukk = ukk.reshape(DEPTH, B_KV_LORA, 1024).astype(BF16)
    ukvv = ukv[..., B_NOPE:].reshape(DEPTH, B_KV_LORA, 512).astype(BF16)
    src = jnp.arange(LANES)[:, None]
    dst = jnp.arange(1024)[None, :]
    pkr = ((dst % LANES == B_NOPE + src) & (src < B_ROPE)).astype(BF16)

    return {"w_head": w_head, "w_cqkv": w_cqkv, "w_tail": w_tail,
            "alog": pad_lane(c_a_log), "dtb": pad_lane(c_dt_bias),
            "n1g": norm1_g.reshape(DEPTH, 1, D_MODEL), "n2g": norm2_g.reshape(DEPTH, 1, D_MODEL),
            "gq": b_gq.reshape(DEPTH, 1, B_Q_LORA), "gkv": b_gkv.reshape(DEPTH, 1, B_KV_LORA),
            "ong": c_onorm_g.reshape(DEPTH, 1, C_DV),
            "wuq": uq, "wukk": ukk, "pkr": pkr, "wukv": ukvv,
            "wbr": w_branch.astype(BF16), "wout": w_out.astype(BF16), "wff1": w_ff1.astype(BF16), "wff2": w_ff2.astype(BF16)}


def kernel(x_prompt, x_sample, cache_a_k, cache_a_v, cache_b_ckv, cache_b_krope, state_c, c, c_ctx, norm1_g,
           norm2_g, w_ada, b_ada, w_in, a_sink, b_gq, b_w_uq, b_gkv, b_w_ukv, c_conv_w, c_a_log, c_dt_bias,
           c_onorm_g, w_branch, w_out, w_ff1, w_ff2, final_g):
    xs = (x_prompt.reshape(T_CTX, D_MODEL), x_sample.reshape(T_LAT, D_MODEL))
    cond8 = jnp.concatenate([c_ctx[None, :], c, jnp.zeros((N_GROUPS - 1 - DEC_BATCH, D_MODEL), F32)], axis=0)
    mod = _adaln(cond8, w_ada, b_ada).reshape(DEPTH, N_GROUPS, 6, D_MODEL)
    tabs = _tables()
    fg = final_g.reshape(1, D_MODEL)
    wts = _prep_weights(norm1_g, norm2_g, w_in, b_gq, b_w_uq, b_gkv, b_w_ukv, c_a_log, c_dt_bias, c_onorm_g,
                        w_branch, w_out, w_ff1, w_ff2)
    ck = cache_a_k.reshape(DEC_BATCH, DEPTH, PAST_LEN, LANES)
    cv = cache_a_v.reshape(DEC_BATCH, DEPTH, PAST_LEN, LANES)
    cache_kr = jnp.pad(cache_b_krope, ((0, 0), (0, 0), (0, 0), (0, LANES - B_ROPE)))

    ak_l, av_l, ckv_l, kr_l, sc_l = [], [], [], [], []
    for l in range(DEPTH):
        (aq, ak, av, bq, bk, bv, ckv, small, cqkv, gcol, grow, cz, gates) = _inproj(xs, mod, wts, tabs, l)

        oa_ctx = _attn_a_ctx(a_sink, aq, ak, av, l)
        oa_lat = _attn_a_lat(a_sink, aq, ak, av, ck, cv, l)

        k2, v2 = _mla_cache(cache_b_ckv, cache_kr, wts, l)
        ob_ctx = _mla_ctx(bq, bk, bv)
        ob_lat = _mla_lat(bq, bk, bv, k2, v2)

        grow3 = grow.reshape(16, T_ALL // GDN_CHUNK, 1, GDN_CHUNK)
        oc_ctx, s_fin = _gdn(cqkv, c_conv_w, gcol, grow3, cz, wts["ong"], state_c, l, latent=False)
        (oc_lat,) = _gdn(cqkv, c_conv_w, gcol, grow3, cz, wts["ong"], state_c, l, latent=True)

        branches = (oa_ctx, oa_lat, ob_ctx, ob_lat, oc_ctx, oc_lat)
        merge = functools.partial(_merge_ffn, xs, branches, gates, mod, wts, fg, l)
        if l < DEPTH - 1:
            xs = (merge(tile0=0, n_tiles=N_TILES, final=False),)
        else:
            xs = (merge(tile0=0, n_tiles=N_CTX_TILES, final=True),
                  merge(tile0=N_CTX_TILES, n_tiles=N_TILES - N_CTX_TILES, final=True))

        ak_l.append(ak[:T_CTX].reshape(BATCH, SEQ, A_KV_HEADS, A_HEAD_DIM))
        av_l.append(av[:T_CTX].reshape(BATCH, SEQ, A_KV_HEADS, A_HEAD_DIM))
        ckv_l.append(ckv[:T_CTX].reshape(BATCH, SEQ, B_KV_LORA))
        kr_l.append(small[:T_CTX, :B_ROPE].reshape(BATCH, SEQ, B_ROPE))
        sc_l.append(s_fin)

    y_prompt = xs[0].reshape(BATCH, SEQ, D_MODEL)
    y_sample = xs[1].reshape(DEC_BATCH, DEC_SEQ, D_MODEL)
    return (y_prompt, y_sample, jnp.stack(ak_l, axis=1), jnp.stack(av_l, axis=1), jnp.stack(ckv_l, axis=1),
            jnp.stack(kr_l, axis=1), jnp.stack(sc_l, axis=1))
```

```python
import functools
import math

import jax
import jax.numpy as jnp
from jax import lax
from jax.experimental import pallas as pl
from jax.experimental.pallas import tpu as pltpu

F32 = jnp.float32
BF16 = jnp.bfloat16

D_MODEL = 1024
BATCH = 16
SEQ = 256
DEPTH = 2
DEC_BATCH = 4
DEC_SEQ = 2048
PAST_LEN = 256
GRID_W = 64
ROPE_BASE = 10000.0
EPS = 1e-6
NEG_INF = -1e30

A_HEADS = 8
A_KV_HEADS = 2
A_GROUP = 4
A_HEAD_DIM = 64
A_WINDOW = 128
A_SCALE = A_HEAD_DIM ** -0.5
B_HEADS = 8
B_Q_LORA = 384
B_KV_LORA = 256
B_NOPE = 64
B_ROPE = 32
B_V = 64
MLA_SCALE = (B_NOPE + B_ROPE) ** -0.5
LOG2E = math.log2(math.e)
C_HEADS = 4
C_DK = 128
C_DV = 128
D_FF = 4 * D_MODEL
BRANCH_W = 512

LANES = 128

T_CTX = BATCH * SEQ
T_LAT = DEC_BATCH * DEC_SEQ
T_ALL = T_CTX + T_LAT
TM = 512
N_CTX_TILES = T_CTX // TM
LAT_TILES_PER_SEQ = DEC_SEQ // TM
N_TILES = T_ALL // TM
N_GROUPS = 8
GDN_CHUNK = 128
GDN_PREP_CHAINS = 16
GDN_HEADS_PER_STEP_CTX = 4
GDN_HEADS_PER_STEP_LAT = 2

W_IN_SPLITS = (512, 128, 128, B_Q_LORA, B_KV_LORA, B_ROPE, 1536, 8, 8, 512, 3072)
O_AQ, O_AK, O_AV, O_BCQ, O_BCKV, O_SMALL, W_HEAD_COLS = 0, 512, 640, 768, 1152, 1408, 1536
W_TAIL_COLS = 512 + 3 * D_MODEL
SM_KR, SM_A, SM_B = 0, 32, 40

VMEM_LIMIT = 56 * 1024 * 1024


def _cparams(*sem):
    return pltpu.CompilerParams(dimension_semantics=sem, vmem_limit_bytes=VMEM_LIMIT)


def _resident(shape):
    nd = len(shape)
    return pl.BlockSpec(shape, lambda *_: (0,) * nd, pipeline_mode=pl.Buffered(1))


def _resident_layer(shape, layer):
    nd = len(shape)
    return pl.BlockSpec((None,) + tuple(shape), lambda *_: (layer,) + (0,) * nd, pipeline_mode=pl.Buffered(1))


def _group_of_tile(i):
    return jnp.where(i < N_CTX_TILES, 0, 1 + (i - N_CTX_TILES) // LAT_TILES_PER_SEQ)


def _rope_block_of_tile(i):
    return jnp.where(i < N_CTX_TILES, 0, 1 + (i - N_CTX_TILES) % LAT_TILES_PER_SEQ)


def _mod_spec(layer, tile0=0):
    return pl.BlockSpec((None, 1, 6, D_MODEL), lambda i: (layer, _group_of_tile(tile0 + i), 0, 0))


def _stream_specs(width, n_arrays, tile0=0):
    if n_arrays == 1:
        return [pl.BlockSpec((TM, width), lambda i: (tile0 + i, 0))]
    return [pl.BlockSpec((TM, width), lambda i: (jnp.minimum(tile0 + i, N_CTX_TILES - 1), 0)),
            pl.BlockSpec((TM, width), lambda i: (jnp.maximum(tile0 + i - N_CTX_TILES, 0), 0))]


def _pick_stream(refs, tile0=0):
    if len(refs) == 1:
        return refs[0][...]
    return jnp.where(tile0 + pl.program_id(0) < N_CTX_TILES, refs[0][...], refs[1][...])


def _rms(x):
    return x * lax.rsqrt(jnp.mean(x * x, axis=-1, keepdims=True) + EPS)


def _sigmoid(x):
    return 0.5 * jnp.tanh(0.5 * x) + 0.5


def _silu(x):
    h = 0.5 * x
    return h * jnp.tanh(h) + h


def _softplus(x):
    return jnp.maximum(x, 0.0) + jnp.log(1.0 + jnp.exp(-jnp.abs(x)))


def _dot(a, b):
    return jnp.dot(a.astype(BF16), b.astype(BF16), preferred_element_type=F32)


def _dot_nt(a, b):
    return lax.dot_general(a.astype(BF16), b.astype(BF16), (((1,), (1,)), ((), ())),
                           preferred_element_type=F32)


def _swap_groups(x, q):
    w = x.shape[1]
    lane = lax.broadcasted_iota(jnp.int32, x.shape, 1)
    even = ((lane // q) % 2) == 0
    return jnp.where(even, pltpu.roll(x, w - q, 1), pltpu.roll(x, q, 1))


def _rope(x, cos, sin, q):
    return x * cos + _swap_groups(x, q) * sin


ADA_TN = 1536


def _adaln_kernel(c_ref, w_ref, b_ref, o_ref):
    c = c_ref[...]
    o_ref[0] = jnp.dot(_silu(c), w_ref[0], preferred_element_type=F32) + b_ref[0]


def _adaln(cond8, w_ada, b_ada):
    n = 6 * D_MODEL
    return pl.pallas_call(
        _adaln_kernel,
        grid=(DEPTH, n // ADA_TN),
        in_specs=[
            pl.BlockSpec((N_GROUPS, D_MODEL), lambda l, j: (0, 0)),
            pl.BlockSpec((1, D_MODEL, ADA_TN), lambda l, j: (l, 0, j)),
            pl.BlockSpec((1, 1, ADA_TN), lambda l, j: (l, 0, j)),
        ],
        out_specs=pl.BlockSpec((1, N_GROUPS, ADA_TN), lambda l, j: (l, 0, j)),
        out_shape=jax.ShapeDtypeStruct((DEPTH, N_GROUPS, n), F32),
        compiler_params=_cparams("arbitrary", "arbitrary"),
        name="adaln",
    )(cond8, w_ada, b_ada.reshape(DEPTH, 1, n))


def _chunk_scan(g, forward):
    n = g.shape[0]
    pos = lax.broadcasted_iota(jnp.int32, g.shape, 0) % GDN_CHUNK
    s = 1
    while s < GDN_CHUNK:
        if forward:
            g = g + jnp.where(pos >= s, pltpu.roll(g, s, 0), 0.0)
        else:
            g = g + jnp.where(pos < GDN_CHUNK - s, pltpu.roll(g, n - s, 0), 0.0)
        s *= 2
    return g


def _inproj_kernel(*refs, n_x):
    x_refs, refs = refs[:n_x], refs[n_x:]
    (mod_ref, n1g_ref, w_head_ref, w_cqkv_ref, w_cz_ref,
     ca_ref, sa_ref, cs_ref, ss_ref, cq_ref, sq_ref,
     alog_ref, dtb_ref, gq_ref, gkv_ref, wuq_ref, wukk_ref, pkr_ref, wukv_ref,
     aq_ref, ak_ref, av_ref, bq_ref, bk_ref, bv_ref, ckv_ref, small_ref,
     cqkv_ref, gcol_ref, grow_ref, cz_ref) = refs
    mod = mod_ref[0]
    h = _rms(_pick_stream(x_refs)) * n1g_ref[...] * (1.0 + mod[1:2]) + mod[0:1]
    hb = h.astype(BF16)

    def proj(w_ref, lo, hi):
        return jnp.dot(hb, w_ref[:, lo:hi], preferred_element_type=F32)

    ca, sa = ca_ref[...], sa_ref[...]
    a_all = proj(w_head_ref, O_AQ, O_BCQ)
    for c in range(4):
        q = a_all[:, c * LANES:(c + 1) * LANES]
        aq_ref[:, c * LANES:(c + 1) * LANES] = (_rope(q, ca, sa, 16) * (A_SCALE * LOG2E)).astype(BF16)
    ak_ref[...] = _rope(a_all[:, O_AK:O_AV], ca, sa, 16)
    av_ref[...] = a_all[:, O_AV:O_BCQ]

    b_all = proj(w_head_ref, O_BCQ, W_HEAD_COLS)
    cq = (_rms(b_all[:, :B_Q_LORA]) * gq_ref[...]).astype(BF16)
    cqc, sqc = cq_ref[...], sq_ref[...]
    q_all = jnp.dot(cq, wuq_ref[...], preferred_element_type=F32)
    for hh in range(B_HEADS):
        qh = q_all[:, hh * LANES:(hh + 1) * LANES]
        bq_ref[:, hh * LANES:(hh + 1) * LANES] = (_rope(qh, cqc, sqc, 8) * (MLA_SCALE * LOG2E)).astype(BF16)

    ckv = _rms(b_all[:, B_Q_LORA:B_Q_LORA + B_KV_LORA]) * gkv_ref[...]
    ckv_ref[...] = ckv
    ckvb = ckv.astype(BF16)
    small = b_all[:, O_SMALL - O_BCQ:]
    small_r = _rope(small, cs_ref[...], ss_ref[...], 8)
    small_ref[...] = small_r
    krb = small_r.astype(BF16)
    bk_ref[...] = (jnp.dot(ckvb, wukk_ref[...], preferred_element_type=F32)
                   + jnp.dot(krb, pkr_ref[...], preferred_element_type=F32)).astype(BF16)
    bv_ref[...] = jnp.dot(ckvb, wukv_ref[...], preferred_element_type=F32).astype(BF16)

    for c in range(3):
        cqkv_ref[:, c * 512:(c + 1) * 512] = proj(w_cqkv_ref, c * 512, (c + 1) * 512)
    cz_ref[...] = proj(w_cz_ref, 0, 512)

    graw = -jnp.exp(alog_ref[...]) * _softplus(small + dtb_ref[...])
    lane = lax.broadcasted_iota(jnp.int32, small.shape, 1)
    gcol = jnp.where(lane < SM_A + C_HEADS, _chunk_scan(graw, True), _chunk_scan(graw, False))
    gcol = jnp.where((lane >= SM_A) & (lane < SM_B), gcol,
                     jnp.where((lane >= SM_B) & (lane < SM_B + 2 * C_HEADS), _sigmoid(small), 0.0))
    gcol_ref[...] = gcol
    grow_ref[...] = gcol.T[SM_A:SM_A + 16, :]


def _inproj(xs, mod, wts, tabs, layer):
    tile = lambda w: pl.BlockSpec((TM, w), lambda i: (i, 0))
    rope_spec = pl.BlockSpec((TM, LANES), lambda i: (_rope_block_of_tile(i), 0))
    res = lambda shape: _resident_layer(shape, layer)
    out_shapes = [
        jax.ShapeDtypeStruct((T_ALL, 512), BF16),
        jax.ShapeDtypeStruct((T_ALL, 128), F32),
        jax.ShapeDtypeStruct((T_ALL, 128), F32),
        jax.ShapeDtypeStruct((T_ALL, 1024), BF16),
        jax.ShapeDtypeStruct((T_ALL, 1024), BF16),
        jax.ShapeDtypeStruct((T_ALL, 512), BF16),
        jax.ShapeDtypeStruct((T_ALL, 256), F32),
        jax.ShapeDtypeStruct((T_ALL, 128), F32),
        jax.ShapeDtypeStruct((T_ALL, 1536), F32),
        jax.ShapeDtypeStruct((T_ALL, 128), F32),
        jax.ShapeDtypeStruct((16, T_ALL), F32),
        jax.ShapeDtypeStruct((T_ALL, 512), F32),
    ]
    out_specs = [tile(512), tile(128), tile(128), tile(1024), tile(1024), tile(512), tile(256), tile(128),
                 tile(1536), tile(128), pl.BlockSpec((16, TM), lambda i: (0, i)), tile(512)]
    in_specs = _stream_specs(D_MODEL, len(xs)) + [
        _mod_spec(layer),
        res((1, D_MODEL)),
        res((D_MODEL, W_HEAD_COLS)), res((D_MODEL, 1536)),
        pl.BlockSpec((None, D_MODEL, 512), lambda i: (layer, 0, 0), pipeline_mode=pl.Buffered(1)),
        rope_spec, rope_spec, rope_spec, rope_spec, rope_spec, rope_spec,
        res((1, LANES)), res((1, LANES)),
        res((1, B_Q_LORA)), res((1, B_KV_LORA)),
        res((B_Q_LORA, 1024)), res((B_KV_LORA, 1024)), _resident((LANES, 1024)),
        res((B_KV_LORA, 512)),
    ]
    return pl.pallas_call(
        functools.partial(_inproj_kernel, n_x=len(xs)),
        grid=(N_TILES,),
        in_specs=in_specs,
        out_specs=out_specs,
        out_shape=out_shapes,
        compiler_params=_cparams("arbitrary"),
        name="inproj",
    )(*xs, mod, wts["n1g"], wts["w_head"], wts["w_cqkv"], wts["w_tail"],
      tabs["ca"], tabs["sa"], tabs["cs"], tabs["ss"], tabs["cq"], tabs["sq"],
      wts["alog"], wts["dtb"], wts["gq"], wts["gkv"], wts["wuq"], wts["wukk"], wts["pkr"], wts["wukv"])


def _mla_cache_kernel(ckv_ref, kr_ref, wukk_ref, pkr_ref, wukv_ref, k_ref, v_ref):
    ckvb = ckv_ref[...].astype(BF16)
    krb = kr_ref[...].astype(BF16)
    k_ref[...] = (jnp.dot(ckvb, wukk_ref[...], preferred_element_type=F32)
                  + jnp.dot(krb, pkr_ref[...], preferred_element_type=F32)).astype(BF16)
    v_ref[...] = jnp.dot(ckvb, wukv_ref[...], preferred_element_type=F32).astype(BF16)


def _mla_cache(cache_ckv, cache_kr, wts, layer):
    n = DEC_BATCH * PAST_LEN
    res = lambda shape: _resident_layer(shape, layer)
    return pl.pallas_call(
        _mla_cache_kernel,
        grid=(DEC_BATCH,),
        in_specs=[pl.BlockSpec((None, None, PAST_LEN, B_KV_LORA), lambda i: (i, layer, 0, 0)),
                  pl.BlockSpec((None, None, PAST_LEN, LANES), lambda i: (i, layer, 0, 0)),
                  res((B_KV_LORA, 1024)), _resident((LANES, 1024)), res((B_KV_LORA, 512))],
        out_specs=[pl.BlockSpec((PAST_LEN, 1024), lambda i: (i, 0)),
                   pl.BlockSpec((PAST_LEN, 512), lambda i: (i, 0))],
        out_shape=[jax.ShapeDtypeStruct((n, 1024), BF16), jax.ShapeDtypeStruct((n, 512), BF16)],
        compiler_params=_cparams("arbitrary"),
        name="mla_cache",
    )(cache_ckv, cache_kr, wts["wukk"], wts["pkr"], wts["wukv"])


TQ = 256
A_WIN_KEYS = TQ + 2 * A_WINDOW
_NT = (((1,), (1,)), ((), ()))


def _softmax_pv(scores, values, extra_logit=None):
    m = scores[0].max(axis=-1, keepdims=True)
    for s in scores[1:]:
        m = jnp.maximum(m, s.max(axis=-1, keepdims=True))
    if extra_logit is not None:
        m = jnp.maximum(m, extra_logit)
    den = None
    acc = None
    for s, v in zip(scores, values):
        e = jnp.exp2(s - m)
        d = e.sum(axis=-1, keepdims=True)
        den = d if den is None else den + d
        pv = jnp.dot(e.astype(BF16), v, preferred_element_type=F32)
        acc = pv if acc is None else acc + pv
    if extra_logit is not None:
        den = den + jnp.exp2(extra_logit - m)
    return acc / den


def _attn_a_kernel(sink_ref, q_ref, k1_ref, v1_ref, *rest, latent, layer):
    if latent:
        k2_ref, v2_ref, o_ref = rest
        qt = pl.program_id(1)
        t0 = qt * TQ
        start = pl.multiple_of(jnp.clip(t0 - A_WINDOW, 0, DEC_SEQ - A_WIN_KEYS), A_WINDOW)
        k1 = k1_ref[pl.ds(start, A_WIN_KEYS), :]
        v1 = v1_ref[pl.ds(start, A_WIN_KEYS), :]
        qpos = t0 + lax.broadcasted_iota(jnp.int32, (TQ, A_WIN_KEYS), 0)
        kpos = start + lax.broadcasted_iota(jnp.int32, (TQ, A_WIN_KEYS), 1)
        valid = jnp.abs(qpos - kpos) <= A_WINDOW
    else:
        (o_ref,) = rest
        k1 = k1_ref[...]
        v1 = v1_ref[...]
    both = lambda x: (x.astype(BF16), pltpu.roll(x, A_HEAD_DIM, 1).astype(BF16))
    k1, v1 = both(k1), both(v1)
    if latent:
        k2, v2 = both(k2_ref[...]), both(v2_ref[...])
    lane = lax.broadcasted_iota(jnp.int32, (TQ, LANES), 1)
    low = lane < A_HEAD_DIM
    for c in range(A_HEADS // 2):
        qc = q_ref[:, c * LANES:(c + 1) * LANES]
        hk = (2 * c) // A_GROUP
        outs = []
        for p in range(2):
            sel = 0 if p == hk else 1
            qz = jnp.where(low if p == 0 else ~low, qc, jnp.zeros_like(qc))
            s1 = lax.dot_general(qz, k1[sel], _NT, preferred_element_type=F32)
            sink = sink_ref[layer, 2 * c + p] * LOG2E
            if latent:
                s1 = jnp.where(valid, s1, NEG_INF)
                s2 = lax.dot_general(qz, k2[sel], _NT, preferred_element_type=F32)
                outs.append(_softmax_pv([s1, s2], [v1[sel], v2[sel]], sink))
            else:
                outs.append(_softmax_pv([s1], [v1[sel]], sink))
        o_ref[:, c * LANES:(c + 1) * LANES] = jnp.where(low, outs[0], outs[1]).astype(BF16)


def _attn_a_ctx(sink, aq, ak, av, layer):
    nb = T_CTX // TQ
    return pl.pallas_call(
        functools.partial(_attn_a_kernel, latent=False, layer=layer),
        grid=(nb,),
        in_specs=[pl.BlockSpec(memory_space=pltpu.SMEM),
                  pl.BlockSpec((TQ, 512), lambda b: (b, 0)),
                  pl.BlockSpec((SEQ, LANES), lambda b: (b, 0)),
                  pl.BlockSpec((SEQ, LANES), lambda b: (b, 0))],
        out_specs=pl.BlockSpec((TQ, 512), lambda b: (b, 0)),
        out_shape=jax.ShapeDtypeStruct((T_CTX, 512), BF16),
        compiler_params=_cparams("arbitrary"),
        name="attn_a_ctx",
    )(sink, aq, ak, av)


def _attn_a_lat(sink, aq, ak, av, ck, cv, layer):
    nq = DEC_SEQ // TQ
    ctx_tiles = T_CTX // TQ
    seq_blocks = T_CTX // DEC_SEQ
    cache_spec = pl.BlockSpec((None, None, PAST_LEN, LANES), lambda b, i: (b, layer, 0, 0))
    return pl.pallas_call(
        functools.partial(_attn_a_kernel, latent=True, layer=layer),
        grid=(DEC_BATCH, nq),
        in_specs=[pl.BlockSpec(memory_space=pltpu.SMEM),
                  pl.BlockSpec((TQ, 512), lambda b, i: (ctx_tiles + b * nq + i, 0)),
                  pl.BlockSpec((DEC_SEQ, LANES), lambda b, i: (seq_blocks + b, 0)),
                  pl.BlockSpec((DEC_SEQ, LANES), lambda b, i: (seq_blocks + b, 0)),
                  cache_spec, cache_spec],
        out_specs=pl.BlockSpec((TQ, 512), lambda b, i: (b * nq + i, 0)),
        out_shape=jax.ShapeDtypeStruct((T_LAT, 512), BF16),
        compiler_params=_cparams("arbitrary", "arbitrary"),
        name="attn_a_lat",
    )(sink, aq, ak, av, ck, cv)


def _mla_kernel(q_ref, k1_ref, v1_ref, *rest, latent):
    if latent:
        k2_ref, v2_ref, o_ref = rest
    else:
        (o_ref,) = rest
    lane = lax.broadcasted_iota(jnp.int32, (TQ, LANES), 1)
    low = lane < B_V
    for j in range(B_HEADS // 2):
        outs = []
        for sub in range(2):
            hh = 2 * j + sub
            hs = slice(hh * LANES, (hh + 1) * LANES)
            vs = slice(j * LANES, (j + 1) * LANES)
            qh = q_ref[:, hs]
            s1 = lax.dot_general(qh, k1_ref[:, hs], _NT, preferred_element_type=F32)
            if latent:
                s2 = lax.dot_general(qh, k2_ref[:, hs], _NT, preferred_element_type=F32)
                outs.append(_softmax_pv([s1, s2], [v1_ref[:, vs], v2_ref[:, vs]]))
            else:
                outs.append(_softmax_pv([s1], [v1_ref[:, vs]]))
        o_ref[:, j * LANES:(j + 1) * LANES] = jnp.where(low, outs[0], outs[1]).astype(BF16)


def _mla_ctx(bq, bk, bv):
    nb = T_CTX // TQ
    return pl.pallas_call(
        functools.partial(_mla_kernel, latent=False),
        grid=(nb,),
        in_specs=[pl.BlockSpec((TQ, 1024), lambda b: (b, 0)),
                  pl.BlockSpec((SEQ, 1024), lambda b: (b, 0)),
                  pl.BlockSpec((SEQ, 512), lambda b: (b, 0))],
        out_specs=pl.BlockSpec((TQ, 512), lambda b: (b, 0)),
        out_shape=jax.ShapeDtypeStruct((T_CTX, 512), BF16),
        compiler_params=_cparams("arbitrary"),
        name="mla_ctx",
    )(bq, bk, bv)


def _mla_lat(bq, bk, bv, k2, v2):
    nq = DEC_SEQ // TQ
    ctx_tiles = T_CTX // TQ
    seq_blocks = T_CTX // DEC_SEQ
    return pl.pallas_call(
        functools.partial(_mla_kernel, latent=True),
        grid=(DEC_BATCH, nq),
        in_specs=[pl.BlockSpec((TQ, 1024), lambda b, i: (ctx_tiles + b * nq + i, 0)),
                  pl.BlockSpec((DEC_SEQ, 1024), lambda b, i: (seq_blocks + b, 0)),
                  pl.BlockSpec((DEC_SEQ, 512), lambda b, i: (seq_blocks + b, 0)),
                  pl.BlockSpec((PAST_LEN, 1024), lambda b, i: (b, 0)),
                  pl.BlockSpec((PAST_LEN, 512), lambda b, i: (b, 0))],
        out_specs=pl.BlockSpec((TQ, 512), lambda b, i: (b * nq + i, 0)),
        out_shape=jax.ShapeDtypeStruct((T_LAT, 512), BF16),
        compiler_params=_cparams("arbitrary", "arbitrary"),
        name="mla_lat",
    )(bq, bk, bv, k2, v2)


def _gdn_kernel(xq_ref, xk_ref, xv_ref, wq_ref, wk_ref, wv_ref, gcol_ref, grow_ref, cz_ref, ong_ref, *rest,
                n, hps, has_s0, emit_state):
    rest = list(rest)
    s0_ref = rest.pop(0) if has_s0 else None
    oc_ref = rest.pop(0)
    sfin_ref = rest.pop(0) if emit_state else None
    q_s, k_s, v_s, u_s, w_s, qk_s, qd_s, kdt_s, ge_s, st_s, o_s = rest
    nc = n // GDN_CHUNK
    head0 = pl.program_id(1) * hps
    C = GDN_CHUNK
    width = hps * LANES
    hl = lambda hh: slice(hh * LANES, (hh + 1) * LANES)
    chains = [(d, hh) for d in range(2) for hh in range(hps)]

    row8 = lax.broadcasted_iota(jnp.int32, (8, width), 0)

    def conv_silu(x_ref, w_ref):
        x = x_ref[...]
        w = w_ref[...]
        y = pltpu.roll(x, 1, 0) * w[0:1] + x * w[1:2] + pltpu.roll(x, n - 1, 0) * w[2:3]
        top = y[0:8] - jnp.where(row8 == 0, x[n - 1:n] * w[0:1], 0.0)
        bot = y[n - 8:n] - jnp.where(row8 == 7, x[0:1] * w[2:3], 0.0)
        return _silu(jnp.concatenate([top, y[8:n - 8], bot], axis=0))

    def l2n(x):
        return x * lax.rsqrt(jnp.sum(x * x, axis=-1, keepdims=True) + EPS)

    qc = conv_silu(xq_ref, wq_ref)
    kc = conv_silu(xk_ref, wk_ref)
    for hh in range(hps):
        q_s[:, hl(hh)] = l2n(qc[:, hl(hh)]) * (C_DK ** -0.5)
        k_s[:, hl(hh)] = l2n(kc[:, hl(hh)])
    v_s[...] = conv_silu(xv_ref, wv_ref)
    o_s[...] = jnp.zeros_like(o_s)

    lane_c = lax.broadcasted_iota(jnp.int32, (C, LANES), 1)
    ri = lax.broadcasted_iota(jnp.int32, (C, C), 0)
    ci = lax.broadcasted_iota(jnp.int32, (C, C), 1)
    eye = (ri == ci).astype(F32)
    same_block = {}
    s = 2
    while s <= C:
        same_block[s] = (ri // s) == (ci // s)
        s *= 2

    def prep(it, carry):
        items = []
        for gi in range(group):
            r = it * group + gi
            rows = pl.ds(pl.multiple_of(r * C, C), C)
            gtile = gcol_ref[rows, :]
            for hh in range(hps):
                q = q_s[rows, hl(hh)]
                k = k_s[rows, hl(hh)]
                v = v_s[rows, hl(hh)]
                qk_raw = _dot_nt(q, k)
                for d in range(2):
                    slot = d * hps + hh
                    j = d * C_HEADS + head0 + hh
                    gc = jnp.sum(jnp.where(lane_c == SM_A + j, gtile, 0.0), axis=1, keepdims=True)
                    beta = jnp.sum(jnp.where(lane_c == SM_B + j, gtile, 0.0), axis=1, keepdims=True)
                    grow = grow_ref[j, r]
                    incl = (ri >= ci) if d == 0 else (ri <= ci)
                    strict = (ri > ci) if d == 0 else (ri < ci)
                    decay = jnp.where(incl, jnp.exp(jnp.where(incl, gc - grow, 0.0)), 0.0)
                    kb = k * beta
                    a = jnp.where(strict, _dot_nt(kb, k) * decay, 0.0)
                    g_end = gc[C - 1:C, :] if d == 0 else gc[0:1, :]
                    egc = jnp.exp(gc)
                    qk_s[slot, rows, :] = (qk_raw * decay).astype(BF16)
                    qd_s[slot, rows, :] = (q * egc).astype(BF16)
                    kdt_s[slot, rows, :] = (k * jnp.exp(g_end - gc)).T.astype(BF16)
                    ge_s[slot, r] = jnp.broadcast_to(jnp.exp(g_end), (8, LANES))
                    items.append((slot, rows, a, jnp.concatenate([v * beta, kb * egc], axis=1), d))
        ts = [eye - jnp.where(same_block[2], a, 0.0) for (_, _, a, _, _) in items]
        s = 2
        while s < C:
            level = same_block[2 * s] & ~same_block[s]
            es = [jnp.where(level, a, 0.0) for (_, _, a, _, _) in items]
            if s < 8:
                xs = [_dot(t, e) for t, e in zip(ts, es)]
                ts = [t - _dot(x, t) for t, x in zip(ts, xs)]
            else:
                def active(t, d):
                    return jnp.concatenate([t[b * s:(b + 1) * s] for b in range(1 - d, C // s, 2)], axis=0)

                def merged(t, upd, d):
                    return jnp.concatenate(
                        [upd[(b // 2) * s:(b // 2 + 1) * s] if b % 2 == 1 - d else t[b * s:(b + 1) * s]
                         for b in range(C // s)], axis=0)

                halves = [active(t, it_[4]) for t, it_ in zip(ts, items)]
                xs = [_dot(h, e) for h, e in zip(halves, es)]
                upds = [h - _dot(x, t) for h, x, t in zip(halves, xs, ts)]
                ts = [merged(t, u, it_[4]) for t, u, it_ in zip(ts, upds, items)]
            s *= 2
        for t, (slot, rows, _, rhs, _) in zip(ts, items):
            uw = _dot(t, rhs)
            u_s[slot, rows, :] = uw[:, :C_DV]
            w_s[slot, rows, :] = uw[:, C_DV:].astype(BF16)
        return carry

    group = max(1, min(nc, GDN_PREP_CHAINS // (2 * hps)))
    lax.fori_loop(0, nc // group, prep, 0)

    for d, hh in chains:
        if has_s0:
            st_s[d * hps + hh] = s0_ref[d, hh]
        else:
            st_s[d * hps + hh] = jnp.zeros((C_DK, C_DV), F32)

    def step(i, carry):
        rows = [pl.ds(pl.multiple_of(i * C, C), C), pl.ds(pl.multiple_of((nc - 1 - i) * C, C), C)]
        chunk = [i, nc - 1 - i]
        slots = [d * hps + hh for d, hh in chains]
        st = [st_s[c] for c in slots]
        stb = [s_.astype(BF16) for s_ in st]
        ws = [jnp.dot(w_s[c, rows[d], :], sb, preferred_element_type=F32)
              for c, (d, _), sb in zip(slots, chains, stb)]
        vnb = [(u_s[c, rows[d], :] - x).astype(BF16) for c, (d, _), x in zip(slots, chains, ws)]
        for c, (d, hh), sb, vb, s_ in zip(slots, chains, stb, vnb, st):
            o = (jnp.dot(qd_s[c, rows[d], :], sb, preferred_element_type=F32)
                 + jnp.dot(qk_s[c, rows[d], :], vb, preferred_element_type=F32))
            o_s[rows[d], hl(hh)] += o
            st_s[c] = s_ * ge_s[c, chunk[d]][0:1, :] + jnp.dot(kdt_s[c, rows[d], :], vb,
                                                                preferred_element_type=F32)
        return carry

    lax.fori_loop(0, nc, step, 0)
    if emit_state:
        for d, hh in chains:
            sfin_ref[0, d, hh] = st_s[d * hps + hh]

    for hh in range(hps):
        o = o_s[:, hl(hh)]
        oc_ref[:, hl(hh)] = (_rms(o) * ong_ref[...] * _silu(cz_ref[:, hl(hh)])).astype(BF16)


def _gdn(cqkv, conv_w, gcol, grow3, cz, ong, state_c, layer, *, latent):
    n = DEC_SEQ if latent else SEQ
    nseq = DEC_BATCH if latent else BATCH
    hps = GDN_HEADS_PER_STEP_LAT if latent else GDN_HEADS_PER_STEP_CTX
    seq0 = (T_CTX // n) if latent else 0
    nc = n // GDN_CHUNK
    width = hps * LANES
    sect = C_HEADS // hps
    col = lambda base: pl.BlockSpec((n, width), lambda b, h: (seq0 + b, base + h))
    wcol = lambda base: pl.BlockSpec((None, 3, width), lambda b, h: (layer, 0, base + h))
    in_specs = [col(0), col(sect), col(2 * sect), wcol(0), wcol(sect), wcol(2 * sect),
                pl.BlockSpec((n, LANES), lambda b, h: (seq0 + b, 0)),
                pl.BlockSpec((16, nc, 1, LANES), lambda b, h: (0, seq0 + b, 0, 0)),
                pl.BlockSpec((n, width), lambda b, h: (seq0 + b, h)),
                pl.BlockSpec((None, 1, LANES), lambda b, h: (layer, 0, 0))]
    args = [cqkv, cqkv, cqkv, conv_w, conv_w, conv_w, gcol, grow3, cz, ong]
    out_specs = [pl.BlockSpec((n, width), lambda b, h: (b, h))]
    out_shape = [jax.ShapeDtypeStruct((nseq * n, 512), BF16)]
    if latent:
        in_specs.append(pl.BlockSpec((None, None, 2, hps, C_DK, C_DV), lambda b, h: (b, layer, 0, h, 0, 0)))
        args.append(state_c)
    else:
        out_specs.append(pl.BlockSpec((1, 2, hps, C_DK, C_DV), lambda b, h: (b, 0, h, 0, 0)))
        out_shape.append(jax.ShapeDtypeStruct((nseq, 2, C_HEADS, C_DK, C_DV), F32))
    nch = 2 * hps
    scratch = [pltpu.VMEM((n, width), F32), pltpu.VMEM((n, width), F32), pltpu.VMEM((n, width), F32),
               pltpu.VMEM((nch, n, LANES), F32), pltpu.VMEM((nch, n, LANES), BF16),
               pltpu.VMEM((nch, n, LANES), BF16), pltpu.VMEM((nch, n, LANES), BF16),
               pltpu.VMEM((nch, n, LANES), BF16), pltpu.VMEM((nch, nc, 8, LANES), F32),
               pltpu.VMEM((nch, C_DK, C_DV), F32), pltpu.VMEM((n, width), F32)]
    res = pl.pallas_call(
        functools.partial(_gdn_kernel, n=n, hps=hps, has_s0=latent, emit_state=not latent),
        grid=(nseq, C_HEADS // hps),
        in_specs=in_specs,
        out_specs=out_specs,
        out_shape=out_shape,
        scratch_shapes=scratch,
        compiler_params=_cparams("arbitrary", "arbitrary"),
        name="gdn_lat" if latent else "gdn_ctx",
    )(*args)
    return res


def _merge_kernel(*refs, n_x):
    x_refs, refs = refs[:n_x], refs[n_x:]
    branch_refs, refs = refs[:6], refs[6:]
    mod_ref, n1g_ref, wtail_ref, wbr_ref, wout_ref, o_ref = refs
    mod = mod_ref[0]
    x = _pick_stream(x_refs)
    hb = (_rms(x) * n1g_ref[...] * (1.0 + mod[1:2]) + mod[0:1]).astype(BF16)
    z = None
    for j in range(3):
        cols = slice(512 + j * D_MODEL, 512 + (j + 1) * D_MODEL)
        gate = _sigmoid(jnp.dot(hb, wtail_ref[:, cols], preferred_element_type=F32))
        y = jnp.dot(_pick_stream(branch_refs[2 * j:2 * j + 2]), wbr_ref[j], preferred_element_type=F32)
        z = gate * y if z is None else z + gate * y
    o_ref[...] = x + mod[2:3] * jnp.dot(z.astype(BF16), wout_ref[...], preferred_element_type=F32)


def _merge(xs, branches, mod, wts, layer):
    res = lambda shape: _resident_layer(shape, layer)
    in_specs = _stream_specs(D_MODEL, len(xs))
    for _ in range(3):
        in_specs += _stream_specs(BRANCH_W, 2)
    in_specs += [_mod_spec(layer), res((1, D_MODEL)), res((D_MODEL, W_TAIL_COLS)),
                 res((3, BRANCH_W, D_MODEL)), res((D_MODEL, D_MODEL))]
    return pl.pallas_call(
        functools.partial(_merge_kernel, n_x=len(xs)),
        grid=(N_TILES,),
        in_specs=in_specs,
        out_specs=pl.BlockSpec((TM, D_MODEL), lambda i: (i, 0)),
        out_shape=jax.ShapeDtypeStruct((T_ALL, D_MODEL), F32),
        compiler_params=_cparams("arbitrary"),
        name="merge",
    )(*xs, *branches, mod, wts["n1g"], wts["w_tail"], wts["wbr"], wts["wout"])


FF_CHUNK = 1024


def _ffn_kernel(x_ref, mod_ref, n2g_ref, wff1_ref, wff2_ref, fg_ref, o_ref, *, final):
    mod = mod_ref[0]
    x = x_ref[...]
    hb = (_rms(x) * n2g_ref[...] * (1.0 + mod[4:5]) + mod[3:4]).astype(BF16)
    acc = None
    for c in range(D_FF // FF_CHUNK):
        cols = slice(c * FF_CHUNK, (c + 1) * FF_CHUNK)
        f = jnp.maximum(jnp.dot(hb, wff1_ref[:, cols], preferred_element_type=F32), 0.0)
        p = jnp.dot((f * f).astype(BF16), wff2_ref[cols, :], preferred_element_type=F32)
        acc = p if acc is None else acc + p
    x = x + mod[5:6] * acc
    if final:
        x = _rms(x) * fg_ref[...]
    o_ref[...] = x


def _ffn(x_all, mod, wts, final_g, layer, *, tile0, n_tiles, final):
    res = lambda shape: _resident_layer(shape, layer)
    return pl.pallas_call(
        functools.partial(_ffn_kernel, final=final),
        grid=(n_tiles,),
        in_specs=[pl.BlockSpec((TM, D_MODEL), lambda i: (tile0 + i, 0)), _mod_spec(layer, tile0),
                  res((1, D_MODEL)), res((D_MODEL, D_FF)), res((D_FF, D_MODEL)), _resident((1, D_MODEL))],
        out_specs=pl.BlockSpec((TM, D_MODEL), lambda i: (i, 0)),
        out_shape=jax.ShapeDtypeStruct((n_tiles * TM, D_MODEL), F32),
        compiler_params=_cparams("arbitrary"),
        name="ffn",
    )(x_all, mod, wts["n2g"], wts["wff1"], wts["wff2"], final_g)


def _rope_tables(r):
    half, quarter = r // 2, r // 4
    t = jnp.arange(DEC_SEQ)
    inv_freq = jnp.power(ROPE_BASE, -jnp.arange(quarter, dtype=F32) / quarter)
    cs, ss = [], []
    for pos in (t // GRID_W, t % GRID_W):
        ang = pos.astype(F32)[:, None] * inv_freq[None, :]
        c, s = jnp.cos(ang), jnp.sin(ang)
        cs += [c, c]
        ss += [-s, s]
    return jnp.concatenate(cs, axis=1), jnp.concatenate(ss, axis=1)


def _tables():
    def with_identity(c, s):
        one = jnp.ones((TM, LANES), F32)
        return jnp.concatenate([one, c], axis=0), jnp.concatenate([0.0 * one, s], axis=0)

    c64, s64 = _rope_tables(A_HEAD_DIM)
    ca, sa = with_identity(jnp.tile(c64, (1, 2)), jnp.tile(s64, (1, 2)))
    c32, s32 = _rope_tables(B_ROPE)
    ones = lambda w: jnp.ones((DEC_SEQ, w), F32)
    zeros = lambda w: jnp.zeros((DEC_SEQ, w), F32)
    cs, ss = with_identity(jnp.concatenate([c32, ones(96)], axis=1), jnp.concatenate([s32, zeros(96)], axis=1))
    cq, sq = with_identity(jnp.concatenate([ones(64), c32, ones(32)], axis=1),
                           jnp.concatenate([zeros(64), s32, zeros(32)], axis=1))
    return {"ca": ca, "sa": sa, "cs": cs, "ss": ss, "cq": cq, "sq": sq}


def _prep_weights(norm1_g, norm2_g, w_in, b_gq, b_w_uq, b_gkv, b_w_ukv, c_a_log, c_dt_bias, c_onorm_g,
                  w_branch, w_out, w_ff1, w_ff2):
    offs = [0]
    for width in W_IN_SPLITS:
        offs.append(offs[-1] + width)
    piece = lambda i, j: w_in[:, :, offs[i]:offs[j]]
    w_head = jnp.concatenate([piece(0, 6), piece(7, 9), jnp.zeros((DEPTH, D_MODEL, LANES - 48), F32)],
                             axis=2).astype(BF16)
    w_cqkv = piece(6, 7).astype(BF16)
    w_tail = piece(9, 11).astype(BF16)

    pad_lane = lambda v: jnp.zeros((DEPTH, 1, LANES), F32).at[:, 0, SM_A:SM_A + 8].set(v.reshape(DEPTH, 8))
    uq = b_w_uq.reshape(DEPTH, B_Q_LORA, B_HEADS, B_NOPE + B_ROPE)
    uq = jnp.pad(uq, ((0, 0), (0, 0), (0, 0), (0, LANES - B_NOPE - B_ROPE)))
    uq = uq.reshape(DEPTH, B_Q_LORA, 1024).astype(BF16)
    ukv = b_w_ukv.reshape(DEPTH, B_KV_LORA, B_HEADS, B_NOPE + B_V)
    ukk = jnp.pad(ukv[..., :B_NOPE], ((0, 0), (0, 0), (0, 0), (0, LANES - B_NOPE)))
    ukk = ukk.reshape(DEPTH, B_KV_LORA, 1024).astype(BF16)
    ukvv = ukv[..., B_NOPE:].reshape(DEPTH, B_KV_LORA, 512).astype(BF16)
    src = jnp.arange(LANES)[:, None]
    dst = jnp.arange(1024)[None, :]
    pkr = ((dst % LANES == B_NOPE + src) & (src < B_ROPE)).astype(BF16)

    return {"w_head": w_head, "w_cqkv": w_cqkv, "w_tail": w_tail,
            "alog": pad_lane(c_a_log), "dtb": pad_lane(c_dt_bias),
            "n1g": norm1_g.reshape(DEPTH, 1, D_MODEL), "n2g": norm2_g.reshape(DEPTH, 1, D_MODEL),
            "gq": b_gq.reshape(DEPTH, 1, B_Q_LORA), "gkv": b_gkv.reshape(DEPTH, 1, B_KV_LORA),
            "ong": c_onorm_g.reshape(DEPTH, 1, C_DV),
            "wuq": uq, "wukk": ukk, "pkr": pkr, "wukv": ukvv,
            "wbr": w_branch.astype(BF16), "wout": w_out.astype(BF16), "wff1": w_ff1.astype(BF16), "wff2": w_ff2.astype(BF16)}


def kernel(x_prompt, x_sample, cache_a_k, cache_a_v, cache_b_ckv, cache_b_krope, state_c, c, c_ctx, norm1_g,
           norm2_g, w_ada, b_ada, w_in, a_sink, b_gq, b_w_uq, b_gkv, b_w_ukv, c_conv_w, c_a_log, c_dt_bias,
           c_onorm_g, w_branch, w_out, w_ff1, w_ff2, final_g):
    xs = (x_prompt.reshape(T_CTX, D_MODEL), x_sample.reshape(T_LAT, D_MODEL))
    cond8 = jnp.concatenate([c_ctx[None, :], c, jnp.zeros((N_GROUPS - 1 - DEC_BATCH, D_MODEL), F32)], axis=0)
    mod = _adaln(cond8, w_ada, b_ada).reshape(DEPTH, N_GROUPS, 6, D_MODEL)
    tabs = _tables()
    fg = final_g.reshape(1, D_MODEL)
    wts = _prep_weights(norm1_g, norm2_g, w_in, b_gq, b_w_uq, b_gkv, b_w_ukv, c_a_log, c_dt_bias, c_onorm_g,
                        w_branch, w_out, w_ff1, w_ff2)
    ck = cache_a_k.reshape(DEC_BATCH, DEPTH, PAST_LEN, LANES)
    cv = cache_a_v.reshape(DEC_BATCH, DEPTH, PAST_LEN, LANES)
    cache_kr = jnp.pad(cache_b_krope, ((0, 0), (0, 0), (0, 0), (0, LANES - B_ROPE)))

    ak_l, av_l, ckv_l, kr_l, sc_l = [], [], [], [], []
    for l in range(DEPTH):
        (aq, ak, av, bq, bk, bv, ckv, small, cqkv, gcol, grow, cz) = _inproj(xs, mod, wts, tabs, l)

        oa_ctx = _attn_a_ctx(a_sink, aq, ak, av, l)
        oa_lat = _attn_a_lat(a_sink, aq, ak, av, ck, cv, l)

        k2, v2 = _mla_cache(cache_b_ckv, cache_kr, wts, l)
        ob_ctx = _mla_ctx(bq, bk, bv)
        ob_lat = _mla_lat(bq, bk, bv, k2, v2)

        grow3 = grow.reshape(16, T_ALL // GDN_CHUNK, 1, GDN_CHUNK)
        oc_ctx, s_fin = _gdn(cqkv, c_conv_w, gcol, grow3, cz, wts["ong"], state_c, l, latent=False)
        (oc_lat,) = _gdn(cqkv, c_conv_w, gcol, grow3, cz, wts["ong"], state_c, l, latent=True)

        branches = (oa_ctx, oa_lat, ob_ctx, ob_lat, oc_ctx, oc_lat)
        x_mid = _merge(xs, branches, mod, wts, l)
        ffn = functools.partial(_ffn, x_mid, mod, wts, fg, l)
        if l < DEPTH - 1:
            xs = (ffn(tile0=0, n_tiles=N_TILES, final=False),)
        else:
            xs = (ffn(tile0=0, n_tiles=N_CTX_TILES, final=True),
                  ffn(tile0=N_CTX_TILES, n_tiles=N_TILES - N_CTX_TILES, final=True))

        ak_l.append(ak[:T_CTX].reshape(BATCH, SEQ, A_KV_HEADS, A_HEAD_DIM))
        av_l.append(av[:T_CTX].reshape(BATCH, SEQ, A_KV_HEADS, A_HEAD_DIM))
        ckv_l.append(ckv[:T_CTX].reshape(BATCH, SEQ, B_KV_LORA))
        kr_l.append(small[:T_CTX, :B_ROPE].reshape(BATCH, SEQ, B_ROPE))
        sc_l.append(s_fin)

    y_prompt = xs[0].reshape(BATCH, SEQ, D_MODEL)
    y_sample = xs[1].reshape(DEC_BATCH, DEC_SEQ, D_MODEL)
    return (y_prompt, y_sample, jnp.stack(ak_l, axis=1), jnp.stack(av_l, axis=1), jnp.stack(ckv_l, axis=1),
            jnp.stack(kr_l, axis=1), jnp.stack(sc_l, axis=1))
```

```python
import functools
import math

import jax
import jax.numpy as jnp
from jax import lax
from jax.experimental import pallas as pl
from jax.experimental.pallas import tpu as pltpu

F32 = jnp.float32
BF16 = jnp.bfloat16

D_MODEL = 1024
BATCH = 16
SEQ = 256
DEPTH = 2
DEC_BATCH = 4
DEC_SEQ = 2048
PAST_LEN = 256
GRID_W = 64
ROPE_BASE = 10000.0
EPS = 1e-6
NEG_INF = -1e30

A_HEADS = 8
A_KV_HEADS = 2
A_GROUP = 4
A_HEAD_DIM = 64
A_WINDOW = 128
A_SCALE = A_HEAD_DIM ** -0.5
B_HEADS = 8
B_Q_LORA = 384
B_KV_LORA = 256
B_NOPE = 64
B_ROPE = 32
B_V = 64
MLA_SCALE = (B_NOPE + B_ROPE) ** -0.5
LOG2E = math.log2(math.e)
C_HEADS = 4
C_DK = 128
C_DV = 128
D_FF = 4 * D_MODEL
BRANCH_W = 512

LANES = 128

T_CTX = BATCH * SEQ
T_LAT = DEC_BATCH * DEC_SEQ
T_ALL = T_CTX + T_LAT
TM = 512
N_CTX_TILES = T_CTX // TM
LAT_TILES_PER_SEQ = DEC_SEQ // TM
N_TILES = T_ALL // TM
N_GROUPS = 8
GDN_CHUNK = 128
GDN_PREP_CHAINS = 16
GDN_HEADS_PER_STEP_CTX = 4
GDN_HEADS_PER_STEP_LAT = 2

W_IN_SPLITS = (512, 128, 128, B_Q_LORA, B_KV_LORA, B_ROPE, 1536, 8, 8, 512, 3072)
O_AQ, O_AK, O_AV, O_BCQ, O_BCKV, O_SMALL, W_HEAD_COLS = 0, 512, 640, 768, 1152, 1408, 1536
W_TAIL_COLS = 512 + 3 * D_MODEL
SM_KR, SM_A, SM_B = 0, 32, 40

VMEM_LIMIT = 56 * 1024 * 1024


def _cparams(*sem):
    return pltpu.CompilerParams(dimension_semantics=sem, vmem_limit_bytes=VMEM_LIMIT)


def _resident(shape):
    nd = len(shape)
    return pl.BlockSpec(shape, lambda *_: (0,) * nd, pipeline_mode=pl.Buffered(1))


def _resident_layer(shape, layer):
    nd = len(shape)
    return pl.BlockSpec((None,) + tuple(shape), lambda *_: (layer,) + (0,) * nd, pipeline_mode=pl.Buffered(1))


def _group_of_tile(i):
    return jnp.where(i < N_CTX_TILES, 0, 1 + (i - N_CTX_TILES) // LAT_TILES_PER_SEQ)


def _rope_block_of_tile(i):
    return jnp.where(i < N_CTX_TILES, 0, 1 + (i - N_CTX_TILES) % LAT_TILES_PER_SEQ)


def _mod_spec(layer, tile0=0):
    return pl.BlockSpec((None, 1, 6, D_MODEL), lambda i: (layer, _group_of_tile(tile0 + i), 0, 0))


def _stream_specs(width, n_arrays, tile0=0):
    if n_arrays == 1:
        return [pl.BlockSpec((TM, width), lambda i: (tile0 + i, 0))]
    return [pl.BlockSpec((TM, width), lambda i: (jnp.minimum(tile0 + i, N_CTX_TILES - 1), 0)),
            pl.BlockSpec((TM, width), lambda i: (jnp.maximum(tile0 + i - N_CTX_TILES, 0), 0))]


def _pick_stream(refs, tile0=0):
    if len(refs) == 1:
        return refs[0][...]
    return jnp.where(tile0 + pl.program_id(0) < N_CTX_TILES, refs[0][...], refs[1][...])


def _rms(x):
    return x * lax.rsqrt(jnp.mean(x * x, axis=-1, keepdims=True) + EPS)


def _sigmoid(x):
    return 0.5 * jnp.tanh(0.5 * x) + 0.5


def _silu(x):
    h = 0.5 * x
    return h * jnp.tanh(h) + h


def _softplus(x):
    return jnp.maximum(x, 0.0) + jnp.log(1.0 + jnp.exp(-jnp.abs(x)))


def _dot(a, b):
    return jnp.dot(a.astype(BF16), b.astype(BF16), preferred_element_type=F32)


def _dot_nt(a, b):
    return lax.dot_general(a.astype(BF16), b.astype(BF16), (((1,), (1,)), ((), ())),
                           preferred_element_type=F32)


def _swap_groups(x, q):
    w = x.shape[1]
    lane = lax.broadcasted_iota(jnp.int32, x.shape, 1)
    even = ((lane // q) % 2) == 0
    return jnp.where(even, pltpu.roll(x, w - q, 1), pltpu.roll(x, q, 1))


def _rope(x, cos, sin, q):
    return x * cos + _swap_groups(x, q) * sin


ADA_TN = 1536


def _adaln_kernel(c_ref, w_ref, b_ref, o_ref):
    c = c_ref[...]
    o_ref[0] = jnp.dot(_silu(c), w_ref[0], preferred_element_type=F32) + b_ref[0]


def _adaln(cond8, w_ada, b_ada):
    n = 6 * D_MODEL
    return pl.pallas_call(
        _adaln_kernel,
        grid=(DEPTH, n // ADA_TN),
        in_specs=[
            pl.BlockSpec((N_GROUPS, D_MODEL), lambda l, j: (0, 0)),
            pl.BlockSpec((1, D_MODEL, ADA_TN), lambda l, j: (l, 0, j)),
            pl.BlockSpec((1, 1, ADA_TN), lambda l, j: (l, 0, j)),
        ],
        out_specs=pl.BlockSpec((1, N_GROUPS, ADA_TN), lambda l, j: (l, 0, j)),
        out_shape=jax.ShapeDtypeStruct((DEPTH, N_GROUPS, n), F32),
        compiler_params=_cparams("arbitrary", "arbitrary"),
        name="adaln",
    )(cond8, w_ada, b_ada.reshape(DEPTH, 1, n))


def _chunk_scan(g, forward):
    n = g.shape[0]
    pos = lax.broadcasted_iota(jnp.int32, g.shape, 0) % GDN_CHUNK
    s = 1
    while s < GDN_CHUNK:
        if forward:
            g = g + jnp.where(pos >= s, pltpu.roll(g, s, 0), 0.0)
        else:
            g = g + jnp.where(pos < GDN_CHUNK - s, pltpu.roll(g, n - s, 0), 0.0)
        s *= 2
    return g


def _inproj_kernel(*refs, n_x):
    x_refs, refs = refs[:n_x], refs[n_x:]
    (mod_ref, n1g_ref, w_head_ref, w_cqkv_ref, w_cz_ref,
     ca_ref, sa_ref, cs_ref, ss_ref, cq_ref, sq_ref,
     alog_ref, dtb_ref, gq_ref, gkv_ref, wuq_ref, wukk_ref, pkr_ref, wukv_ref,
     aq_ref, ak_ref, av_ref, bq_ref, bk_ref, bv_ref, ckv_ref, small_ref,
     cqkv_ref, gcol_ref, grow_ref, cz_ref) = refs
    mod = mod_ref[0]
    h = _rms(_pick_stream(x_refs)) * n1g_ref[...] * (1.0 + mod[1:2]) + mod[0:1]
    hb = h.astype(BF16)

    def proj(w_ref, lo, hi):
        return jnp.dot(hb, w_ref[:, lo:hi], preferred_element_type=F32)

    ca, sa = ca_ref[...], sa_ref[...]
    a_all = proj(w_head_ref, O_AQ, O_BCQ)
    for c in range(4):
        q = a_all[:, c * LANES:(c + 1) * LANES]
        aq_ref[:, c * LANES:(c + 1) * LANES] = (_rope(q, ca, sa, 16) * (A_SCALE * LOG2E)).astype(BF16)
    ak_ref[...] = _rope(a_all[:, O_AK:O_AV], ca, sa, 16)
    av_ref[...] = a_all[:, O_AV:O_BCQ]

    b_all = proj(w_head_ref, O_BCQ, W_HEAD_COLS)
    cq = (_rms(b_all[:, :B_Q_LORA]) * gq_ref[...]).astype(BF16)
    cqc, sqc = cq_ref[...], sq_ref[...]
    q_all = jnp.dot(cq, wuq_ref[...], preferred_element_type=F32)
    for hh in range(B_HEADS):
        qh = q_all[:, hh * LANES:(hh + 1) * LANES]
        bq_ref[:, hh * LANES:(hh + 1) * LANES] = (_rope(qh, cqc, sqc, 8) * (MLA_SCALE * LOG2E)).astype(BF16)

    ckv = _rms(b_all[:, B_Q_LORA:B_Q_LORA + B_KV_LORA]) * gkv_ref[...]
    ckv_ref[...] = ckv
    ckvb = ckv.astype(BF16)
    small = b_all[:, O_SMALL - O_BCQ:]
    small_r = _rope(small, cs_ref[...], ss_ref[...], 8)
    small_ref[...] = small_r
    krb = small_r.astype(BF16)
    bk_ref[...] = (jnp.dot(ckvb, wukk_ref[...], preferred_element_type=F32)
                   + jnp.dot(krb, pkr_ref[...], preferred_element_type=F32)).astype(BF16)
    bv_ref[...] = jnp.dot(ckvb, wukv_ref[...], preferred_element_type=F32).astype(BF16)

    for c in range(3):
        cqkv_ref[:, c * 512:(c + 1) * 512] = proj(w_cqkv_ref, c * 512, (c + 1) * 512)
    cz_ref[...] = proj(w_cz_ref, 0, 512)

    graw = -jnp.exp(alog_ref[...]) * _softplus(small + dtb_ref[...])
    lane = lax.broadcasted_iota(jnp.int32, small.shape, 1)
    gcol = jnp.where(lane < SM_A + C_HEADS, _chunk_scan(graw, True), _chunk_scan(graw, False))
    gcol = jnp.where((lane >= SM_A) & (lane < SM_B), gcol,
                     jnp.where((lane >= SM_B) & (lane < SM_B + 2 * C_HEADS), _sigmoid(small), 0.0))
    gcol_ref[...] = gcol
    grow_ref[...] = gcol.T[SM_A:SM_A + 16, :]


def _inproj(xs, mod, wts, tabs, layer):
    tile = lambda w: pl.BlockSpec((TM, w), lambda i: (i, 0))
    rope_spec = pl.BlockSpec((TM, LANES), lambda i: (_rope_block_of_tile(i), 0))
    res = lambda shape: _resident_layer(shape, layer)
    out_shapes = [
        jax.ShapeDtypeStruct((T_ALL, 512), BF16),
        jax.ShapeDtypeStruct((T_ALL, 128), F32),
        jax.ShapeDtypeStruct((T_ALL, 128), F32),
        jax.ShapeDtypeStruct((T_ALL, 1024), BF16),
        jax.ShapeDtypeStruct((T_ALL, 1024), BF16),
        jax.ShapeDtypeStruct((T_ALL, 512), BF16),
        jax.ShapeDtypeStruct((T_ALL, 256), F32),
        jax.ShapeDtypeStruct((T_ALL, 128), F32),
        jax.ShapeDtypeStruct((T_ALL, 1536), F32),
        jax.ShapeDtypeStruct((T_ALL, 128), F32),
        jax.ShapeDtypeStruct((16, T_ALL), F32),
        jax.ShapeDtypeStruct((T_ALL, 512), F32),
    ]
    out_specs = [tile(512), tile(128), tile(128), tile(1024), tile(1024), tile(512), tile(256), tile(128),
                 tile(1536), tile(128), pl.BlockSpec((16, TM), lambda i: (0, i)), tile(512)]
    in_specs = _stream_specs(D_MODEL, len(xs)) + [
        _mod_spec(layer),
        res((1, D_MODEL)),
        res((D_MODEL, W_HEAD_COLS)), res((D_MODEL, 1536)),
        pl.BlockSpec((None, D_MODEL, 512), lambda i: (layer, 0, 0), pipeline_mode=pl.Buffered(1)),
        rope_spec, rope_spec, rope_spec, rope_spec, rope_spec, rope_spec,
        res((1, LANES)), res((1, LANES)),
        res((1, B_Q_LORA)), res((1, B_KV_LORA)),
        res((B_Q_LORA, 1024)), res((B_KV_LORA, 1024)), _resident((LANES, 1024)),
        res((B_KV_LORA, 512)),
    ]
    return pl.pallas_call(
        functools.partial(_inproj_kernel, n_x=len(xs)),
        grid=(N_TILES,),
        in_specs=in_specs,
        out_specs=out_specs,
        out_shape=out_shapes,
        compiler_params=_cparams("arbitrary"),
        name="inproj",
    )(*xs, mod, wts["n1g"], wts["w_head"], wts["w_cqkv"], wts["w_tail"],
      tabs["ca"], tabs["sa"], tabs["cs"], tabs["ss"], tabs["cq"], tabs["sq"],
      wts["alog"], wts["dtb"], wts["gq"], wts["gkv"], wts["wuq"], wts["wukk"], wts["pkr"], wts["wukv"])


def _mla_cache_kernel(ckv_ref, kr_ref, wukk_ref, pkr_ref, wukv_ref, k_ref, v_ref):
    ckvb = ckv_ref[...].astype(BF16)
    krb = kr_ref[...].astype(BF16)
    k_ref[...] = (jnp.dot(ckvb, wukk_ref[...], preferred_element_type=F32)
                  + jnp.dot(krb, pkr_ref[...], preferred_element_type=F32)).astype(BF16)
    v_ref[...] = jnp.dot(ckvb, wukv_ref[...], preferred_element_type=F32).astype(BF16)


def _mla_cache(cache_ckv, cache_kr, wts, layer):
    n = DEC_BATCH * PAST_LEN
    res = lambda shape: _resident_layer(shape, layer)
    return pl.pallas_call(
        _mla_cache_kernel,
        grid=(DEC_BATCH,),
        in_specs=[pl.BlockSpec((None, None, PAST_LEN, B_KV_LORA), lambda i: (i, layer, 0, 0)),
                  pl.BlockSpec((None, None, PAST_LEN, LANES), lambda i: (i, layer, 0, 0)),
                  res((B_KV_LORA, 1024)), _resident((LANES, 1024)), res((B_KV_LORA, 512))],
        out_specs=[pl.BlockSpec((PAST_LEN, 1024), lambda i: (i, 0)),
                   pl.BlockSpec((PAST_LEN, 512), lambda i: (i, 0))],
        out_shape=[jax.ShapeDtypeStruct((n, 1024), BF16), jax.ShapeDtypeStruct((n, 512), BF16)],
        compiler_params=_cparams("arbitrary"),
        name="mla_cache",
    )(cache_ckv, cache_kr, wts["wukk"], wts["pkr"], wts["wukv"])


TQ = 256
A_WIN_KEYS = TQ + 2 * A_WINDOW
_NT = (((1,), (1,)), ((), ()))


def _softmax_pv(scores, values, extra_logit=None):
    m = scores[0].max(axis=-1, keepdims=True)
    for s in scores[1:]:
        m = jnp.maximum(m, s.max(axis=-1, keepdims=True))
    if extra_logit is not None:
        m = jnp.maximum(m, extra_logit)
    den = None
    acc = None
    for s, v in zip(scores, values):
        e = jnp.exp2(s - m)
        d = e.sum(axis=-1, keepdims=True)
        den = d if den is None else den + d
        pv = jnp.dot(e.astype(BF16), v, preferred_element_type=F32)
        acc = pv if acc is None else acc + pv
    if extra_logit is not None:
        den = den + jnp.exp2(extra_logit - m)
    return acc / den


def _attn_a_kernel(sink_ref, q_ref, k1_ref, v1_ref, *rest, latent, layer):
    if latent:
        k2_ref, v2_ref, o_ref = rest
        qt = pl.program_id(1)
        t0 = qt * TQ
        start = pl.multiple_of(jnp.clip(t0 - A_WINDOW, 0, DEC_SEQ - A_WIN_KEYS), A_WINDOW)
        k1 = k1_ref[pl.ds(start, A_WIN_KEYS), :]
        v1 = v1_ref[pl.ds(start, A_WIN_KEYS), :]
        qpos = t0 + lax.broadcasted_iota(jnp.int32, (TQ, A_WIN_KEYS), 0)
        kpos = start + lax.broadcasted_iota(jnp.int32, (TQ, A_WIN_KEYS), 1)
        valid = jnp.abs(qpos - kpos) <= A_WINDOW
    else:
        (o_ref,) = rest
        k1 = k1_ref[...]
        v1 = v1_ref[...]
    both = lambda x: (x.astype(BF16), pltpu.roll(x, A_HEAD_DIM, 1).astype(BF16))
    k1, v1 = both(k1), both(v1)
    if latent:
        k2, v2 = both(k2_ref[...]), both(v2_ref[...])
    lane = lax.broadcasted_iota(jnp.int32, (TQ, LANES), 1)
    low = lane < A_HEAD_DIM
    for c in range(A_HEADS // 2):
        qc = q_ref[:, c * LANES:(c + 1) * LANES]
        hk = (2 * c) // A_GROUP
        outs = []
        for p in range(2):
            sel = 0 if p == hk else 1
            qz = jnp.where(low if p == 0 else ~low, qc, jnp.zeros_like(qc))
            s1 = lax.dot_general(qz, k1[sel], _NT, preferred_element_type=F32)
            sink = sink_ref[layer, 2 * c + p] * LOG2E
            if latent:
                s1 = jnp.where(valid, s1, NEG_INF)
                s2 = lax.dot_general(qz, k2[sel], _NT, preferred_element_type=F32)
                outs.append(_softmax_pv([s1, s2], [v1[sel], v2[sel]], sink))
            else:
                outs.append(_softmax_pv([s1], [v1[sel]], sink))
        o_ref[:, c * LANES:(c + 1) * LANES] = jnp.where(low, outs[0], outs[1]).astype(BF16)


def _attn_a_ctx(sink, aq, ak, av, layer):
    nb = T_CTX // TQ
    return pl.pallas_call(
        functools.partial(_attn_a_kernel, latent=False, layer=layer),
        grid=(nb,),
        in_specs=[pl.BlockSpec(memory_space=pltpu.SMEM),
                  pl.BlockSpec((TQ, 512), lambda b: (b, 0)),
                  pl.BlockSpec((SEQ, LANES), lambda b: (b, 0)),
                  pl.BlockSpec((SEQ, LANES), lambda b: (b, 0))],
        out_specs=pl.BlockSpec((TQ, 512), lambda b: (b, 0)),
        out_shape=jax.ShapeDtypeStruct((T_CTX, 512), BF16),
        compiler_params=_cparams("arbitrary"),
        name="attn_a_ctx",
    )(sink, aq, ak, av)


def _attn_a_lat(sink, aq, ak, av, ck, cv, layer):
    nq = DEC_SEQ // TQ
    ctx_tiles = T_CTX // TQ
    seq_blocks = T_CTX // DEC_SEQ
    cache_spec = pl.BlockSpec((None, None, PAST_LEN, LANES), lambda b, i: (b, layer, 0, 0))
    return pl.pallas_call(
        functools.partial(_attn_a_kernel, latent=True, layer=layer),
        grid=(DEC_BATCH, nq),
        in_specs=[pl.BlockSpec(memory_space=pltpu.SMEM),
                  pl.BlockSpec((TQ, 512), lambda b, i: (ctx_tiles + b * nq + i, 0)),
                  pl.BlockSpec((DEC_SEQ, LANES), lambda b, i: (seq_blocks + b, 0)),
                  pl.BlockSpec((DEC_SEQ, LANES), lambda b, i: (seq_blocks + b, 0)),
                  cache_spec, cache_spec],
        out_specs=pl.BlockSpec((TQ, 512), lambda b, i: (b * nq + i, 0)),
        out_shape=jax.ShapeDtypeStruct((T_LAT, 512), BF16),
        compiler_params=_cparams("arbitrary", "arbitrary"),
        name="attn_a_lat",
    )(sink, aq, ak, av, ck, cv)


MLA_TQ_LAT = 512


def _mla_kernel(q_ref, k1_ref, v1_ref, *rest, latent):
    if latent:
        k2_ref, v2_ref, o_ref = rest
    else:
        (o_ref,) = rest
    lane = lax.broadcasted_iota(jnp.int32, (q_ref.shape[0], LANES), 1)
    low = lane < B_V
    for j in range(B_HEADS // 2):
        outs = []
        for sub in range(2):
            hh = 2 * j + sub
            hs = slice(hh * LANES, (hh + 1) * LANES)
            vs = slice(j * LANES, (j + 1) * LANES)
            qh = q_ref[:, hs]
            s1 = lax.dot_general(qh, k1_ref[:, hs], _NT, preferred_element_type=F32)
            if latent:
                s2 = lax.dot_general(qh, k2_ref[:, hs], _NT, preferred_element_type=F32)
                outs.append(_softmax_pv([s1, s2], [v1_ref[:, vs], v2_ref[:, vs]]))
            else:
                outs.append(_softmax_pv([s1], [v1_ref[:, vs]]))
        o_ref[:, j * LANES:(j + 1) * LANES] = jnp.where(low, outs[0], outs[1]).astype(BF16)


def _mla_ctx(bq, bk, bv):
    nb = T_CTX // TQ
    return pl.pallas_call(
        functools.partial(_mla_kernel, latent=False),
        grid=(nb,),
        in_specs=[pl.BlockSpec((TQ, 1024), lambda b: (b, 0)),
                  pl.BlockSpec((SEQ, 1024), lambda b: (b, 0)),
                  pl.BlockSpec((SEQ, 512), lambda b: (b, 0))],
        out_specs=pl.BlockSpec((TQ, 512), lambda b: (b, 0)),
        out_shape=jax.ShapeDtypeStruct((T_CTX, 512), BF16),
        compiler_params=_cparams("arbitrary"),
        name="mla_ctx",
    )(bq, bk, bv)


def _mla_lat(bq, bk, bv, k2, v2):
    tq = MLA_TQ_LAT
    nq = DEC_SEQ // tq
    ctx_tiles = T_CTX // tq
    seq_blocks = T_CTX // DEC_SEQ
    return pl.pallas_call(
        functools.partial(_mla_kernel, latent=True),
        grid=(DEC_BATCH, nq),
        in_specs=[pl.BlockSpec((tq, 1024), lambda b, i: (ctx_tiles + b * nq + i, 0)),
                  pl.BlockSpec((DEC_SEQ, 1024), lambda b, i: (seq_blocks + b, 0)),
                  pl.BlockSpec((DEC_SEQ, 512), lambda b, i: (seq_blocks + b, 0)),
                  pl.BlockSpec((PAST_LEN, 1024), lambda b, i: (b, 0)),
                  pl.BlockSpec((PAST_LEN, 512), lambda b, i: (b, 0))],
        out_specs=pl.BlockSpec((tq, 512), lambda b, i: (b * nq + i, 0)),
        out_shape=jax.ShapeDtypeStruct((T_LAT, 512), BF16),
        compiler_params=_cparams("arbitrary", "arbitrary"),
        name="mla_lat",
    )(bq, bk, bv, k2, v2)


def _gdn_kernel(xq_ref, xk_ref, xv_ref, wq_ref, wk_ref, wv_ref, gcol_ref, grow_ref, cz_ref, ong_ref, *rest,
                n, hps, has_s0, emit_state):
    rest = list(rest)
    s0_ref = rest.pop(0) if has_s0 else None
    oc_ref = rest.pop(0)
    sfin_ref = rest.pop(0) if emit_state else None
    q_s, k_s, v_s, u_s, w_s, qk_s, qd_s, kdt_s, ge_s, st_s, o_s = rest
    nc = n // GDN_CHUNK
    head0 = pl.program_id(1) * hps
    C = GDN_CHUNK
    width = hps * LANES
    hl = lambda hh: slice(hh * LANES, (hh + 1) * LANES)
    chains = [(d, hh) for d in range(2) for hh in range(hps)]

    row8 = lax.broadcasted_iota(jnp.int32, (8, width), 0)

    def conv_silu(x_ref, w_ref):
        x = x_ref[...]
        w = w_ref[...]
        y = pltpu.roll(x, 1, 0) * w[0:1] + x * w[1:2] + pltpu.roll(x, n - 1, 0) * w[2:3]
        top = y[0:8] - jnp.where(row8 == 0, x[n - 1:n] * w[0:1], 0.0)
        bot = y[n - 8:n] - jnp.where(row8 == 7, x[0:1] * w[2:3], 0.0)
        return _silu(jnp.concatenate([top, y[8:n - 8], bot], axis=0))

    def l2n(x):
        return x * lax.rsqrt(jnp.sum(x * x, axis=-1, keepdims=True) + EPS)

    qc = conv_silu(xq_ref, wq_ref)
    kc = conv_silu(xk_ref, wk_ref)
    for hh in range(hps):
        q_s[:, hl(hh)] = l2n(qc[:, hl(hh)]) * (C_DK ** -0.5)
        k_s[:, hl(hh)] = l2n(kc[:, hl(hh)])
    v_s[...] = conv_silu(xv_ref, wv_ref)
    o_s[...] = jnp.zeros_like(o_s)

    lane_c = lax.broadcasted_iota(jnp.int32, (C, LANES), 1)
    ri = lax.broadcasted_iota(jnp.int32, (C, C), 0)
    ci = lax.broadcasted_iota(jnp.int32, (C, C), 1)
    eye = (ri == ci).astype(F32)
    same_block = {}
    s = 2
    while s <= C:
        same_block[s] = (ri // s) == (ci // s)
        s *= 2

    def prep(it, carry):
        items = []
        for gi in range(group):
            r = it * group + gi
            rows = pl.ds(pl.multiple_of(r * C, C), C)
            gtile = gcol_ref[rows, :]
            for hh in range(hps):
                q = q_s[rows, hl(hh)]
                k = k_s[rows, hl(hh)]
                v = v_s[rows, hl(hh)]
                qk_raw = _dot_nt(q, k)
                for d in range(2):
                    slot = d * hps + hh
                    j = d * C_HEADS + head0 + hh
                    gc = jnp.sum(jnp.where(lane_c == SM_A + j, gtile, 0.0), axis=1, keepdims=True)
                    beta = jnp.sum(jnp.where(lane_c == SM_B + j, gtile, 0.0), axis=1, keepdims=True)
                    grow = grow_ref[j, r]
                    incl = (ri >= ci) if d == 0 else (ri <= ci)
                    strict = (ri > ci) if d == 0 else (ri < ci)
                    decay = jnp.where(incl, jnp.exp(jnp.where(incl, gc - grow, 0.0)), 0.0)
                    kb = k * beta
                    a = jnp.where(strict, _dot_nt(kb, k) * decay, 0.0)
                    g_end = gc[C - 1:C, :] if d == 0 else gc[0:1, :]
                    egc = jnp.exp(gc)
                    qk_s[slot, rows, :] = (qk_raw * decay).astype(BF16)
                    qd_s[slot, rows, :] = (q * egc).astype(BF16)
                    kdt_s[slot, rows, :] = (k * jnp.exp(g_end - gc)).T.astype(BF16)
                    ge_s[slot, r] = jnp.broadcast_to(jnp.exp(g_end), (8, LANES))
                    items.append((slot, rows, a, jnp.concatenate([v * beta, kb * egc], axis=1), d))
        ts = [eye - jnp.where(same_block[2], a, 0.0) for (_, _, a, _, _) in items]
        s = 2
        while s < C:
            level = same_block[2 * s] & ~same_block[s]
            es = [jnp.where(level, a, 0.0) for (_, _, a, _, _) in items]
            if s < 8:
                xs = [_dot(t, e) for t, e in zip(ts, es)]
                ts = [t - _dot(x, t) for t, x in zip(ts, xs)]
            else:
                def active(t, d):
                    return jnp.concatenate([t[b * s:(b + 1) * s] for b in range(1 - d, C // s, 2)], axis=0)

                def merged(t, upd, d):
                    return jnp.concatenate(
                        [upd[(b // 2) * s:(b // 2 + 1) * s] if b % 2 == 1 - d else t[b * s:(b + 1) * s]
                         for b in range(C // s)], axis=0)

                halves = [active(t, it_[4]) for t, it_ in zip(ts, items)]
                xs = [_dot(h, e) for h, e in zip(halves, es)]
                upds = [h - _dot(x, t) for h, x, t in zip(halves, xs, ts)]
                ts = [merged(t, u, it_[4]) for t, u, it_ in zip(ts, upds, items)]
            s *= 2
        for t, (slot, rows, _, rhs, _) in zip(ts, items):
            uw = _dot(t, rhs)
            u_s[slot, rows, :] = uw[:, :C_DV]
            w_s[slot, rows, :] = uw[:, C_DV:].astype(BF16)
        return carry

    group = max(1, min(nc, GDN_PREP_CHAINS // (2 * hps)))
    lax.fori_loop(0, nc // group, prep, 0)

    for d, hh in chains:
        if has_s0:
            st_s[d * hps + hh] = s0_ref[d, hh]
        else:
            st_s[d * hps + hh] = jnp.zeros((C_DK, C_DV), F32)

    def step(i, carry):
        rows = [pl.ds(pl.multiple_of(i * C, C), C), pl.ds(pl.multiple_of((nc - 1 - i) * C, C), C)]
        chunk = [i, nc - 1 - i]
        slots = [d * hps + hh for d, hh in chains]
        st = [st_s[c] for c in slots]
        stb = [s_.astype(BF16) for s_ in st]
        ws = [jnp.dot(w_s[c, rows[d], :], sb, preferred_element_type=F32)
              for c, (d, _), sb in zip(slots, chains, stb)]
        vnb = [(u_s[c, rows[d], :] - x).astype(BF16) for c, (d, _), x in zip(slots, chains, ws)]
        for c, (d, hh), sb, vb, s_ in zip(slots, chains, stb, vnb, st):
            o = (jnp.dot(qd_s[c, rows[d], :], sb, preferred_element_type=F32)
                 + jnp.dot(qk_s[c, rows[d], :], vb, preferred_element_type=F32))
            o_s[rows[d], hl(hh)] += o
            st_s[c] = s_ * ge_s[c, chunk[d]][0:1, :] + jnp.dot(kdt_s[c, rows[d], :], vb,
                                                                preferred_element_type=F32)
        return carry

    lax.fori_loop(0, nc, step, 0)
    if emit_state:
        for d, hh in chains:
            sfin_ref[0, d, hh] = st_s[d * hps + hh]

    for hh in range(hps):
        o = o_s[:, hl(hh)]
        oc_ref[:, hl(hh)] = (_rms(o) * ong_ref[...] * _silu(cz_ref[:, hl(hh)])).astype(BF16)


def _gdn(cqkv, conv_w, gcol, grow3, cz, ong, state_c, layer, *, latent):
    n = DEC_SEQ if latent else SEQ
    nseq = DEC_BATCH if latent else BATCH
    hps = GDN_HEADS_PER_STEP_LAT if latent else GDN_HEADS_PER_STEP_CTX
    seq0 = (T_CTX // n) if latent else 0
    nc = n // GDN_CHUNK
    width = hps * LANES
    sect = C_HEADS // hps
    col = lambda base: pl.BlockSpec((n, width), lambda b, h: (seq0 + b, base + h))
    wcol = lambda base: pl.BlockSpec((None, 3, width), lambda b, h: (layer, 0, base + h))
    in_specs = [col(0), col(sect), col(2 * sect), wcol(0), wcol(sect), wcol(2 * sect),
                pl.BlockSpec((n, LANES), lambda b, h: (seq0 + b, 0)),
                pl.BlockSpec((16, nc, 1, LANES), lambda b, h: (0, seq0 + b, 0, 0)),
                pl.BlockSpec((n, width), lambda b, h: (seq0 + b, h)),
                pl.BlockSpec((None, 1, LANES), lambda b, h: (layer, 0, 0))]
    args = [cqkv, cqkv, cqkv, conv_w, conv_w, conv_w, gcol, grow3, cz, ong]
    out_specs = [pl.BlockSpec((n, width), lambda b, h: (b, h))]
    out_shape = [jax.ShapeDtypeStruct((nseq * n, 512), BF16)]
    if latent:
        in_specs.append(pl.BlockSpec((None, None, 2, hps, C_DK, C_DV), lambda b, h: (b, layer, 0, h, 0, 0)))
        args.append(state_c)
    else:
        out_specs.append(pl.BlockSpec((1, 2, hps, C_DK, C_DV), lambda b, h: (b, 0, h, 0, 0)))
        out_shape.append(jax.ShapeDtypeStruct((nseq, 2, C_HEADS, C_DK, C_DV), F32))
    nch = 2 * hps
    scratch = [pltpu.VMEM((n, width), F32), pltpu.VMEM((n, width), F32), pltpu.VMEM((n, width), F32),
               pltpu.VMEM((nch, n, LANES), F32), pltpu.VMEM((nch, n, LANES), BF16),
               pltpu.VMEM((nch, n, LANES), BF16), pltpu.VMEM((nch, n, LANES), BF16),
               pltpu.VMEM((nch, n, LANES), BF16), pltpu.VMEM((nch, nc, 8, LANES), F32),
               pltpu.VMEM((nch, C_DK, C_DV), F32), pltpu.VMEM((n, width), F32)]
    res = pl.pallas_call(
        functools.partial(_gdn_kernel, n=n, hps=hps, has_s0=latent, emit_state=not latent),
        grid=(nseq, C_HEADS // hps),
        in_specs=in_specs,
        out_specs=out_specs,
        out_shape=out_shape,
        scratch_shapes=scratch,
        compiler_params=_cparams("arbitrary", "arbitrary"),
        name="gdn_lat" if latent else "gdn_ctx",
    )(*args)
    return res


def _merge_kernel(*refs, n_x):
    x_refs, refs = refs[:n_x], refs[n_x:]
    branch_refs, refs = refs[:6], refs[6:]
    mod_ref, n1g_ref, wtail_ref, wbr_ref, wout_ref, o_ref = refs
    mod = mod_ref[0]
    ys = [jnp.dot(_pick_stream(branch_refs[2 * j:2 * j + 2]), wbr_ref[j], preferred_element_type=F32)
          for j in range(3)]
    x = _pick_stream(x_refs)
    hb = (_rms(x) * n1g_ref[...] * (1.0 + mod[1:2]) + mod[0:1]).astype(BF16)
    z = None
    for j in range(3):
        cols = slice(512 + j * D_MODEL, 512 + (j + 1) * D_MODEL)
        gate = _sigmoid(jnp.dot(hb, wtail_ref[:, cols], preferred_element_type=F32))
        z = gate * ys[j] if z is None else z + gate * ys[j]
    o_ref[...] = x + mod[2:3] * jnp.dot(z.astype(BF16), wout_ref[...], preferred_element_type=F32)


def _merge(xs, branches, mod, wts, layer):
    res = lambda shape: _resident_layer(shape, layer)
    in_specs = _stream_specs(D_MODEL, len(xs))
    for _ in range(3):
        in_specs += _stream_specs(BRANCH_W, 2)
    in_specs += [_mod_spec(layer), res((1, D_MODEL)), res((D_MODEL, W_TAIL_COLS)),
                 res((3, BRANCH_W, D_MODEL)), res((D_MODEL, D_MODEL))]
    return pl.pallas_call(
        functools.partial(_merge_kernel, n_x=len(xs)),
        grid=(N_TILES,),
        in_specs=in_specs,
        out_specs=pl.BlockSpec((TM, D_MODEL), lambda i: (i, 0)),
        out_shape=jax.ShapeDtypeStruct((T_ALL, D_MODEL), F32),
        compiler_params=_cparams("arbitrary"),
        name="merge",
    )(*xs, *branches, mod, wts["n1g"], wts["w_tail"], wts["wbr"], wts["wout"])


FF_CHUNK = 1024


def _ffn_kernel(x_ref, mod_ref, n2g_ref, wff1_ref, wff2_ref, fg_ref, o_ref, *, final):
    mod = mod_ref[0]
    x = x_ref[...]
    hb = (_rms(x) * n2g_ref[...] * (1.0 + mod[4:5]) + mod[3:4]).astype(BF16)
    acc = None
    for c in range(D_FF // FF_CHUNK):
        cols = slice(c * FF_CHUNK, (c + 1) * FF_CHUNK)
        f = jnp.maximum(jnp.dot(hb, wff1_ref[:, cols], preferred_element_type=F32), 0.0)
        p = jnp.dot((f * f).astype(BF16), wff2_ref[cols, :], preferred_element_type=F32)
        acc = p if acc is None else acc + p
    x = x + mod[5:6] * acc
    if final:
        x = _rms(x) * fg_ref[...]
    o_ref[...] = x


def _ffn(x_all, mod, wts, final_g, layer, *, tile0, n_tiles, final):
    res = lambda shape: _resident_layer(shape, layer)
    return pl.pallas_call(
        functools.partial(_ffn_kernel, final=final),
        grid=(n_tiles,),
        in_specs=[pl.BlockSpec((TM, D_MODEL), lambda i: (tile0 + i, 0)), _mod_spec(layer, tile0),
                  res((1, D_MODEL)), res((D_MODEL, D_FF)), res((D_FF, D_MODEL)), _resident((1, D_MODEL))],
        out_specs=pl.BlockSpec((TM, D_MODEL), lambda i: (i, 0)),
        out_shape=jax.ShapeDtypeStruct((n_tiles * TM, D_MODEL), F32),
        compiler_params=_cparams("arbitrary"),
        name="ffn",
    )(x_all, mod, wts["n2g"], wts["wff1"], wts["wff2"], final_g)


def _rope_tables(r):
    half, quarter = r // 2, r // 4
    t = jnp.arange(DEC_SEQ)
    inv_freq = jnp.power(ROPE_BASE, -jnp.arange(quarter, dtype=F32) / quarter)
    cs, ss = [], []
    for pos in (t // GRID_W, t % GRID_W):
        ang = pos.astype(F32)[:, None] * inv_freq[None, :]
        c, s = jnp.cos(ang), jnp.sin(ang)
        cs += [c, c]
        ss += [-s, s]
    return jnp.concatenate(cs, axis=1), jnp.concatenate(ss, axis=1)


def _tables():
    def with_identity(c, s):
        one = jnp.ones((TM, LANES), F32)
        return jnp.concatenate([one, c], axis=0), jnp.concatenate([0.0 * one, s], axis=0)

    c64, s64 = _rope_tables(A_HEAD_DIM)
    ca, sa = with_identity(jnp.tile(c64, (1, 2)), jnp.tile(s64, (1, 2)))
    c32, s32 = _rope_tables(B_ROPE)
    ones = lambda w: jnp.ones((DEC_SEQ, w), F32)
    zeros = lambda w: jnp.zeros((DEC_SEQ, w), F32)
    cs, ss = with_identity(jnp.concatenate([c32, ones(96)], axis=1), jnp.concatenate([s32, zeros(96)], axis=1))
    cq, sq = with_identity(jnp.concatenate([ones(64), c32, ones(32)], axis=1),
                           jnp.concatenate([zeros(64), s32, zeros(32)], axis=1))
    return {"ca": ca, "sa": sa, "cs": cs, "ss": ss, "cq": cq, "sq": sq}


def _prep_weights(norm1_g, norm2_g, w_in, b_gq, b_w_uq, b_gkv, b_w_ukv, c_a_log, c_dt_bias, c_onorm_g,
                  w_branch, w_out, w_ff1, w_ff2):
    offs = [0]
    for width in W_IN_SPLITS:
        offs.append(offs[-1] + width)
    piece = lambda i, j: w_in[:, :, offs[i]:offs[j]]
    w_head = jnp.concatenate([piece(0, 6), piece(7, 9), jnp.zeros((DEPTH, D_MODEL, LANES - 48), F32)],
                             axis=2).astype(BF16)
    w_cqkv = piece(6, 7).astype(BF16)
    w_tail = piece(9, 11).astype(BF16)

    pad_lane = lambda v: jnp.zeros((DEPTH, 1, LANES), F32).at[:, 0, SM_A:SM_A + 8].set(v.reshape(DEPTH, 8))
    uq = b_w_uq.reshape(DEPTH, B_Q_LORA, B_HEADS, B_NOPE + B_ROPE)
    uq = jnp.pad(uq, ((0, 0), (0, 0), (0, 0), (0, LANES - B_NOPE - B_ROPE)))
    uq = uq.reshape(DEPTH, B_Q_LORA, 1024).astype(BF16)
    ukv = b_w_ukv.reshape(DEPTH, B_KV_LORA, B_HEADS, B_NOPE + B_V)
    ukk = jnp.pad(ukv[..., :B_NOPE], ((0, 0), (0, 0), (0, 0), (0, LANES - B_NOPE)))
    ukk = ukk.reshape(DEPTH, B_KV_LORA, 1024).astype(BF16)
    ukvv = ukv[..., B_NOPE:].reshape(DEPTH, B_KV_LORA, 512).astype(BF16)
    src = jnp.arange(LANES)[:, None]
    dst = jnp.arange(1024)[None, :]
    pkr = ((dst % LANES == B_NOPE + src) & (src < B_ROPE)).astype(BF16)

    return {"w_head": w_head, "w_cqkv": w_cqkv, "w_tail": w_tail,
            "alog": pad_lane(c_a_log), "dtb": pad_lane(c_dt_bias),
            "n1g": norm1_g.reshape(DEPTH, 1, D_MODEL), "n2g": norm2_g.reshape(DEPTH, 1, D_MODEL),
            "gq": b_gq.reshape(DEPTH, 1, B_Q_LORA), "gkv": b_gkv.reshape(DEPTH, 1, B_KV_LORA),
            "ong": c_onorm_g.reshape(DEPTH, 1, C_DV),
            "wuq": uq, "wukk": ukk, "pkr": pkr, "wukv": ukvv,
            "wbr": w_branch.astype(BF16), "wout": w_out.astype(BF16), "wff1": w_ff1.astype(BF16), "wff2": w_ff2.astype(BF16)}


def kernel(x_prompt, x_sample, cache_a_k, cache_a_v, cache_b_ckv, cache_b_krope, state_c, c, c_ctx, norm1_g,
           norm2_g, w_ada, b_ada, w_in, a_sink, b_gq, b_w_uq, b_gkv, b_w_ukv, c_conv_w, c_a_log, c_dt_bias,
           c_onorm_g, w_branch, w_out, w_ff1, w_ff2, final_g):
    xs = (x_prompt.reshape(T_CTX, D_MODEL), x_sample.reshape(T_LAT, D_MODEL))
    cond8 = jnp.concatenate([c_ctx[None, :], c, jnp.zeros((N_GROUPS - 1 - DEC_BATCH, D_MODEL), F32)], axis=0)
    mod = _adaln(cond8, w_ada, b_ada).reshape(DEPTH, N_GROUPS, 6, D_MODEL)
    tabs = _tables()
    fg = final_g.reshape(1, D_MODEL)
    wts = _prep_weights(norm1_g, norm2_g, w_in, b_gq, b_w_uq, b_gkv, b_w_ukv, c_a_log, c_dt_bias, c_onorm_g,
                        w_branch, w_out, w_ff1, w_ff2)
    ck = cache_a_k.reshape(DEC_BATCH, DEPTH, PAST_LEN, LANES)
    cv = cache_a_v.reshape(DEC_BATCH, DEPTH, PAST_LEN, LANES)
    cache_kr = jnp.pad(cache_b_krope, ((0, 0), (0, 0), (0, 0), (0, LANES - B_ROPE)))

    ak_l, av_l, ckv_l, kr_l, sc_l = [], [], [], [], []
    for l in range(DEPTH):
        (aq, ak, av, bq, bk, bv, ckv, small, cqkv, gcol, grow, cz) = _inproj(xs, mod, wts, tabs, l)

        oa_ctx = _attn_a_ctx(a_sink, aq, ak, av, l)
        oa_lat = _attn_a_lat(a_sink, aq, ak, av, ck, cv, l)

        k2, v2 = _mla_cache(cache_b_ckv, cache_kr, wts, l)
        ob_ctx = _mla_ctx(bq, bk, bv)
        ob_lat = _mla_lat(bq, bk, bv, k2, v2)

        grow3 = grow.reshape(16, T_ALL // GDN_CHUNK, 1, GDN_CHUNK)
        oc_ctx, s_fin = _gdn(cqkv, c_conv_w, gcol, grow3, cz, wts["ong"], state_c, l, latent=False)
        (oc_lat,) = _gdn(cqkv, c_conv_w, gcol, grow3, cz, wts["ong"], state_c, l, latent=True)

        branches = (oa_ctx, oa_lat, ob_ctx, ob_lat, oc_ctx, oc_lat)
        x_mid = _merge(xs, branches, mod, wts, l)
        ffn = functools.partial(_ffn, x_mid, mod, wts, fg, l)
        if l < DEPTH - 1:
            xs = (ffn(tile0=0, n_tiles=N_TILES, final=False),)
        else:
            xs = (ffn(tile0=0, n_tiles=N_CTX_TILES, final=True),
                  ffn(tile0=N_CTX_TILES, n_tiles=N_TILES - N_CTX_TILES, final=True))

        ak_l.append(ak[:T_CTX].reshape(BATCH, SEQ, A_KV_HEADS, A_HEAD_DIM))
        av_l.append(av[:T_CTX].reshape(BATCH, SEQ, A_KV_HEADS, A_HEAD_DIM))
        ckv_l.append(ckv[:T_CTX].reshape(BATCH, SEQ, B_KV_LORA))
        kr_l.append(small[:T_CTX, :B_ROPE].reshape(BATCH, SEQ, B_ROPE))
        sc_l.append(s_fin)

    y_prompt = xs[0].reshape(BATCH, SEQ, D_MODEL)
    y_sample = xs[1].reshape(DEC_BATCH, DEC_SEQ, D_MODEL)
    return (y_prompt, y_sample, jnp.stack(ak_l, axis=1), jnp.stack(av_l, axis=1), jnp.stack(ckv_l, axis=1),
            jnp.stack(kr_l, axis=1), jnp.stack(sc_l, axis=1))
```

```python
import functools
import math

import jax
import jax.numpy as jnp
from jax import lax
from jax.experimental import pallas as pl
from jax.experimental.pallas import tpu as pltpu

F32 = jnp.float32
BF16 = jnp.bfloat16

D_MODEL = 1024
BATCH = 16
SEQ = 256
DEPTH = 2
DEC_BATCH = 4
DEC_SEQ = 2048
PAST_LEN = 256
GRID_W = 64
ROPE_BASE = 10000.0
EPS = 1e-6
NEG_INF = -1e30

A_HEADS = 8
A_KV_HEADS = 2
A_GROUP = 4
A_HEAD_DIM = 64
A_WINDOW = 128
A_SCALE = A_HEAD_DIM ** -0.5
B_HEADS = 8
B_Q_LORA = 384
B_KV_LORA = 256
B_NOPE = 64
B_ROPE = 32
B_V = 64
MLA_SCALE = (B_NOPE + B_ROPE) ** -0.5
LOG2E = math.log2(math.e)
C_HEADS = 4
C_DK = 128
C_DV = 128
D_FF = 4 * D_MODEL
BRANCH_W = 512

LANES = 128

T_CTX = BATCH * SEQ
T_LAT = DEC_BATCH * DEC_SEQ
T_ALL = T_CTX + T_LAT
TM = 512
N_CTX_TILES = T_CTX // TM
LAT_TILES_PER_SEQ = DEC_SEQ // TM
N_TILES = T_ALL // TM
N_GROUPS = 8
GDN_CHUNK = 128
GDN_PREP_CHAINS = 16
GDN_HEADS_PER_STEP_CTX = 4
GDN_HEADS_PER_STEP_LAT = 2

W_IN_SPLITS = (512, 128, 128, B_Q_LORA, B_KV_LORA, B_ROPE, 1536, 8, 8, 512, 3072)
O_AQ, O_AK, O_AV, O_BCQ, O_BCKV, O_SMALL, W_HEAD_COLS = 0, 512, 640, 768, 1152, 1408, 1536
W_TAIL_COLS = 512 + 3 * D_MODEL
SM_KR, SM_A, SM_B = 0, 32, 40

VMEM_LIMIT = 56 * 1024 * 1024


def _cparams(*sem):
    return pltpu.CompilerParams(dimension_semantics=sem, vmem_limit_bytes=VMEM_LIMIT)


def _resident(shape):
    nd = len(shape)
    return pl.BlockSpec(shape, lambda *_: (0,) * nd, pipeline_mode=pl.Buffered(1))


def _resident_layer(shape, layer):
    nd = len(shape)
    return pl.BlockSpec((None,) + tuple(shape), lambda *_: (layer,) + (0,) * nd, pipeline_mode=pl.Buffered(1))


def _group_of_tile(i):
    return jnp.where(i < N_CTX_TILES, 0, 1 + (i - N_CTX_TILES) // LAT_TILES_PER_SEQ)


def _rope_block_of_tile(i):
    return jnp.where(i < N_CTX_TILES, 0, 1 + (i - N_CTX_TILES) % LAT_TILES_PER_SEQ)


def _mod_spec(layer, tile0=0):
    return pl.BlockSpec((None, 1, 6, D_MODEL), lambda i: (layer, _group_of_tile(tile0 + i), 0, 0))


def _stream_specs(width, n_arrays, tile0=0):
    if n_arrays == 1:
        return [pl.BlockSpec((TM, width), lambda i: (tile0 + i, 0))]
    return [pl.BlockSpec((TM, width), lambda i: (jnp.minimum(tile0 + i, N_CTX_TILES - 1), 0)),
            pl.BlockSpec((TM, width), lambda i: (jnp.maximum(tile0 + i - N_CTX_TILES, 0), 0))]


def _pick_stream(refs, tile0=0):
    if len(refs) == 1:
        return refs[0][...]
    return jnp.where(tile0 + pl.program_id(0) < N_CTX_TILES, refs[0][...], refs[1][...])


def _rms(x):
    return x * lax.rsqrt(jnp.mean(x * x, axis=-1, keepdims=True) + EPS)


def _sigmoid(x):
    return 0.5 * jnp.tanh(0.5 * x) + 0.5


def _silu(x):
    h = 0.5 * x
    return h * jnp.tanh(h) + h


def _softplus(x):
    return jnp.maximum(x, 0.0) + jnp.log(1.0 + jnp.exp(-jnp.abs(x)))


def _dot(a, b):
    return jnp.dot(a.astype(BF16), b.astype(BF16), preferred_element_type=F32)


def _dot_nt(a, b):
    return lax.dot_general(a.astype(BF16), b.astype(BF16), (((1,), (1,)), ((), ())),
                           preferred_element_type=F32)


def _swap_groups(x, q):
    w = x.shape[1]
    lane = lax.broadcasted_iota(jnp.int32, x.shape, 1)
    even = ((lane // q) % 2) == 0
    return jnp.where(even, pltpu.roll(x, w - q, 1), pltpu.roll(x, q, 1))


def _rope(x, cos, sin, q):
    return x * cos + _swap_groups(x, q) * sin


ADA_TN = 1536


def _adaln_kernel(c_ref, w_ref, b_ref, o_ref):
    c = c_ref[...]
    o_ref[0] = jnp.dot(_silu(c), w_ref[0], preferred_element_type=F32) + b_ref[0]


def _adaln(cond8, w_ada, b_ada):
    n = 6 * D_MODEL
    return pl.pallas_call(
        _adaln_kernel,
        grid=(DEPTH, n // ADA_TN),
        in_specs=[
            pl.BlockSpec((N_GROUPS, D_MODEL), lambda l, j: (0, 0)),
            pl.BlockSpec((1, D_MODEL, ADA_TN), lambda l, j: (l, 0, j)),
            pl.BlockSpec((1, 1, ADA_TN), lambda l, j: (l, 0, j)),
        ],
        out_specs=pl.BlockSpec((1, N_GROUPS, ADA_TN), lambda l, j: (l, 0, j)),
        out_shape=jax.ShapeDtypeStruct((DEPTH, N_GROUPS, n), F32),
        compiler_params=_cparams("arbitrary", "arbitrary"),
        name="adaln",
    )(cond8, w_ada, b_ada.reshape(DEPTH, 1, n))


def _chunk_scan(g, forward):
    n = g.shape[0]
    pos = lax.broadcasted_iota(jnp.int32, g.shape, 0) % GDN_CHUNK
    s = 1
    while s < GDN_CHUNK:
        if forward:
            g = g + jnp.where(pos >= s, pltpu.roll(g, s, 0), 0.0)
        else:
            g = g + jnp.where(pos < GDN_CHUNK - s, pltpu.roll(g, n - s, 0), 0.0)
        s *= 2
    return g


def _inproj_kernel(*refs, n_x):
    x_refs, refs = refs[:n_x], refs[n_x:]
    (mod_ref, n1g_ref, w_head_ref, w_cqkv_ref, w_cz_ref,
     ca_ref, sa_ref, cs_ref, ss_ref, cq_ref, sq_ref,
     alog_ref, dtb_ref, gq_ref, gkv_ref, wuq_ref, wukk_ref, pkr_ref, wukv_ref,
     aq_ref, ak_ref, av_ref, bq_ref, bk_ref, bv_ref, ckv_ref, small_ref,
     cqkv_ref, gcol_ref, grow_ref, cz_ref) = refs
    mod = mod_ref[0]
    h = _rms(_pick_stream(x_refs)) * n1g_ref[...] * (1.0 + mod[1:2]) + mod[0:1]
    hb = h.astype(BF16)

    def proj(w_ref, lo, hi):
        return jnp.dot(hb, w_ref[:, lo:hi], preferred_element_type=F32)

    ca, sa = ca_ref[...], sa_ref[...]
    a_all = proj(w_head_ref, O_AQ, O_BCQ)
    for c in range(4):
        q = a_all[:, c * LANES:(c + 1) * LANES]
        aq_ref[:, c * LANES:(c + 1) * LANES] = (_rope(q, ca, sa, 16) * (A_SCALE * LOG2E)).astype(BF16)
    ak_ref[...] = _rope(a_all[:, O_AK:O_AV], ca, sa, 16)
    av_ref[...] = a_all[:, O_AV:O_BCQ]

    b_all = proj(w_head_ref, O_BCQ, W_HEAD_COLS)
    cq = (_rms(b_all[:, :B_Q_LORA]) * gq_ref[...]).astype(BF16)
    cqc, sqc = cq_ref[...], sq_ref[...]
    q_all = jnp.dot(cq, wuq_ref[...], preferred_element_type=F32)
    for hh in range(B_HEADS):
        qh = q_all[:, hh * LANES:(hh + 1) * LANES]
        bq_ref[:, hh * LANES:(hh + 1) * LANES] = (_rope(qh, cqc, sqc, 8) * (MLA_SCALE * LOG2E)).astype(BF16)

    ckv = _rms(b_all[:, B_Q_LORA:B_Q_LORA + B_KV_LORA]) * gkv_ref[...]
    ckv_ref[...] = ckv
    ckvb = ckv.astype(BF16)
    small = b_all[:, O_SMALL - O_BCQ:]
    small_r = _rope(small, cs_ref[...], ss_ref[...], 8)
    small_ref[...] = small_r
    krb = small_r.astype(BF16)
    bk_ref[...] = (jnp.dot(ckvb, wukk_ref[...], preferred_element_type=F32)
                   + jnp.dot(krb, pkr_ref[...], preferred_element_type=F32)).astype(BF16)
    bv_ref[...] = jnp.dot(ckvb, wukv_ref[...], preferred_element_type=F32).astype(BF16)

    for c in range(3):
        cqkv_ref[:, c * 512:(c + 1) * 512] = proj(w_cqkv_ref, c * 512, (c + 1) * 512)
    cz_ref[...] = proj(w_cz_ref, 0, 512)

    graw = -jnp.exp(alog_ref[...]) * _softplus(small + dtb_ref[...])
    lane = lax.broadcasted_iota(jnp.int32, small.shape, 1)
    gcol = jnp.where(lane < SM_A + C_HEADS, _chunk_scan(graw, True), _chunk_scan(graw, False))
    gcol = jnp.where((lane >= SM_A) & (lane < SM_B), gcol,
                     jnp.where((lane >= SM_B) & (lane < SM_B + 2 * C_HEADS), _sigmoid(small), 0.0))
    gcol_ref[...] = gcol
    grow_ref[...] = gcol.T[SM_A:SM_A + 16, :]


def _inproj(xs, mod, wts, tabs, layer):
    tile = lambda w: pl.BlockSpec((TM, w), lambda i: (i, 0))
    rope_spec = pl.BlockSpec((TM, LANES), lambda i: (_rope_block_of_tile(i), 0))
    res = lambda shape: _resident_layer(shape, layer)
    out_shapes = [
        jax.ShapeDtypeStruct((T_ALL, 512), BF16),
        jax.ShapeDtypeStruct((T_ALL, 128), F32),
        jax.ShapeDtypeStruct((T_ALL, 128), F32),
        jax.ShapeDtypeStruct((T_ALL, 1024), BF16),
        jax.ShapeDtypeStruct((T_ALL, 1024), BF16),
        jax.ShapeDtypeStruct((T_ALL, 512), BF16),
        jax.ShapeDtypeStruct((T_ALL, 256), F32),
        jax.ShapeDtypeStruct((T_ALL, 128), F32),
        jax.ShapeDtypeStruct((T_ALL, 1536), F32),
        jax.ShapeDtypeStruct((T_ALL, 128), F32),
        jax.ShapeDtypeStruct((16, T_ALL), F32),
        jax.ShapeDtypeStruct((T_ALL, 512), F32),
    ]
    out_specs = [tile(512), tile(128), tile(128), tile(1024), tile(1024), tile(512), tile(256), tile(128),
                 tile(1536), tile(128), pl.BlockSpec((16, TM), lambda i: (0, i)), tile(512)]
    in_specs = _stream_specs(D_MODEL, len(xs)) + [
        _mod_spec(layer),
        res((1, D_MODEL)),
        res((D_MODEL, W_HEAD_COLS)), res((D_MODEL, 1536)),
        pl.BlockSpec((None, D_MODEL, 512), lambda i: (layer, 0, 0), pipeline_mode=pl.Buffered(1)),
        rope_spec, rope_spec, rope_spec, rope_spec, rope_spec, rope_spec,
        res((1, LANES)), res((1, LANES)),
        res((1, B_Q_LORA)), res((1, B_KV_LORA)),
        res((B_Q_LORA, 1024)), res((B_KV_LORA, 1024)), _resident((LANES, 1024)),
        res((B_KV_LORA, 512)),
    ]
    return pl.pallas_call(
        functools.partial(_inproj_kernel, n_x=len(xs)),
        grid=(N_TILES,),
        in_specs=in_specs,
        out_specs=out_specs,
        out_shape=out_shapes,
        compiler_params=_cparams("arbitrary"),
        name="inproj",
    )(*xs, mod, wts["n1g"], wts["w_head"], wts["w_cqkv"], wts["w_tail"],
      tabs["ca"], tabs["sa"], tabs["cs"], tabs["ss"], tabs["cq"], tabs["sq"],
      wts["alog"], wts["dtb"], wts["gq"], wts["gkv"], wts["wuq"], wts["wukk"], wts["pkr"], wts["wukv"])


def _mla_cache_kernel(ckv_ref, kr_ref, wukk_ref, pkr_ref, wukv_ref, k_ref, v_ref):
    ckvb = ckv_ref[...].astype(BF16)
    krb = kr_ref[...].astype(BF16)
    k_ref[...] = (jnp.dot(ckvb, wukk_ref[...], preferred_element_type=F32)
                  + jnp.dot(krb, pkr_ref[...], preferred_element_type=F32)).astype(BF16)
    v_ref[...] = jnp.dot(ckvb, wukv_ref[...], preferred_element_type=F32).astype(BF16)


def _mla_cache(cache_ckv, cache_kr, wts, layer):
    n = DEC_BATCH * PAST_LEN
    res = lambda shape: _resident_layer(shape, layer)
    return pl.pallas_call(
        _mla_cache_kernel,
        grid=(DEC_BATCH,),
        in_specs=[pl.BlockSpec((None, None, PAST_LEN, B_KV_LORA), lambda i: (i, layer, 0, 0)),
                  pl.BlockSpec((None, None, PAST_LEN, LANES), lambda i: (i, layer, 0, 0)),
                  res((B_KV_LORA, 1024)), _resident((LANES, 1024)), res((B_KV_LORA, 512))],
        out_specs=[pl.BlockSpec((PAST_LEN, 1024), lambda i: (i, 0)),
                   pl.BlockSpec((PAST_LEN, 512), lambda i: (i, 0))],
        out_shape=[jax.ShapeDtypeStruct((n, 1024), BF16), jax.ShapeDtypeStruct((n, 512), BF16)],
        compiler_params=_cparams("arbitrary"),
        name="mla_cache",
    )(cache_ckv, cache_kr, wts["wukk"], wts["pkr"], wts["wukv"])


TQ = 256
A_WIN_KEYS = TQ + 2 * A_WINDOW
_NT = (((1,), (1,)), ((), ()))


def _softmax_pv(scores, values, extra_logit=None):
    m = scores[0].max(axis=-1, keepdims=True)
    for s in scores[1:]:
        m = jnp.maximum(m, s.max(axis=-1, keepdims=True))
    if extra_logit is not None:
        m = jnp.maximum(m, extra_logit)
    den = None
    acc = None
    for s, v in zip(scores, values):
        e = jnp.exp2(s - m)
        d = e.sum(axis=-1, keepdims=True)
        den = d if den is None else den + d
        pv = jnp.dot(e.astype(BF16), v, preferred_element_type=F32)
        acc = pv if acc is None else acc + pv
    if extra_logit is not None:
        den = den + jnp.exp2(extra_logit - m)
    return acc / den


def _attn_a_kernel(sink_ref, q_ref, k1_ref, v1_ref, *rest, latent, layer):
    if latent:
        k2_ref, v2_ref, o_ref = rest
        qt = pl.program_id(1)
        t0 = qt * TQ
        start = pl.multiple_of(jnp.clip(t0 - A_WINDOW, 0, DEC_SEQ - A_WIN_KEYS), A_WINDOW)
        k1 = k1_ref[pl.ds(start, A_WIN_KEYS), :]
        v1 = v1_ref[pl.ds(start, A_WIN_KEYS), :]
        qpos = t0 + lax.broadcasted_iota(jnp.int32, (TQ, A_WIN_KEYS), 0)
        kpos = start + lax.broadcasted_iota(jnp.int32, (TQ, A_WIN_KEYS), 1)
        valid = jnp.abs(qpos - kpos) <= A_WINDOW
    else:
        (o_ref,) = rest
        k1 = k1_ref[...]
        v1 = v1_ref[...]
    both = lambda x: (x.astype(BF16), pltpu.roll(x, A_HEAD_DIM, 1).astype(BF16))
    k1, v1 = both(k1), both(v1)
    if latent:
        k2, v2 = both(k2_ref[...]), both(v2_ref[...])
    lane = lax.broadcasted_iota(jnp.int32, (TQ, LANES), 1)
    low = lane < A_HEAD_DIM
    for c in range(A_HEADS // 2):
        qc = q_ref[:, c * LANES:(c + 1) * LANES]
        hk = (2 * c) // A_GROUP
        outs = []
        for p in range(2):
            sel = 0 if p == hk else 1
            qz = jnp.where(low if p == 0 else ~low, qc, jnp.zeros_like(qc))
            s1 = lax.dot_general(qz, k1[sel], _NT, preferred_element_type=F32)
            sink = sink_ref[layer, 2 * c + p] * LOG2E
            if latent:
                s1 = jnp.where(valid, s1, NEG_INF)
                s2 = lax.dot_general(qz, k2[sel], _NT, preferred_element_type=F32)
                outs.append(_softmax_pv([s1, s2], [v1[sel], v2[sel]], sink))
            else:
                outs.append(_softmax_pv([s1], [v1[sel]], sink))
        o_ref[:, c * LANES:(c + 1) * LANES] = jnp.where(low, outs[0], outs[1]).astype(BF16)


def _attn_a_ctx(sink, aq, ak, av, layer):
    nb = T_CTX // TQ
    return pl.pallas_call(
        functools.partial(_attn_a_kernel, latent=False, layer=layer),
        grid=(nb,),
        in_specs=[pl.BlockSpec(memory_space=pltpu.SMEM),
                  pl.BlockSpec((TQ, 512), lambda b: (b, 0)),
                  pl.BlockSpec((SEQ, LANES), lambda b: (b, 0)),
                  pl.BlockSpec((SEQ, LANES), lambda b: (b, 0))],
        out_specs=pl.BlockSpec((TQ, 512), lambda b: (b, 0)),
        out_shape=jax.ShapeDtypeStruct((T_CTX, 512), BF16),
        compiler_params=_cparams("arbitrary"),
        name="attn_a_ctx",
    )(sink, aq, ak, av)


def _attn_a_lat(sink, aq, ak, av, ck, cv, layer):
    nq = DEC_SEQ // TQ
    ctx_tiles = T_CTX // TQ
    seq_blocks = T_CTX // DEC_SEQ
    cache_spec = pl.BlockSpec((None, None, PAST_LEN, LANES), lambda b, i: (b, layer, 0, 0))
    return pl.pallas_call(
        functools.partial(_attn_a_kernel, latent=True, layer=layer),
        grid=(DEC_BATCH, nq),
        in_specs=[pl.BlockSpec(memory_space=pltpu.SMEM),
                  pl.BlockSpec((TQ, 512), lambda b, i: (ctx_tiles + b * nq + i, 0)),
                  pl.BlockSpec((DEC_SEQ, LANES), lambda b, i: (seq_blocks + b, 0)),
                  pl.BlockSpec((DEC_SEQ, LANES), lambda b, i: (seq_blocks + b, 0)),
                  cache_spec, cache_spec],
        out_specs=pl.BlockSpec((TQ, 512), lambda b, i: (b * nq + i, 0)),
        out_shape=jax.ShapeDtypeStruct((T_LAT, 512), BF16),
        compiler_params=_cparams("arbitrary", "arbitrary"),
        name="attn_a_lat",
    )(sink, aq, ak, av, ck, cv)


MLA_TQ_LAT = 512


def _mla_kernel(q_ref, k1_ref, v1_ref, *rest, latent):
    if latent:
        k2_ref, v2_ref, o_ref = rest
    else:
        (o_ref,) = rest
    lane = lax.broadcasted_iota(jnp.int32, (q_ref.shape[0], LANES), 1)
    low = lane < B_V
    for j in range(B_HEADS // 2):
        vs = slice((j // 2) * 2 * LANES, (j // 2 + 1) * 2 * LANES)
        half = slice((j % 2) * LANES, (j % 2 + 1) * LANES)
        outs = []
        for sub in range(2):
            hh = 2 * j + sub
            hs = slice(hh * LANES, (hh + 1) * LANES)
            qh = q_ref[:, hs]
            s1 = lax.dot_general(qh, k1_ref[:, hs], _NT, preferred_element_type=F32)
            if latent:
                s2 = lax.dot_general(qh, k2_ref[:, hs], _NT, preferred_element_type=F32)
                outs.append(_softmax_pv([s1, s2], [v1_ref[:, vs], v2_ref[:, vs]])[:, half])
            else:
                outs.append(_softmax_pv([s1], [v1_ref[:, vs]])[:, half])
        o_ref[:, j * LANES:(j + 1) * LANES] = jnp.where(low, outs[0], outs[1]).astype(BF16)


def _mla_ctx(bq, bk, bv):
    nb = T_CTX // TQ
    return pl.pallas_call(
        functools.partial(_mla_kernel, latent=False),
        grid=(nb,),
        in_specs=[pl.BlockSpec((TQ, 1024), lambda b: (b, 0)),
                  pl.BlockSpec((SEQ, 1024), lambda b: (b, 0)),
                  pl.BlockSpec((SEQ, 512), lambda b: (b, 0))],
        out_specs=pl.BlockSpec((TQ, 512), lambda b: (b, 0)),
        out_shape=jax.ShapeDtypeStruct((T_CTX, 512), BF16),
        compiler_params=_cparams("arbitrary"),
        name="mla_ctx",
    )(bq, bk, bv)


def _mla_lat(bq, bk, bv, k2, v2):
    tq = MLA_TQ_LAT
    nq = DEC_SEQ // tq
    ctx_tiles = T_CTX // tq
    seq_blocks = T_CTX // DEC_SEQ
    return pl.pallas_call(
        functools.partial(_mla_kernel, latent=True),
        grid=(DEC_BATCH, nq),
        in_specs=[pl.BlockSpec((tq, 1024), lambda b, i: (ctx_tiles + b * nq + i, 0)),
                  pl.BlockSpec((DEC_SEQ, 1024), lambda b, i: (seq_blocks + b, 0)),
                  pl.BlockSpec((DEC_SEQ, 512), lambda b, i: (seq_blocks + b, 0)),
                  pl.BlockSpec((PAST_LEN, 1024), lambda b, i: (b, 0)),
                  pl.BlockSpec((PAST_LEN, 512), lambda b, i: (b, 0))],
        out_specs=pl.BlockSpec((tq, 512), lambda b, i: (b * nq + i, 0)),
        out_shape=jax.ShapeDtypeStruct((T_LAT, 512), BF16),
        compiler_params=_cparams("arbitrary", "arbitrary"),
        name="mla_lat",
    )(bq, bk, bv, k2, v2)


def _gdn_kernel(xq_ref, xk_ref, xv_ref, wq_ref, wk_ref, wv_ref, gcol_ref, grow_ref, cz_ref, ong_ref, *rest,
                n, hps, has_s0, emit_state):
    rest = list(rest)
    s0_ref = rest.pop(0) if has_s0 else None
    oc_ref = rest.pop(0)
    sfin_ref = rest.pop(0) if emit_state else None
    q_s, k_s, v_s, u_s, w_s, qk_s, qd_s, kdt_s, ge_s, st_s, o_s = rest
    nc = n // GDN_CHUNK
    head0 = pl.program_id(1) * hps
    C = GDN_CHUNK
    width = hps * LANES
    hl = lambda hh: slice(hh * LANES, (hh + 1) * LANES)
    chains = [(d, hh) for d in range(2) for hh in range(hps)]

    row8 = lax.broadcasted_iota(jnp.int32, (8, width), 0)

    def conv_silu(x_ref, w_ref):
        x = x_ref[...]
        w = w_ref[...]
        y = pltpu.roll(x, 1, 0) * w[0:1] + x * w[1:2] + pltpu.roll(x, n - 1, 0) * w[2:3]
        top = y[0:8] - jnp.where(row8 == 0, x[n - 1:n] * w[0:1], 0.0)
        bot = y[n - 8:n] - jnp.where(row8 == 7, x[0:1] * w[2:3], 0.0)
        return _silu(jnp.concatenate([top, y[8:n - 8], bot], axis=0))

    def l2n(x):
        return x * lax.rsqrt(jnp.sum(x * x, axis=-1, keepdims=True) + EPS)

    qc = conv_silu(xq_ref, wq_ref)
    kc = conv_silu(xk_ref, wk_ref)
    for hh in range(hps):
        q_s[:, hl(hh)] = l2n(qc[:, hl(hh)]) * (C_DK ** -0.5)
        k_s[:, hl(hh)] = l2n(kc[:, hl(hh)])
    v_s[...] = conv_silu(xv_ref, wv_ref)
    o_s[...] = jnp.zeros_like(o_s)

    lane_c = lax.broadcasted_iota(jnp.int32, (C, LANES), 1)
    ri = lax.broadcasted_iota(jnp.int32, (C, C), 0)
    ci = lax.broadcasted_iota(jnp.int32, (C, C), 1)
    eye = (ri == ci).astype(F32)
    same_block = {}
    s = 2
    while s <= C:
        same_block[s] = (ri // s) == (ci // s)
        s *= 2

    def prep(it, carry):
        items = []
        for gi in range(group):
            r = it * group + gi
            rows = pl.ds(pl.multiple_of(r * C, C), C)
            gtile = gcol_ref[rows, :]
            for hh in range(hps):
                q = q_s[rows, hl(hh)]
                k = k_s[rows, hl(hh)]
                v = v_s[rows, hl(hh)]
                qk_raw = _dot_nt(q, k)
                for d in range(2):
                    slot = d * hps + hh
                    j = d * C_HEADS + head0 + hh
                    gc = jnp.sum(jnp.where(lane_c == SM_A + j, gtile, 0.0), axis=1, keepdims=True)
                    beta = jnp.sum(jnp.where(lane_c == SM_B + j, gtile, 0.0), axis=1, keepdims=True)
                    grow = grow_ref[j, r]
                    incl = (ri >= ci) if d == 0 else (ri <= ci)
                    strict = (ri > ci) if d == 0 else (ri < ci)
                    decay = jnp.where(incl, jnp.exp(jnp.where(incl, gc - grow, 0.0)), 0.0)
                    kb = k * beta
                    a = jnp.where(strict, _dot_nt(kb, k) * decay, 0.0)
                    g_end = gc[C - 1:C, :] if d == 0 else gc[0:1, :]
                    egc = jnp.exp(gc)
                    qk_s[slot, rows, :] = (qk_raw * decay).astype(BF16)
                    qd_s[slot, rows, :] = (q * egc).astype(BF16)
                    kdt_s[slot, rows, :] = (k * jnp.exp(g_end - gc)).T.astype(BF16)
                    ge_s[slot, r] = jnp.broadcast_to(jnp.exp(g_end), (8, LANES))
                    items.append((slot, rows, a, jnp.concatenate([v * beta, kb * egc], axis=1), d))
        ts = [eye - jnp.where(same_block[2], a, 0.0) for (_, _, a, _, _) in items]
        s = 2
        while s < C:
            level = same_block[2 * s] & ~same_block[s]
            es = [jnp.where(level, a, 0.0) for (_, _, a, _, _) in items]
            if s < 8:
                xs = [_dot(t, e) for t, e in zip(ts, es)]
                ts = [t - _dot(x, t) for t, x in zip(ts, xs)]
            else:
                def active(t, d):
                    return jnp.concatenate([t[b * s:(b + 1) * s] for b in range(1 - d, C // s, 2)], axis=0)

                def merged(t, upd, d):
                    return jnp.concatenate(
                        [upd[(b // 2) * s:(b // 2 + 1) * s] if b % 2 == 1 - d else t[b * s:(b + 1) * s]
                         for b in range(C // s)], axis=0)

                halves = [active(t, it_[4]) for t, it_ in zip(ts, items)]
                xs = [_dot(h, e) for h, e in zip(halves, es)]
                upds = [h - _dot(x, t) for h, x, t in zip(halves, xs, ts)]
                ts = [merged(t, u, it_[4]) for t, u, it_ in zip(ts, upds, items)]
            s *= 2
        for t, (slot, rows, _, rhs, _) in zip(ts, items):
            uw = _dot(t, rhs)
            u_s[slot, rows, :] = uw[:, :C_DV]
            w_s[slot, rows, :] = uw[:, C_DV:].astype(BF16)
        return carry

    group = max(1, min(nc, GDN_PREP_CHAINS // (2 * hps)))
    lax.fori_loop(0, nc // group, prep, 0)

    for d, hh in chains:
        if has_s0:
            st_s[d * hps + hh] = s0_ref[d, hh]
        else:
            st_s[d * hps + hh] = jnp.zeros((C_DK, C_DV), F32)

    def step(i, carry):
        rows = [pl.ds(pl.multiple_of(i * C, C), C), pl.ds(pl.multiple_of((nc - 1 - i) * C, C), C)]
        chunk = [i, nc - 1 - i]
        slots = [d * hps + hh for d, hh in chains]
        st = [st_s[c] for c in slots]
        stb = [s_.astype(BF16) for s_ in st]
        ws = [jnp.dot(w_s[c, rows[d], :], sb, preferred_element_type=F32)
              for c, (d, _), sb in zip(slots, chains, stb)]
        vnb = [(u_s[c, rows[d], :] - x).astype(BF16) for c, (d, _), x in zip(slots, chains, ws)]
        for c, (d, hh), sb, vb, s_ in zip(slots, chains, stb, vnb, st):
            o = (jnp.dot(qd_s[c, rows[d], :], sb, preferred_element_type=F32)
                 + jnp.dot(qk_s[c, rows[d], :], vb, preferred_element_type=F32))
            o_s[rows[d], hl(hh)] += o
            st_s[c] = s_ * ge_s[c, chunk[d]][0:1, :] + jnp.dot(kdt_s[c, rows[d], :], vb,
                                                                preferred_element_type=F32)
        return carry

    lax.fori_loop(0, nc, step, 0)
    if emit_state:
        for d, hh in chains:
            sfin_ref[0, d, hh] = st_s[d * hps + hh]

    for hh in range(hps):
        o = o_s[:, hl(hh)]
        oc_ref[:, hl(hh)] = (_rms(o) * ong_ref[...] * _silu(cz_ref[:, hl(hh)])).astype(BF16)


def _gdn(cqkv, conv_w, gcol, grow3, cz, ong, state_c, layer, *, latent):
    n = DEC_SEQ if latent else SEQ
    nseq = DEC_BATCH if latent else BATCH
    hps = GDN_HEADS_PER_STEP_LAT if latent else GDN_HEADS_PER_STEP_CTX
    seq0 = (T_CTX // n) if latent else 0
    nc = n // GDN_CHUNK
    width = hps * LANES
    sect = C_HEADS // hps
    col = lambda base: pl.BlockSpec((n, width), lambda b, h: (seq0 + b, base + h))
    wcol = lambda base: pl.BlockSpec((None, 3, width), lambda b, h: (layer, 0, base + h))
    in_specs = [col(0), col(sect), col(2 * sect), wcol(0), wcol(sect), wcol(2 * sect),
                pl.BlockSpec((n, LANES), lambda b, h: (seq0 + b, 0)),
                pl.BlockSpec((16, nc, 1, LANES), lambda b, h: (0, seq0 + b, 0, 0)),
                pl.BlockSpec((n, width), lambda b, h: (seq0 + b, h)),
                pl.BlockSpec((None, 1, LANES), lambda b, h: (layer, 0, 0))]
    args = [cqkv, cqkv, cqkv, conv_w, conv_w, conv_w, gcol, grow3, cz, ong]
    out_specs = [pl.BlockSpec((n, width), lambda b, h: (b, h))]
    out_shape = [jax.ShapeDtypeStruct((nseq * n, 512), BF16)]
    if latent:
        in_specs.append(pl.BlockSpec((None, None, 2, hps, C_DK, C_DV), lambda b, h: (b, layer, 0, h, 0, 0)))
        args.append(state_c)
    else:
        out_specs.append(pl.BlockSpec((1, 2, hps, C_DK, C_DV), lambda b, h: (b, 0, h, 0, 0)))
        out_shape.append(jax.ShapeDtypeStruct((nseq, 2, C_HEADS, C_DK, C_DV), F32))
    nch = 2 * hps
    scratch = [pltpu.VMEM((n, width), F32), pltpu.VMEM((n, width), F32), pltpu.VMEM((n, width), F32),
               pltpu.VMEM((nch, n, LANES), F32), pltpu.VMEM((nch, n, LANES), BF16),
               pltpu.VMEM((nch, n, LANES), BF16), pltpu.VMEM((nch, n, LANES), BF16),
               pltpu.VMEM((nch, n, LANES), BF16), pltpu.VMEM((nch, nc, 8, LANES), F32),
               pltpu.VMEM((nch, C_DK, C_DV), F32), pltpu.VMEM((n, width), F32)]
    res = pl.pallas_call(
        functools.partial(_gdn_kernel, n=n, hps=hps, has_s0=latent, emit_state=not latent),
        grid=(nseq, C_HEADS // hps),
        in_specs=in_specs,
        out_specs=out_specs,
        out_shape=out_shape,
        scratch_shapes=scratch,
        compiler_params=_cparams("arbitrary", "arbitrary"),
        name="gdn_lat" if latent else "gdn_ctx",
    )(*args)
    return res


def _merge_kernel(*refs, n_x):
    x_refs, refs = refs[:n_x], refs[n_x:]
    branch_refs, refs = refs[:6], refs[6:]
    mod_ref, n1g_ref, wtail_ref, wbr_ref, wout_ref, o_ref = refs
    mod = mod_ref[0]
    ys = [jnp.dot(_pick_stream(branch_refs[2 * j:2 * j + 2]), wbr_ref[j], preferred_element_type=F32)
          for j in range(3)]
    x = _pick_stream(x_refs)
    hb = (_rms(x) * n1g_ref[...] * (1.0 + mod[1:2]) + mod[0:1]).astype(BF16)
    z = None
    for j in range(3):
        cols = slice(512 + j * D_MODEL, 512 + (j + 1) * D_MODEL)
        gate = _sigmoid(jnp.dot(hb, wtail_ref[:, cols], preferred_element_type=F32))
        z = gate * ys[j] if z is None else z + gate * ys[j]
    o_ref[...] = x + mod[2:3] * jnp.dot(z.astype(BF16), wout_ref[...], preferred_element_type=F32)


def _merge(xs, branches, mod, wts, layer):
    res = lambda shape: _resident_layer(shape, layer)
    in_specs = _stream_specs(D_MODEL, len(xs))
    for _ in range(3):
        in_specs += _stream_specs(BRANCH_W, 2)
    in_specs += [_mod_spec(layer), res((1, D_MODEL)), res((D_MODEL, W_TAIL_COLS)),
                 res((3, BRANCH_W, D_MODEL)), res((D_MODEL, D_MODEL))]
    return pl.pallas_call(
        functools.partial(_merge_kernel, n_x=len(xs)),
        grid=(N_TILES,),
        in_specs=in_specs,
        out_specs=pl.BlockSpec((TM, D_MODEL), lambda i: (i, 0)),
        out_shape=jax.ShapeDtypeStruct((T_ALL, D_MODEL), F32),
        compiler_params=_cparams("arbitrary"),
        name="merge",
    )(*xs, *branches, mod, wts["n1g"], wts["w_tail"], wts["wbr"], wts["wout"])


FF_CHUNK = 1024


def _ffn_kernel(x_ref, mod_ref, n2g_ref, wff1_ref, wff2_ref, fg_ref, o_ref, *, final):
    mod = mod_ref[0]
    x = x_ref[...]
    hb = (_rms(x) * n2g_ref[...] * (1.0 + mod[4:5]) + mod[3:4]).astype(BF16)
    acc = None
    for c in range(D_FF // FF_CHUNK):
        cols = slice(c * FF_CHUNK, (c + 1) * FF_CHUNK)
        f = jnp.maximum(jnp.dot(hb, wff1_ref[:, cols], preferred_element_type=F32), 0.0)
        p = jnp.dot((f * f).astype(BF16), wff2_ref[cols, :], preferred_element_type=F32)
        acc = p if acc is None else acc + p
    x = x + mod[5:6] * acc
    if final:
        x = _rms(x) * fg_ref[...]
    o_ref[...] = x


def _ffn(x_all, mod, wts, final_g, layer, *, tile0, n_tiles, final):
    res = lambda shape: _resident_layer(shape, layer)
    return pl.pallas_call(
        functools.partial(_ffn_kernel, final=final),
        grid=(n_tiles,),
        in_specs=[pl.BlockSpec((TM, D_MODEL), lambda i: (tile0 + i, 0)), _mod_spec(layer, tile0),
                  res((1, D_MODEL)), res((D_MODEL, D_FF)), res((D_FF, D_MODEL)), _resident((1, D_MODEL))],
        out_specs=pl.BlockSpec((TM, D_MODEL), lambda i: (i, 0)),
        out_shape=jax.ShapeDtypeStruct((n_tiles * TM, D_MODEL), F32),
        compiler_params=_cparams("arbitrary"),
        name="ffn",
    )(x_all, mod, wts["n2g"], wts["wff1"], wts["wff2"], final_g)


def _rope_tables(r):
    half, quarter = r // 2, r // 4
    t = jnp.arange(DEC_SEQ)
    inv_freq = jnp.power(ROPE_BASE, -jnp.arange(quarter, dtype=F32) / quarter)
    cs, ss = [], []
    for pos in (t // GRID_W, t % GRID_W):
        ang = pos.astype(F32)[:, None] * inv_freq[None, :]
        c, s = jnp.cos(ang), jnp.sin(ang)
        cs += [c, c]
        ss += [-s, s]
    return jnp.concatenate(cs, axis=1), jnp.concatenate(ss, axis=1)


def _tables():
    def with_identity(c, s):
        one = jnp.ones((TM, LANES), F32)
        return jnp.concatenate([one, c], axis=0), jnp.concatenate([0.0 * one, s], axis=0)

    c64, s64 = _rope_tables(A_HEAD_DIM)
    ca, sa = with_identity(jnp.tile(c64, (1, 2)), jnp.tile(s64, (1, 2)))
    c32, s32 = _rope_tables(B_ROPE)
    ones = lambda w: jnp.ones((DEC_SEQ, w), F32)
    zeros = lambda w: jnp.zeros((DEC_SEQ, w), F32)
    cs, ss = with_identity(jnp.concatenate([c32, ones(96)], axis=1), jnp.concatenate([s32, zeros(96)], axis=1))
    cq, sq = with_identity(jnp.concatenate([ones(64), c32, ones(32)], axis=1),
                           jnp.concatenate([zeros(64), s32, zeros(32)], axis=1))
    return {"ca": ca, "sa": sa, "cs": cs, "ss": ss, "cq": cq, "sq": sq}


WIN_COLS = sum(W_IN_SPLITS)
WIN_COLS_PADDED = -(-WIN_COLS // LANES) * LANES
WIN_PREP_ROWS = 256
C_QKV_COL = 1440
C_AB_COL0 = 2944
C_TAIL_COL = 2992


def _prep_win_kernel(w_ref, head_ref, cqkv_ref, tail_ref):
    first = O_SMALL
    head_ref[:, 0:first] = w_ref[:, 0:first].astype(BF16)
    lane = lax.broadcasted_iota(jnp.int32, (WIN_PREP_ROWS, LANES), 1)
    kr = w_ref[:, first:first + LANES]
    ab = w_ref[:, C_AB_COL0:C_AB_COL0 + LANES]
    head_ref[:, first:W_HEAD_COLS] = jnp.where(lane < SM_A, kr, jnp.where(lane < SM_B + 8, ab, 0.0)).astype(BF16)

    def shifted(col, width):
        lo = col // LANES * LANES
        hi = -(-(col + width) // LANES) * LANES
        x = w_ref[:, lo:hi]
        return pltpu.roll(x, (hi - lo) - (col - lo), 1)[:, 0:width]

    cqkv_ref[...] = shifted(C_QKV_COL, 1536).astype(BF16)
    tail_ref[...] = shifted(C_TAIL_COL, W_TAIL_COLS).astype(BF16)


def _prep_win(w_in):
    assert (C_QKV_COL, C_AB_COL0 + SM_A, C_TAIL_COL) == (
        sum(W_IN_SPLITS[:6]), sum(W_IN_SPLITS[:7]), sum(W_IN_SPLITS[:9]))
    rows = WIN_PREP_ROWS
    out = lambda width: pl.BlockSpec((None, rows, width), lambda l, r: (l, r, 0))
    return pl.pallas_call(
        _prep_win_kernel,
        grid=(DEPTH, D_MODEL // rows),
        in_specs=[pl.BlockSpec((None, rows, WIN_COLS_PADDED), lambda l, r: (l, r, 0))],
        out_specs=[out(W_HEAD_COLS), out(1536), out(W_TAIL_COLS)],
        out_shape=[jax.ShapeDtypeStruct((DEPTH, D_MODEL, W_HEAD_COLS), BF16),
                   jax.ShapeDtypeStruct((DEPTH, D_MODEL, 1536), BF16),
                   jax.ShapeDtypeStruct((DEPTH, D_MODEL, W_TAIL_COLS), BF16)],
        compiler_params=_cparams("arbitrary", "arbitrary"),
        name="prep_win",
    )(w_in)


def _prep_weights(norm1_g, norm2_g, w_in, b_gq, b_w_uq, b_gkv, b_w_ukv, c_a_log, c_dt_bias, c_onorm_g,
                  w_branch, w_out, w_ff1, w_ff2):
    w_head, w_cqkv, w_tail = _prep_win(w_in)

    pad_lane = lambda v: jnp.zeros((DEPTH, 1, LANES), F32).at[:, 0, SM_A:SM_A + 8].set(v.reshape(DEPTH, 8))
    uq = b_w_uq.reshape(DEPTH, B_Q_LORA, B_HEADS, B_NOPE + B_ROPE)
    uq = jnp.pad(uq, ((0, 0), (0, 0), (0, 0), (0, LANES - B_NOPE - B_ROPE)))
    uq = uq.reshape(DEPTH, B_Q_LORA, 1024).astype(BF16)
    ukv = b_w_ukv.reshape(DEPTH, B_KV_LORA, B_HEADS, B_NOPE + B_V)
    ukk = jnp.pad(ukv[..., :B_NOPE], ((0, 0), (0, 0), (0, 0), (0, LANES - B_NOPE)))
    ukk = ukk.reshape(DEPTH, B_KV_LORA, 1024).astype(BF16)
    ukvv = ukv[..., B_NOPE:].reshape(DEPTH, B_KV_LORA, 512).astype(BF16)
    src = jnp.arange(LANES)[:, None]
    dst = jnp.arange(1024)[None, :]
    pkr = ((dst % LANES == B_NOPE + src) & (src < B_ROPE)).astype(BF16)

    return {"w_head": w_head, "w_cqkv": w_cqkv, "w_tail": w_tail,
            "alog": pad_lane(c_a_log), "dtb": pad_lane(c_dt_bias),
            "n1g": norm1_g.reshape(DEPTH, 1, D_MODEL), "n2g": norm2_g.reshape(DEPTH, 1, D_MODEL),
            "gq": b_gq.reshape(DEPTH, 1, B_Q_LORA), "gkv": b_gkv.reshape(DEPTH, 1, B_KV_LORA),
            "ong": c_onorm_g.reshape(DEPTH, 1, C_DV),
            "wuq": uq, "wukk": ukk, "pkr": pkr, "wukv": ukvv,
            "wbr": w_branch.astype(BF16), "wout": w_out.astype(BF16), "wff1": w_ff1.astype(BF16), "wff2": w_ff2.astype(BF16)}


def kernel(x_prompt, x_sample, cache_a_k, cache_a_v, cache_b_ckv, cache_b_krope, state_c, c, c_ctx, norm1_g,
           norm2_g, w_ada, b_ada, w_in, a_sink, b_gq, b_w_uq, b_gkv, b_w_ukv, c_conv_w, c_a_log, c_dt_bias,
           c_onorm_g, w_branch, w_out, w_ff1, w_ff2, final_g):
    xs = (x_prompt.reshape(T_CTX, D_MODEL), x_sample.reshape(T_LAT, D_MODEL))
    cond8 = jnp.concatenate([c_ctx[None, :], c, jnp.zeros((N_GROUPS - 1 - DEC_BATCH, D_MODEL), F32)], axis=0)
    mod = _adaln(cond8, w_ada, b_ada).reshape(DEPTH, N_GROUPS, 6, D_MODEL)
    tabs = _tables()
    fg = final_g.reshape(1, D_MODEL)
    wts = _prep_weights(norm1_g, norm2_g, w_in, b_gq, b_w_uq, b_gkv, b_w_ukv, c_a_log, c_dt_bias, c_onorm_g,
                        w_branch, w_out, w_ff1, w_ff2)
    ck = cache_a_k.reshape(DEC_BATCH, DEPTH, PAST_LEN, LANES)
    cv = cache_a_v.reshape(DEC_BATCH, DEPTH, PAST_LEN, LANES)
    cache_kr = jnp.pad(cache_b_krope, ((0, 0), (0, 0), (0, 0), (0, LANES - B_ROPE)))

    ak_l, av_l, ckv_l, kr_l, sc_l = [], [], [], [], []
    for l in range(DEPTH):
        (aq, ak, av, bq, bk, bv, ckv, small, cqkv, gcol, grow, cz) = _inproj(xs, mod, wts, tabs, l)

        oa_ctx = _attn_a_ctx(a_sink, aq, ak, av, l)
        oa_lat = _attn_a_lat(a_sink, aq, ak, av, ck, cv, l)

        k2, v2 = _mla_cache(cache_b_ckv, cache_kr, wts, l)
        ob_ctx = _mla_ctx(bq, bk, bv)
        ob_lat = _mla_lat(bq, bk, bv, k2, v2)

        grow3 = grow.reshape(16, T_ALL // GDN_CHUNK, 1, GDN_CHUNK)
        oc_ctx, s_fin = _gdn(cqkv, c_conv_w, gcol, grow3, cz, wts["ong"], state_c, l, latent=False)
        (oc_lat,) = _gdn(cqkv, c_conv_w, gcol, grow3, cz, wts["ong"], state_c, l, latent=True)

        branches = (oa_ctx, oa_lat, ob_ctx, ob_lat, oc_ctx, oc_lat)
        x_mid = _merge(xs, branches, mod, wts, l)
        ffn = functools.partial(_ffn, x_mid, mod, wts, fg, l)
        if l < DEPTH - 1:
            xs = (ffn(tile0=0, n_tiles=N_TILES, final=False),)
        else:
            xs = (ffn(tile0=0, n_tiles=N_CTX_TILES, final=True),
                  ffn(tile0=N_CTX_TILES, n_tiles=N_TILES - N_CTX_TILES, final=True))

        ak_l.append(ak[:T_CTX].reshape(BATCH, SEQ, A_KV_HEADS, A_HEAD_DIM))
        av_l.append(av[:T_CTX].reshape(BATCH, SEQ, A_KV_HEADS, A_HEAD_DIM))
        ckv_l.append(ckv[:T_CTX].reshape(BATCH, SEQ, B_KV_LORA))
        kr_l.append(small[:T_CTX, :B_ROPE].reshape(BATCH, SEQ, B_ROPE))
        sc_l.append(s_fin)

    y_prompt = xs[0].reshape(BATCH, SEQ, D_MODEL)
    y_sample = xs[1].reshape(DEC_BATCH, DEC_SEQ, D_MODEL)
    return (y_prompt, y_sample, jnp.stack(ak_l, axis=1), jnp.stack(av_l, axis=1), jnp.stack(ckv_l, axis=1),
            jnp.stack(kr_l, axis=1), jnp.stack(sc_l, axis=1))
```

```python
import functools
import math

import jax
import jax.numpy as jnp
import numpy as np
from jax import lax
from jax.experimental import pallas as pl
from jax.experimental.pallas import tpu as pltpu

F32 = jnp.float32
BF16 = jnp.bfloat16

D_MODEL = 1024
BATCH = 16
SEQ = 256
DEPTH = 2
DEC_BATCH = 4
DEC_SEQ = 2048
PAST_LEN = 256
GRID_W = 64
ROPE_BASE = 10000.0
EPS = 1e-6
NEG_INF = -1e30

A_HEADS = 8
A_KV_HEADS = 2
A_GROUP = 4
A_HEAD_DIM = 64
A_WINDOW = 128
A_SCALE = A_HEAD_DIM ** -0.5
B_HEADS = 8
B_Q_LORA = 384
B_KV_LORA = 256
B_NOPE = 64
B_ROPE = 32
B_V = 64
MLA_SCALE = (B_NOPE + B_ROPE) ** -0.5
LOG2E = math.log2(math.e)
C_HEADS = 4
C_DK = 128
C_DV = 128
D_FF = 4 * D_MODEL
BRANCH_W = 512

LANES = 128

T_CTX = BATCH * SEQ
T_LAT = DEC_BATCH * DEC_SEQ
T_ALL = T_CTX + T_LAT
TM = 512
N_CTX_TILES = T_CTX // TM
LAT_TILES_PER_SEQ = DEC_SEQ // TM
N_TILES = T_ALL // TM
N_GROUPS = 8
GDN_CHUNK = 128
GDN_PREP_CHAINS = 16
GDN_HEADS_PER_STEP_CTX = 4
GDN_HEADS_PER_STEP_LAT = 2

W_IN_SPLITS = (512, 128, 128, B_Q_LORA, B_KV_LORA, B_ROPE, 1536, 8, 8, 512, 3072)
O_AQ, O_AK, O_AV, O_BCQ, O_BCKV, O_SMALL, W_HEAD_COLS = 0, 512, 640, 768, 1152, 1408, 1536
W_TAIL_COLS = 512 + 3 * D_MODEL
SM_KR, SM_A, SM_B = 0, 32, 40

VMEM_LIMIT = 56 * 1024 * 1024


def _cparams(*sem):
    return pltpu.CompilerParams(dimension_semantics=sem, vmem_limit_bytes=VMEM_LIMIT)


def _resident(shape):
    nd = len(shape)
    return pl.BlockSpec(shape, lambda *_: (0,) * nd, pipeline_mode=pl.Buffered(1))


def _resident_layer(shape, layer):
    nd = len(shape)
    return pl.BlockSpec((None,) + tuple(shape), lambda *_: (layer,) + (0,) * nd, pipeline_mode=pl.Buffered(1))


def _group_of_tile(i):
    return jnp.where(i < N_CTX_TILES, 0, 1 + (i - N_CTX_TILES) // LAT_TILES_PER_SEQ)


def _rope_block_of_tile(i):
    return jnp.where(i < N_CTX_TILES, 0, 1 + (i - N_CTX_TILES) % LAT_TILES_PER_SEQ)


def _mod_spec(layer, tile0=0):
    return pl.BlockSpec((None, 1, 6, D_MODEL), lambda i: (layer, _group_of_tile(tile0 + i), 0, 0))


def _stream_specs(width, n_arrays, tile0=0):
    if n_arrays == 1:
        return [pl.BlockSpec((TM, width), lambda i: (tile0 + i, 0))]
    return [pl.BlockSpec((TM, width), lambda i: (jnp.minimum(tile0 + i, N_CTX_TILES - 1), 0)),
            pl.BlockSpec((TM, width), lambda i: (jnp.maximum(tile0 + i - N_CTX_TILES, 0), 0))]


def _pick_stream(refs, tile0=0):
    if len(refs) == 1:
        return refs[0][...]
    return jnp.where(tile0 + pl.program_id(0) < N_CTX_TILES, refs[0][...], refs[1][...])


def _rms(x):
    return x * lax.rsqrt(jnp.mean(x * x, axis=-1, keepdims=True) + EPS)


def _sigmoid(x):
    return 0.5 * jnp.tanh(0.5 * x) + 0.5


def _silu(x):
    h = 0.5 * x
    return h * jnp.tanh(h) + h


def _softplus(x):
    return jnp.maximum(x, 0.0) + jnp.log(1.0 + jnp.exp(-jnp.abs(x)))


def _dot(a, b):
    return jnp.dot(a.astype(BF16), b.astype(BF16), preferred_element_type=F32)


def _dot_nt(a, b):
    return lax.dot_general(a.astype(BF16), b.astype(BF16), (((1,), (1,)), ((), ())),
                           preferred_element_type=F32)


def _swap_groups(x, q):
    w = x.shape[1]
    lane = lax.broadcasted_iota(jnp.int32, x.shape, 1)
    even = ((lane // q) % 2) == 0
    return jnp.where(even, pltpu.roll(x, w - q, 1), pltpu.roll(x, q, 1))


def _rope(x, cos, sin, q):
    return x * cos + _swap_groups(x, q) * sin


ADA_TN = 1536


def _adaln_kernel(c_ref, w_ref, b_ref, o_ref):
    c = c_ref[...]
    o_ref[0] = jnp.dot(_silu(c), w_ref[0], preferred_element_type=F32) + b_ref[0]


def _adaln(cond8, w_ada, b_ada):
    n = 6 * D_MODEL
    return pl.pallas_call(
        _adaln_kernel,
        grid=(DEPTH, n // ADA_TN),
        in_specs=[
            pl.BlockSpec((N_GROUPS, D_MODEL), lambda l, j: (0, 0)),
            pl.BlockSpec((1, D_MODEL, ADA_TN), lambda l, j: (l, 0, j)),
            pl.BlockSpec((1, 1, ADA_TN), lambda l, j: (l, 0, j)),
        ],
        out_specs=pl.BlockSpec((1, N_GROUPS, ADA_TN), lambda l, j: (l, 0, j)),
        out_shape=jax.ShapeDtypeStruct((DEPTH, N_GROUPS, n), F32),
        compiler_params=_cparams("arbitrary", "arbitrary"),
        name="adaln",
    )(cond8, w_ada, b_ada.reshape(DEPTH, 1, n))


def _chunk_scan(g, forward):
    n = g.shape[0]
    pos = lax.broadcasted_iota(jnp.int32, g.shape, 0) % GDN_CHUNK
    s = 1
    while s < GDN_CHUNK:
        if forward:
            g = g + jnp.where(pos >= s, pltpu.roll(g, s, 0), 0.0)
        else:
            g = g + jnp.where(pos < GDN_CHUNK - s, pltpu.roll(g, n - s, 0), 0.0)
        s *= 2
    return g


def _inproj_kernel(*refs, n_x):
    x_refs, refs = refs[:n_x], refs[n_x:]
    (mod_ref, n1g_ref, w_head_ref, w_cqkv_ref, w_cz_ref,
     ca_ref, sa_ref, cs_ref, ss_ref, cq_ref, sq_ref,
     alog_ref, dtb_ref, gq_ref, gkv_ref, wuq_ref, wukk_ref, pkr_ref, wukv_ref,
     aq_ref, ak_ref, av_ref, bq_ref, bk_ref, bv_ref, ckv_ref, small_ref,
     cqkv_ref, gcol_ref, grow_ref, cz_ref) = refs
    mod = mod_ref[0]
    h = _rms(_pick_stream(x_refs)) * n1g_ref[...] * (1.0 + mod[1:2]) + mod[0:1]
    hb = h.astype(BF16)

    def proj(w_ref, lo, hi):
        return jnp.dot(hb, w_ref[:, lo:hi], preferred_element_type=F32)

    ca, sa = ca_ref[...], sa_ref[...]
    a_all = proj(w_head_ref, O_AQ, O_BCQ)
    for c in range(4):
        q = a_all[:, c * LANES:(c + 1) * LANES]
        aq_ref[:, c * LANES:(c + 1) * LANES] = (_rope(q, ca, sa, 16) * (A_SCALE * LOG2E)).astype(BF16)
    ak_ref[...] = _rope(a_all[:, O_AK:O_AV], ca, sa, 16)
    av_ref[...] = a_all[:, O_AV:O_BCQ]

    b_all = proj(w_head_ref, O_BCQ, W_HEAD_COLS)
    cq = (_rms(b_all[:, :B_Q_LORA]) * gq_ref[...]).astype(BF16)
    cqc, sqc = cq_ref[...], sq_ref[...]
    q_all = jnp.dot(cq, wuq_ref[...], preferred_element_type=F32)
    for hh in range(B_HEADS):
        qh = q_all[:, hh * LANES:(hh + 1) * LANES]
        bq_ref[:, hh * LANES:(hh + 1) * LANES] = (_rope(qh, cqc, sqc, 8) * (MLA_SCALE * LOG2E)).astype(BF16)

    ckv = _rms(b_all[:, B_Q_LORA:B_Q_LORA + B_KV_LORA]) * gkv_ref[...]
    ckv_ref[...] = ckv
    ckvb = ckv.astype(BF16)
    small = b_all[:, O_SMALL - O_BCQ:]
    small_r = _rope(small, cs_ref[...], ss_ref[...], 8)
    small_ref[...] = small_r
    krb = small_r.astype(BF16)
    bk_ref[...] = (jnp.dot(ckvb, wukk_ref[...], preferred_element_type=F32)
                   + jnp.dot(krb, pkr_ref[...], preferred_element_type=F32)).astype(BF16)
    bv_ref[...] = jnp.dot(ckvb, wukv_ref[...], preferred_element_type=F32).astype(BF16)

    for c in range(3):
        cqkv_ref[:, c * 512:(c + 1) * 512] = proj(w_cqkv_ref, c * 512, (c + 1) * 512)
    cz_ref[...] = proj(w_cz_ref, 0, 512)

    graw = -jnp.exp(alog_ref[...]) * _softplus(small + dtb_ref[...])
    lane = lax.broadcasted_iota(jnp.int32, small.shape, 1)
    gcol = jnp.where(lane < SM_A + C_HEADS, _chunk_scan(graw, True), _chunk_scan(graw, False))
    gcol = jnp.where((lane >= SM_A) & (lane < SM_B), gcol,
                     jnp.where((lane >= SM_B) & (lane < SM_B + 2 * C_HEADS), _sigmoid(small), 0.0))
    gcol_ref[...] = gcol
    grow_ref[...] = gcol.T[SM_A:SM_A + 16, :]


def _inproj(xs, mod, wts, tabs, layer):
    tile = lambda w: pl.BlockSpec((TM, w), lambda i: (i, 0))
    rope_spec = pl.BlockSpec((TM, LANES), lambda i: (_rope_block_of_tile(i), 0))
    res = lambda shape: _resident_layer(shape, layer)
    out_shapes = [
        jax.ShapeDtypeStruct((T_ALL, 512), BF16),
        jax.ShapeDtypeStruct((T_ALL, 128), F32),
        jax.ShapeDtypeStruct((T_ALL, 128), F32),
        jax.ShapeDtypeStruct((T_ALL, 1024), BF16),
        jax.ShapeDtypeStruct((T_ALL, 1024), BF16),
        jax.ShapeDtypeStruct((T_ALL, 512), BF16),
        jax.ShapeDtypeStruct((T_ALL, 256), F32),
        jax.ShapeDtypeStruct((T_ALL, 128), F32),
        jax.ShapeDtypeStruct((T_ALL, 1536), F32),
        jax.ShapeDtypeStruct((T_ALL, 128), F32),
        jax.ShapeDtypeStruct((16, T_ALL), F32),
        jax.ShapeDtypeStruct((T_ALL, 512), F32),
    ]
    out_specs = [tile(512), tile(128), tile(128), tile(1024), tile(1024), tile(512), tile(256), tile(128),
                 tile(1536), tile(128), pl.BlockSpec((16, TM), lambda i: (0, i)), tile(512)]
    in_specs = _stream_specs(D_MODEL, len(xs)) + [
        _mod_spec(layer),
        res((1, D_MODEL)),
        res((D_MODEL, W_HEAD_COLS)), res((D_MODEL, 1536)),
        pl.BlockSpec((None, D_MODEL, 512), lambda i: (layer, 0, 0), pipeline_mode=pl.Buffered(1)),
        rope_spec, rope_spec, rope_spec, rope_spec, rope_spec, rope_spec,
        res((1, LANES)), res((1, LANES)),
        res((1, B_Q_LORA)), res((1, B_KV_LORA)),
        res((B_Q_LORA, 1024)), res((B_KV_LORA, 1024)), _resident((LANES, 1024)),
        res((B_KV_LORA, 512)),
    ]
    return pl.pallas_call(
        functools.partial(_inproj_kernel, n_x=len(xs)),
        grid=(N_TILES,),
        in_specs=in_specs,
        out_specs=out_specs,
        out_shape=out_shapes,
        compiler_params=_cparams("arbitrary"),
        name="inproj",
    )(*xs, mod, wts["n1g"], wts["w_head"], wts["w_cqkv"], wts["w_tail"],
      tabs["ca"], tabs["sa"], tabs["cs"], tabs["ss"], tabs["cq"], tabs["sq"],
      wts["alog"], wts["dtb"], wts["gq"], wts["gkv"], wts["wuq"], wts["wukk"], wts["pkr"], wts["wukv"])


def _mla_cache_kernel(ckv_ref, kr_ref, wukk_ref, pkr_ref, wukv_ref, k_ref, v_ref):
    ckvb = ckv_ref[...].astype(BF16)
    krb = kr_ref[...].astype(BF16)
    k_ref[...] = (jnp.dot(ckvb, wukk_ref[...], preferred_element_type=F32)
                  + jnp.dot(krb, pkr_ref[...], preferred_element_type=F32)).astype(BF16)
    v_ref[...] = jnp.dot(ckvb, wukv_ref[...], preferred_element_type=F32).astype(BF16)


def _mla_cache(cache_ckv, cache_kr, wts, layer):
    n = DEC_BATCH * PAST_LEN
    res = lambda shape: _resident_layer(shape, layer)
    return pl.pallas_call(
        _mla_cache_kernel,
        grid=(DEC_BATCH,),
        in_specs=[pl.BlockSpec((None, None, PAST_LEN, B_KV_LORA), lambda i: (i, layer, 0, 0)),
                  pl.BlockSpec((None, None, PAST_LEN, LANES), lambda i: (i, layer, 0, 0)),
                  res((B_KV_LORA, 1024)), _resident((LANES, 1024)), res((B_KV_LORA, 512))],
        out_specs=[pl.BlockSpec((PAST_LEN, 1024), lambda i: (i, 0)),
                   pl.BlockSpec((PAST_LEN, 512), lambda i: (i, 0))],
        out_shape=[jax.ShapeDtypeStruct((n, 1024), BF16), jax.ShapeDtypeStruct((n, 512), BF16)],
        compiler_params=_cparams("arbitrary"),
        name="mla_cache",
    )(cache_ckv, cache_kr, wts["wukk"], wts["pkr"], wts["wukv"])


TQ = 256
A_WIN_KEYS = TQ + 2 * A_WINDOW
_NT = (((1,), (1,)), ((), ()))


def _softmax_pv(scores, values, extra_logit=None):
    m = scores[0].max(axis=-1, keepdims=True)
    for s in scores[1:]:
        m = jnp.maximum(m, s.max(axis=-1, keepdims=True))
    if extra_logit is not None:
        m = jnp.maximum(m, extra_logit)
    den = None
    acc = None
    for s, v in zip(scores, values):
        e = jnp.exp2(s - m)
        d = e.sum(axis=-1, keepdims=True)
        den = d if den is None else den + d
        pv = jnp.dot(e.astype(BF16), v, preferred_element_type=F32)
        acc = pv if acc is None else acc + pv
    if extra_logit is not None:
        den = den + jnp.exp2(extra_logit - m)
    return acc / den


def _attn_a_kernel(sink_ref, q_ref, k1_ref, v1_ref, *rest, latent, layer):
    if latent:
        k2_ref, v2_ref, o_ref = rest
        qt = pl.program_id(1)
        t0 = qt * TQ
        start = pl.multiple_of(jnp.clip(t0 - A_WINDOW, 0, DEC_SEQ - A_WIN_KEYS), A_WINDOW)
        k1 = k1_ref[pl.ds(start, A_WIN_KEYS), :]
        v1 = v1_ref[pl.ds(start, A_WIN_KEYS), :]
        qpos = t0 + lax.broadcasted_iota(jnp.int32, (TQ, A_WIN_KEYS), 0)
        kpos = start + lax.broadcasted_iota(jnp.int32, (TQ, A_WIN_KEYS), 1)
        valid = jnp.abs(qpos - kpos) <= A_WINDOW
    else:
        (o_ref,) = rest
        k1 = k1_ref[...]
        v1 = v1_ref[...]
    both = lambda x: (x.astype(BF16), pltpu.roll(x, A_HEAD_DIM, 1).astype(BF16))
    k1, v1 = both(k1), both(v1)
    if latent:
        k2, v2 = both(k2_ref[...]), both(v2_ref[...])
    lane = lax.broadcasted_iota(jnp.int32, (TQ, LANES), 1)
    low = lane < A_HEAD_DIM
    top = lax.broadcasted_iota(jnp.int32, (2 * TQ, 1), 0) < TQ
    if latent:
        valid2 = jnp.concatenate([valid, valid], axis=0)
    for hk in range(A_KV_HEADS):
        chunks = (2 * hk, 2 * hk + 1)
        qcs = [q_ref[:, c * LANES:(c + 1) * LANES] for c in chunks]
        outs = []
        for p in range(2):
            sel = 0 if p == hk else 1
            keep = low if p == 0 else ~low
            qz = jnp.concatenate([jnp.where(keep, qc, jnp.zeros_like(qc)) for qc in qcs], axis=0)
            s1 = lax.dot_general(qz, k1[sel], _NT, preferred_element_type=F32)
            sink = jnp.where(top, sink_ref[layer, 2 * chunks[0] + p], sink_ref[layer, 2 * chunks[1] + p]) * LOG2E
            if latent:
                s1 = jnp.where(valid2, s1, NEG_INF)
                s2 = lax.dot_general(qz, k2[sel], _NT, preferred_element_type=F32)
                outs.append(_softmax_pv([s1, s2], [v1[sel], v2[sel]], sink))
            else:
                outs.append(_softmax_pv([s1], [v1[sel]], sink))
        for i, c in enumerate(chunks):
            rows = slice(i * TQ, (i + 1) * TQ)
            o_ref[:, c * LANES:(c + 1) * LANES] = jnp.where(low, outs[0][rows], outs[1][rows]).astype(BF16)


def _attn_a_ctx(sink, aq, ak, av, layer):
    nb = T_CTX // TQ
    return pl.pallas_call(
        functools.partial(_attn_a_kernel, latent=False, layer=layer),
        grid=(nb,),
        in_specs=[pl.BlockSpec(memory_space=pltpu.SMEM),
                  pl.BlockSpec((TQ, 512), lambda b: (b, 0)),
                  pl.BlockSpec((SEQ, LANES), lambda b: (b, 0)),
                  pl.BlockSpec((SEQ, LANES), lambda b: (b, 0))],
        out_specs=pl.BlockSpec((TQ, 512), lambda b: (b, 0)),
        out_shape=jax.ShapeDtypeStruct((T_CTX, 512), BF16),
        compiler_params=_cparams("arbitrary"),
        name="attn_a_ctx",
    )(sink, aq, ak, av)


def _attn_a_lat(sink, aq, ak, av, ck, cv, layer):
    nq = DEC_SEQ // TQ
    ctx_tiles = T_CTX // TQ
    seq_blocks = T_CTX // DEC_SEQ
    cache_spec = pl.BlockSpec((None, None, PAST_LEN, LANES), lambda b, i: (b, layer, 0, 0))
    return pl.pallas_call(
        functools.partial(_attn_a_kernel, latent=True, layer=layer),
        grid=(DEC_BATCH, nq),
        in_specs=[pl.BlockSpec(memory_space=pltpu.SMEM),
                  pl.BlockSpec((TQ, 512), lambda b, i: (ctx_tiles + b * nq + i, 0)),
                  pl.BlockSpec((DEC_SEQ, LANES), lambda b, i: (seq_blocks + b, 0)),
                  pl.BlockSpec((DEC_SEQ, LANES), lambda b, i: (seq_blocks + b, 0)),
                  cache_spec, cache_spec],
        out_specs=pl.BlockSpec((TQ, 512), lambda b, i: (b * nq + i, 0)),
        out_shape=jax.ShapeDtypeStruct((T_LAT, 512), BF16),
        compiler_params=_cparams("arbitrary", "arbitrary"),
        name="attn_a_lat",
    )(sink, aq, ak, av, ck, cv)


MLA_TQ_LAT = 512


def _mla_kernel(q_ref, k1_ref, v1_ref, *rest, latent):
    if latent:
        k2_ref, v2_ref, o_ref = rest
    else:
        (o_ref,) = rest
    lane = lax.broadcasted_iota(jnp.int32, (q_ref.shape[0], LANES), 1)
    low = lane < B_V
    for j in range(B_HEADS // 2):
        vs = slice((j // 2) * 2 * LANES, (j // 2 + 1) * 2 * LANES)
        half = slice((j % 2) * LANES, (j % 2 + 1) * LANES)
        outs = []
        for sub in range(2):
            hh = 2 * j + sub
            hs = slice(hh * LANES, (hh + 1) * LANES)
            qh = q_ref[:, hs]
            s1 = lax.dot_general(qh, k1_ref[:, hs], _NT, preferred_element_type=F32)
            if latent:
                s2 = lax.dot_general(qh, k2_ref[:, hs], _NT, preferred_element_type=F32)
                outs.append(_softmax_pv([s1, s2], [v1_ref[:, vs], v2_ref[:, vs]])[:, half])
            else:
                outs.append(_softmax_pv([s1], [v1_ref[:, vs]])[:, half])
        o_ref[:, j * LANES:(j + 1) * LANES] = jnp.where(low, outs[0], outs[1]).astype(BF16)


def _mla_ctx(bq, bk, bv):
    nb = T_CTX // TQ
    return pl.pallas_call(
        functools.partial(_mla_kernel, latent=False),
        grid=(nb,),
        in_specs=[pl.BlockSpec((TQ, 1024), lambda b: (b, 0)),
                  pl.BlockSpec((SEQ, 1024), lambda b: (b, 0)),
                  pl.BlockSpec((SEQ, 512), lambda b: (b, 0))],
        out_specs=pl.BlockSpec((TQ, 512), lambda b: (b, 0)),
        out_shape=jax.ShapeDtypeStruct((T_CTX, 512), BF16),
        compiler_params=_cparams("arbitrary"),
        name="mla_ctx",
    )(bq, bk, bv)


def _mla_lat(bq, bk, bv, k2, v2):
    tq = MLA_TQ_LAT
    nq = DEC_SEQ // tq
    ctx_tiles = T_CTX // tq
    seq_blocks = T_CTX // DEC_SEQ
    return pl.pallas_call(
        functools.partial(_mla_kernel, latent=True),
        grid=(DEC_BATCH, nq),
        in_specs=[pl.BlockSpec((tq, 1024), lambda b, i: (ctx_tiles + b * nq + i, 0)),
                  pl.BlockSpec((DEC_SEQ, 1024), lambda b, i: (seq_blocks + b, 0)),
                  pl.BlockSpec((DEC_SEQ, 512), lambda b, i: (seq_blocks + b, 0)),
                  pl.BlockSpec((PAST_LEN, 1024), lambda b, i: (b, 0)),
                  pl.BlockSpec((PAST_LEN, 512), lambda b, i: (b, 0))],
        out_specs=pl.BlockSpec((tq, 512), lambda b, i: (b * nq + i, 0)),
        out_shape=jax.ShapeDtypeStruct((T_LAT, 512), BF16),
        compiler_params=_cparams("arbitrary", "arbitrary"),
        name="mla_lat",
    )(bq, bk, bv, k2, v2)


def _gdn_kernel(xq_ref, xk_ref, xv_ref, wq_ref, wk_ref, wv_ref, gcol_ref, grow_ref, cz_ref, ong_ref, *rest,
                n, hps, has_s0, emit_state):
    rest = list(rest)
    s0_ref = rest.pop(0) if has_s0 else None
    oc_ref = rest.pop(0)
    sfin_ref = rest.pop(0) if emit_state else None
    q_s, k_s, v_s, u_s, w_s, qk_s, qd_s, kdt_s, ge_s, st_s, o_s = rest
    nc = n // GDN_CHUNK
    head0 = pl.program_id(1) * hps
    C = GDN_CHUNK
    width = hps * LANES
    hl = lambda hh: slice(hh * LANES, (hh + 1) * LANES)
    chains = [(d, hh) for d in range(2) for hh in range(hps)]

    row8 = lax.broadcasted_iota(jnp.int32, (8, width), 0)

    def conv_silu(x_ref, w_ref):
        x = x_ref[...]
        w = w_ref[...]
        y = pltpu.roll(x, 1, 0) * w[0:1] + x * w[1:2] + pltpu.roll(x, n - 1, 0) * w[2:3]
        top = y[0:8] - jnp.where(row8 == 0, x[n - 1:n] * w[0:1], 0.0)
        bot = y[n - 8:n] - jnp.where(row8 == 7, x[0:1] * w[2:3], 0.0)
        return _silu(jnp.concatenate([top, y[8:n - 8], bot], axis=0))

    def l2n(x):
        return x * lax.rsqrt(jnp.sum(x * x, axis=-1, keepdims=True) + EPS)

    qc = conv_silu(xq_ref, wq_ref)
    kc = conv_silu(xk_ref, wk_ref)
    for hh in range(hps):
        q_s[:, hl(hh)] = l2n(qc[:, hl(hh)]) * (C_DK ** -0.5)
        k_s[:, hl(hh)] = l2n(kc[:, hl(hh)])
    v_s[...] = conv_silu(xv_ref, wv_ref)
    o_s[...] = jnp.zeros_like(o_s)

    lane_c = lax.broadcasted_iota(jnp.int32, (C, LANES), 1)
    ri = lax.broadcasted_iota(jnp.int32, (C, C), 0)
    ci = lax.broadcasted_iota(jnp.int32, (C, C), 1)
    eye = (ri == ci).astype(F32)
    same_block = {}
    s = 2
    while s <= C:
        same_block[s] = (ri // s) == (ci // s)
        s *= 2

    def prep(it, carry):
        items = []
        for gi in range(group):
            r = it * group + gi
            rows = pl.ds(pl.multiple_of(r * C, C), C)
            gtile = gcol_ref[rows, :]
            for hh in range(hps):
                q = q_s[rows, hl(hh)]
                k = k_s[rows, hl(hh)]
                v = v_s[rows, hl(hh)]
                qk_raw = _dot_nt(q, k)
                for d in range(2):
                    slot = d * hps + hh
                    j = d * C_HEADS + head0 + hh
                    gc = jnp.sum(jnp.where(lane_c == SM_A + j, gtile, 0.0), axis=1, keepdims=True)
                    beta = jnp.sum(jnp.where(lane_c == SM_B + j, gtile, 0.0), axis=1, keepdims=True)
                    grow = grow_ref[j, r]
                    incl = (ri >= ci) if d == 0 else (ri <= ci)
                    strict = (ri > ci) if d == 0 else (ri < ci)
                    decay = jnp.where(incl, jnp.exp(jnp.where(incl, gc - grow, 0.0)), 0.0)
                    kb = k * beta
                    a = jnp.where(strict, _dot_nt(kb, k) * decay, 0.0)
                    g_end = gc[C - 1:C, :] if d == 0 else gc[0:1, :]
                    egc = jnp.exp(gc)
                    qk_s[slot, rows, :] = (qk_raw * decay).astype(BF16)
                    qd_s[slot, rows, :] = (q * egc).astype(BF16)
                    kdt_s[slot, rows, :] = (k * jnp.exp(g_end - gc)).T.astype(BF16)
                    ge_s[slot, r] = jnp.broadcast_to(jnp.exp(g_end), (8, LANES))
                    items.append((slot, rows, a, jnp.concatenate([v * beta, kb * egc], axis=1), d))
        ts = [eye - jnp.where(same_block[2], a, 0.0) for (_, _, a, _, _) in items]
        s = 2
        while s < C:
            level = same_block[2 * s] & ~same_block[s]
            es = [jnp.where(level, a, 0.0) for (_, _, a, _, _) in items]
            if s < 8:
                xs = [_dot(t, e) for t, e in zip(ts, es)]
                ts = [t - _dot(x, t) for t, x in zip(ts, xs)]
            else:
                def active(t, d):
                    return jnp.concatenate([t[b * s:(b + 1) * s] for b in range(1 - d, C // s, 2)], axis=0)

                def merged(t, upd, d):
                    return jnp.concatenate(
                        [upd[(b // 2) * s:(b // 2 + 1) * s] if b % 2 == 1 - d else t[b * s:(b + 1) * s]
                         for b in range(C // s)], axis=0)

                halves = [active(t, it_[4]) for t, it_ in zip(ts, items)]
                xs = [_dot(h, e) for h, e in zip(halves, es)]
                upds = [h - _dot(x, t) for h, x, t in zip(halves, xs, ts)]
                ts = [merged(t, u, it_[4]) for t, u, it_ in zip(ts, upds, items)]
            s *= 2
        for t, (slot, rows, _, rhs, _) in zip(ts, items):
            uw = _dot(t, rhs)
            u_s[slot, rows, :] = uw[:, :C_DV]
            w_s[slot, rows, :] = uw[:, C_DV:].astype(BF16)
        return carry

    group = max(1, min(nc, GDN_PREP_CHAINS // (2 * hps)))
    lax.fori_loop(0, nc // group, prep, 0)

    for d, hh in chains:
        if has_s0:
            st_s[d * hps + hh] = s0_ref[d, hh]
        else:
            st_s[d * hps + hh] = jnp.zeros((C_DK, C_DV), F32)

    def step(i, carry):
        rows = [pl.ds(pl.multiple_of(i * C, C), C), pl.ds(pl.multiple_of((nc - 1 - i) * C, C), C)]
        chunk = [i, nc - 1 - i]
        slots = [d * hps + hh for d, hh in chains]
        st = [st_s[c] for c in slots]
        stb = [s_.astype(BF16) for s_ in st]
        ws = [jnp.dot(w_s[c, rows[d], :], sb, preferred_element_type=F32)
              for c, (d, _), sb in zip(slots, chains, stb)]
        vnb = [(u_s[c, rows[d], :] - x).astype(BF16) for c, (d, _), x in zip(slots, chains, ws)]
        for c, (d, hh), sb, vb, s_ in zip(slots, chains, stb, vnb, st):
            o = (jnp.dot(qd_s[c, rows[d], :], sb, preferred_element_type=F32)
                 + jnp.dot(qk_s[c, rows[d], :], vb, preferred_element_type=F32))
            o_s[rows[d], hl(hh)] += o
            st_s[c] = s_ * ge_s[c, chunk[d]][0:1, :] + jnp.dot(kdt_s[c, rows[d], :], vb,
                                                                preferred_element_type=F32)
        return carry

    lax.fori_loop(0, nc, step, 0)
    if emit_state:
        for d, hh in chains:
            sfin_ref[0, d, hh] = st_s[d * hps + hh]

    for hh in range(hps):
        o = o_s[:, hl(hh)]
        oc_ref[:, hl(hh)] = (_rms(o) * ong_ref[...] * _silu(cz_ref[:, hl(hh)])).astype(BF16)


def _gdn(cqkv, conv_w, gcol, grow3, cz, ong, state_c, layer, *, latent):
    n = DEC_SEQ if latent else SEQ
    nseq = DEC_BATCH if latent else BATCH
    hps = GDN_HEADS_PER_STEP_LAT if latent else GDN_HEADS_PER_STEP_CTX
    seq0 = (T_CTX // n) if latent else 0
    nc = n // GDN_CHUNK
    width = hps * LANES
    sect = C_HEADS // hps
    col = lambda base: pl.BlockSpec((n, width), lambda b, h: (seq0 + b, base + h))
    wcol = lambda base: pl.BlockSpec((None, 3, width), lambda b, h: (layer, 0, base + h))
    in_specs = [col(0), col(sect), col(2 * sect), wcol(0), wcol(sect), wcol(2 * sect),
                pl.BlockSpec((n, LANES), lambda b, h: (seq0 + b, 0)),
                pl.BlockSpec((16, nc, 1, LANES), lambda b, h: (0, seq0 + b, 0, 0)),
                pl.BlockSpec((n, width), lambda b, h: (seq0 + b, h)),
                pl.BlockSpec((None, 1, LANES), lambda b, h: (layer, 0, 0))]
    args = [cqkv, cqkv, cqkv, conv_w, conv_w, conv_w, gcol, grow3, cz, ong]
    out_specs = [pl.BlockSpec((n, width), lambda b, h: (b, h))]
    out_shape = [jax.ShapeDtypeStruct((nseq * n, 512), BF16)]
    if latent:
        in_specs.append(pl.BlockSpec((None, None, 2, hps, C_DK, C_DV), lambda b, h: (b, layer, 0, h, 0, 0)))
        args.append(state_c)
    else:
        out_specs.append(pl.BlockSpec((1, 2, hps, C_DK, C_DV), lambda b, h: (b, 0, h, 0, 0)))
        out_shape.append(jax.ShapeDtypeStruct((nseq, 2, C_HEADS, C_DK, C_DV), F32))
    nch = 2 * hps
    scratch = [pltpu.VMEM((n, width), F32), pltpu.VMEM((n, width), F32), pltpu.VMEM((n, width), F32),
               pltpu.VMEM((nch, n, LANES), F32), pltpu.VMEM((nch, n, LANES), BF16),
               pltpu.VMEM((nch, n, LANES), BF16), pltpu.VMEM((nch, n, LANES), BF16),
               pltpu.VMEM((nch, n, LANES), BF16), pltpu.VMEM((nch, nc, 8, LANES), F32),
               pltpu.VMEM((nch, C_DK, C_DV), F32), pltpu.VMEM((n, width), F32)]
    res = pl.pallas_call(
        functools.partial(_gdn_kernel, n=n, hps=hps, has_s0=latent, emit_state=not latent),
        grid=(nseq, C_HEADS // hps),
        in_specs=in_specs,
        out_specs=out_specs,
        out_shape=out_shape,
        scratch_shapes=scratch,
        compiler_params=_cparams("arbitrary", "arbitrary"),
        name="gdn_lat" if latent else "gdn_ctx",
    )(*args)
    return res


def _merge_kernel(*refs, n_x):
    x_refs, refs = refs[:n_x], refs[n_x:]
    branch_refs, refs = refs[:6], refs[6:]
    mod_ref, n1g_ref, wtail_ref, wbr_ref, wout_ref, o_ref = refs
    mod = mod_ref[0]
    ys = [jnp.dot(_pick_stream(branch_refs[2 * j:2 * j + 2]), wbr_ref[j], preferred_element_type=F32)
          for j in range(3)]
    x = _pick_stream(x_refs)
    hb = (_rms(x) * n1g_ref[...] * (1.0 + mod[1:2]) + mod[0:1]).astype(BF16)
    z = None
    for j in range(3):
        cols = slice(512 + j * D_MODEL, 512 + (j + 1) * D_MODEL)
        gate = _sigmoid(jnp.dot(hb, wtail_ref[:, cols], preferred_element_type=F32))
        z = gate * ys[j] if z is None else z + gate * ys[j]
    o_ref[...] = x + mod[2:3] * jnp.dot(z.astype(BF16), wout_ref[...], preferred_element_type=F32)


def _merge(xs, branches, mod, wts, layer):
    res = lambda shape: _resident_layer(shape, layer)
    in_specs = _stream_specs(D_MODEL, len(xs))
    for _ in range(3):
        in_specs += _stream_specs(BRANCH_W, 2)
    in_specs += [_mod_spec(layer), res((1, D_MODEL)), res((D_MODEL, W_TAIL_COLS)),
                 res((3, BRANCH_W, D_MODEL)), res((D_MODEL, D_MODEL))]
    return pl.pallas_call(
        functools.partial(_merge_kernel, n_x=len(xs)),
        grid=(N_TILES,),
        in_specs=in_specs,
        out_specs=pl.BlockSpec((TM, D_MODEL), lambda i: (i, 0)),
        out_shape=jax.ShapeDtypeStruct((T_ALL, D_MODEL), F32),
        compiler_params=_cparams("arbitrary"),
        name="merge",
    )(*xs, *branches, mod, wts["n1g"], wts["w_tail"], wts["wbr"], wts["wout"])


FF_CHUNK = 1024


def _ffn_kernel(x_ref, mod_ref, n2g_ref, wff1_ref, wff2_ref, fg_ref, o_ref, *, final):
    mod = mod_ref[0]
    x = x_ref[...]
    hb = (_rms(x) * n2g_ref[...] * (1.0 + mod[4:5]) + mod[3:4]).astype(BF16)
    acc = None
    for c in range(D_FF // FF_CHUNK):
        cols = slice(c * FF_CHUNK, (c + 1) * FF_CHUNK)
        f = jnp.maximum(jnp.dot(hb, wff1_ref[:, cols], preferred_element_type=F32), 0.0)
        p = jnp.dot((f * f).astype(BF16), wff2_ref[cols, :], preferred_element_type=F32)
        acc = p if acc is None else acc + p
    x = x + mod[5:6] * acc
    if final:
        x = _rms(x) * fg_ref[...]
    o_ref[...] = x


def _ffn(x_all, mod, wts, final_g, layer, *, tile0, n_tiles, final):
    res = lambda shape: _resident_layer(shape, layer)
    return pl.pallas_call(
        functools.partial(_ffn_kernel, final=final),
        grid=(n_tiles,),
        in_specs=[pl.BlockSpec((TM, D_MODEL), lambda i: (tile0 + i, 0)), _mod_spec(layer, tile0),
                  res((1, D_MODEL)), res((D_MODEL, D_FF)), res((D_FF, D_MODEL)), _resident((1, D_MODEL))],
        out_specs=pl.BlockSpec((TM, D_MODEL), lambda i: (i, 0)),
        out_shape=jax.ShapeDtypeStruct((n_tiles * TM, D_MODEL), F32),
        compiler_params=_cparams("arbitrary"),
        name="ffn",
    )(x_all, mod, wts["n2g"], wts["wff1"], wts["wff2"], final_g)


def _rope_tables(r):
    quarter = r // 4
    t = np.arange(DEC_SEQ)
    inv_freq = np.power(np.float32(ROPE_BASE), -np.arange(quarter, dtype=np.float32) / np.float32(quarter))
    cs, ss = [], []
    for pos in (t // GRID_W, t % GRID_W):
        ang = pos.astype(np.float32)[:, None] * inv_freq[None, :]
        c, s = np.cos(ang), np.sin(ang)
        cs += [c, c]
        ss += [-s, s]
    return np.concatenate(cs, axis=1), np.concatenate(ss, axis=1)


def _tables():
    def with_identity(c, s):
        one = np.ones((TM, LANES), np.float32)
        return (np.concatenate([one, c], axis=0).astype(np.float32),
                np.concatenate([0.0 * one, s], axis=0).astype(np.float32))

    c64, s64 = _rope_tables(A_HEAD_DIM)
    ca, sa = with_identity(np.tile(c64, (1, 2)), np.tile(s64, (1, 2)))
    c32, s32 = _rope_tables(B_ROPE)
    ones = lambda w: np.ones((DEC_SEQ, w), np.float32)
    zeros = lambda w: np.zeros((DEC_SEQ, w), np.float32)
    cs, ss = with_identity(np.concatenate([c32, ones(96)], axis=1), np.concatenate([s32, zeros(96)], axis=1))
    cq, sq = with_identity(np.concatenate([ones(64), c32, ones(32)], axis=1),
                           np.concatenate([zeros(64), s32, zeros(32)], axis=1))
    return {"ca": ca, "sa": sa, "cs": cs, "ss": ss, "cq": cq, "sq": sq}


def _prep_weights(norm1_g, norm2_g, w_in, b_gq, b_w_uq, b_gkv, b_w_ukv, c_a_log, c_dt_bias, c_onorm_g,
                  w_branch, w_out, w_ff1, w_ff2):
    offs = [0]
    for width in W_IN_SPLITS:
        offs.append(offs[-1] + width)
    piece = lambda i, j: w_in[:, :, offs[i]:offs[j]]
    w_head = jnp.concatenate([piece(0, 6), piece(7, 9), jnp.zeros((DEPTH, D_MODEL, LANES - 48), F32)],
                             axis=2).astype(BF16)
    w_cqkv = piece(6, 7).astype(BF16)
    w_tail = piece(9, 11).astype(BF16)

    pad_lane = lambda v: jnp.zeros((DEPTH, 1, LANES), F32).at[:, 0, SM_A:SM_A + 8].set(v.reshape(DEPTH, 8))
    uq = b_w_uq.reshape(DEPTH, B_Q_LORA, B_HEADS, B_NOPE + B_ROPE)
    uq = jnp.pad(uq, ((0, 0), (0, 0), (0, 0), (0, LANES - B_NOPE - B_ROPE)))
    uq = uq.reshape(DEPTH, B_Q_LORA, 1024).astype(BF16)
    ukv = b_w_ukv.reshape(DEPTH, B_KV_LORA, B_HEADS, B_NOPE + B_V)
    ukk = jnp.pad(ukv[..., :B_NOPE], ((0, 0), (0, 0), (0, 0), (0, LANES - B_NOPE)))
    ukk = ukk.reshape(DEPTH, B_KV_LORA, 1024).astype(BF16)
    ukvv = ukv[..., B_NOPE:].reshape(DEPTH, B_KV_LORA, 512).astype(BF16)
    src = np.arange(LANES)[:, None]
    dst = np.arange(1024)[None, :]
    pkr = jnp.asarray((dst % LANES == B_NOPE + src) & (src < B_ROPE), BF16)

    return {"w_head": w_head, "w_cqkv": w_cqkv, "w_tail": w_tail,
            "alog": pad_lane(c_a_log), "dtb": pad_lane(c_dt_bias),
            "n1g": norm1_g.reshape(DEPTH, 1, D_MODEL), "n2g": norm2_g.reshape(DEPTH, 1, D_MODEL),
            "gq": b_gq.reshape(DEPTH, 1, B_Q_LORA), "gkv": b_gkv.reshape(DEPTH, 1, B_KV_LORA),
            "ong": c_onorm_g.reshape(DEPTH, 1, C_DV),
            "wuq": uq, "wukk": ukk, "pkr": pkr, "wukv": ukvv,
            "wbr": w_branch.astype(BF16), "wout": w_out.astype(BF16), "wff1": w_ff1.astype(BF16), "wff2": w_ff2.astype(BF16)}


def kernel(x_prompt, x_sample, cache_a_k, cache_a_v, cache_b_ckv, cache_b_krope, state_c, c, c_ctx, norm1_g,
           norm2_g, w_ada, b_ada, w_in, a_sink, b_gq, b_w_uq, b_gkv, b_w_ukv, c_conv_w, c_a_log, c_dt_bias,
           c_onorm_g, w_branch, w_out, w_ff1, w_ff2, final_g):
    xs = (x_prompt.reshape(T_CTX, D_MODEL), x_sample.reshape(T_LAT, D_MODEL))
    cond8 = jnp.concatenate([c_ctx[None, :], c, jnp.zeros((N_GROUPS - 1 - DEC_BATCH, D_MODEL), F32)], axis=0)
    mod = _adaln(cond8, w_ada, b_ada).reshape(DEPTH, N_GROUPS, 6, D_MODEL)
    tabs = _tables()
    fg = final_g.reshape(1, D_MODEL)
    wts = _prep_weights(norm1_g, norm2_g, w_in, b_gq, b_w_uq, b_gkv, b_w_ukv, c_a_log, c_dt_bias, c_onorm_g,
                        w_branch, w_out, w_ff1, w_ff2)
    ck = cache_a_k.reshape(DEC_BATCH, DEPTH, PAST_LEN, LANES)
    cv = cache_a_v.reshape(DEC_BATCH, DEPTH, PAST_LEN, LANES)
    cache_kr = jnp.pad(cache_b_krope, ((0, 0), (0, 0), (0, 0), (0, LANES - B_ROPE)))

    ak_l, av_l, ckv_l, kr_l, sc_l = [], [], [], [], []
    for l in range(DEPTH):
        (aq, ak, av, bq, bk, bv, ckv, small, cqkv, gcol, grow, cz) = _inproj(xs, mod, wts, tabs, l)

        oa_ctx = _attn_a_ctx(a_sink, aq, ak, av, l)
        oa_lat = _attn_a_lat(a_sink, aq, ak, av, ck, cv, l)

        k2, v2 = _mla_cache(cache_b_ckv, cache_kr, wts, l)
        ob_ctx = _mla_ctx(bq, bk, bv)
        ob_lat = _mla_lat(bq, bk, bv, k2, v2)

        grow3 = grow.reshape(16, T_ALL // GDN_CHUNK, 1, GDN_CHUNK)
        oc_ctx, s_fin = _gdn(cqkv, c_conv_w, gcol, grow3, cz, wts["ong"], state_c, l, latent=False)
        (oc_lat,) = _gdn(cqkv, c_conv_w, gcol, grow3, cz, wts["ong"], state_c, l, latent=True)

        branches = (oa_ctx, oa_lat, ob_ctx, ob_lat, oc_ctx, oc_lat)
        x_mid = _merge(xs, branches, mod, wts, l)
        ffn = functools.partial(_ffn, x_mid, mod, wts, fg, l)
        if l < DEPTH - 1:
            xs = (ffn(tile0=0, n_tiles=N_TILES, final=False),)
        else:
            xs = (ffn(tile0=0, n_tiles=N_CTX_TILES, final=True),
                  ffn(tile0=N_CTX_TILES, n_tiles=N_TILES - N_CTX_TILES, final=True))

        ak_l.append(ak[:T_CTX].reshape(BATCH, SEQ, A_KV_HEADS, A_HEAD_DIM))
        av_l.append(av[:T_CTX].reshape(BATCH, SEQ, A_KV_HEADS, A_HEAD_DIM))
        ckv_l.append(ckv[:T_CTX].reshape(BATCH, SEQ, B_KV_LORA))
        kr_l.append(small[:T_CTX, :B_ROPE].reshape(BATCH, SEQ, B_ROPE))
        sc_l.append(s_fin)

    y_prompt = xs[0].reshape(BATCH, SEQ, D_MODEL)
    y_sample = xs[1].reshape(DEC_BATCH, DEC_SEQ, D_MODEL)
    return (y_prompt, y_sample, jnp.stack(ak_l, axis=1), jnp.stack(av_l, axis=1), jnp.stack(ckv_l, axis=1),
            jnp.stack(kr_l, axis=1), jnp.stack(sc_l, axis=1))
```

```python
import functools
import math

import jax
import jax.numpy as jnp
import numpy as np
from jax import lax
from jax.experimental import pallas as pl
from jax.experimental.pallas import tpu as pltpu

F32 = jnp.float32
BF16 = jnp.bfloat16

D_MODEL = 1024
BATCH = 16
SEQ = 256
DEPTH = 2
DEC_BATCH = 4
DEC_SEQ = 2048
PAST_LEN = 256
GRID_W = 64
ROPE_BASE = 10000.0
EPS = 1e-6
NEG_INF = -1e30

A_HEADS = 8
A_KV_HEADS = 2
A_GROUP = 4
A_HEAD_DIM = 64
A_WINDOW = 128
A_SCALE = A_HEAD_DIM ** -0.5
B_HEADS = 8
B_Q_LORA = 384
B_KV_LORA = 256
B_NOPE = 64
B_ROPE = 32
B_V = 64
MLA_SCALE = (B_NOPE + B_ROPE) ** -0.5
LOG2E = math.log2(math.e)
C_HEADS = 4
C_DK = 128
C_DV = 128
D_FF = 4 * D_MODEL
BRANCH_W = 512

LANES = 128

T_CTX = BATCH * SEQ
T_LAT = DEC_BATCH * DEC_SEQ
T_ALL = T_CTX + T_LAT
TM = 512
N_CTX_TILES = T_CTX // TM
LAT_TILES_PER_SEQ = DEC_SEQ // TM
N_TILES = T_ALL // TM
N_GROUPS = 8
GDN_CHUNK = 128
GDN_PREP_CHAINS = 16
GDN_HEADS_PER_STEP_CTX = 4
GDN_HEADS_PER_STEP_LAT = 2

W_IN_SPLITS = (512, 128, 128, B_Q_LORA, B_KV_LORA, B_ROPE, 1536, 8, 8, 512, 3072)
O_AQ, O_AK, O_AV, O_BCQ, O_BCKV, O_KR, O_MID, W_LEAD_COLS = 0, 512, 640, 768, 1152, 1408, 1536, 3072
C_QKV_COL = O_KR + B_ROPE
C_AB_COL = C_QKV_COL + 1536
W_TAIL_COLS = 512 + 3 * D_MODEL
SM_KR, SM_A, SM_B = 0, 32, 40

VMEM_LIMIT = 56 * 1024 * 1024


def _cparams(*sem):
    return pltpu.CompilerParams(dimension_semantics=sem, vmem_limit_bytes=VMEM_LIMIT)


def _resident(shape):
    nd = len(shape)
    return pl.BlockSpec(shape, lambda *_: (0,) * nd, pipeline_mode=pl.Buffered(1))


def _resident_layer(shape, layer):
    nd = len(shape)
    return pl.BlockSpec((None,) + tuple(shape), lambda *_: (layer,) + (0,) * nd, pipeline_mode=pl.Buffered(1))


def _group_of_tile(i):
    return jnp.where(i < N_CTX_TILES, 0, 1 + (i - N_CTX_TILES) // LAT_TILES_PER_SEQ)


def _rope_block_of_tile(i):
    return jnp.where(i < N_CTX_TILES, 0, 1 + (i - N_CTX_TILES) % LAT_TILES_PER_SEQ)


def _mod_spec(layer, tile0=0):
    return pl.BlockSpec((None, 1, 6, D_MODEL), lambda i: (layer, _group_of_tile(tile0 + i), 0, 0))


def _stream_specs(width, n_arrays, tile0=0):
    if n_arrays == 1:
        return [pl.BlockSpec((TM, width), lambda i: (tile0 + i, 0))]
    return [pl.BlockSpec((TM, width), lambda i: (jnp.minimum(tile0 + i, N_CTX_TILES - 1), 0)),
            pl.BlockSpec((TM, width), lambda i: (jnp.maximum(tile0 + i - N_CTX_TILES, 0), 0))]


def _pick_stream(refs, tile0=0):
    if len(refs) == 1:
        return refs[0][...]
    return jnp.where(tile0 + pl.program_id(0) < N_CTX_TILES, refs[0][...], refs[1][...])


def _rms(x):
    return x * lax.rsqrt(jnp.mean(x * x, axis=-1, keepdims=True) + EPS)


def _sigmoid(x):
    return 0.5 * jnp.tanh(0.5 * x) + 0.5


def _silu(x):
    h = 0.5 * x
    return h * jnp.tanh(h) + h


def _softplus(x):
    return jnp.maximum(x, 0.0) + jnp.log(1.0 + jnp.exp(-jnp.abs(x)))


def _dot(a, b):
    return jnp.dot(a.astype(BF16), b.astype(BF16), preferred_element_type=F32)


def _dot_nt(a, b):
    return lax.dot_general(a.astype(BF16), b.astype(BF16), (((1,), (1,)), ((), ())),
                           preferred_element_type=F32)


def _swap_groups(x, q):
    w = x.shape[1]
    lane = lax.broadcasted_iota(jnp.int32, x.shape, 1)
    even = ((lane // q) % 2) == 0
    return jnp.where(even, pltpu.roll(x, w - q, 1), pltpu.roll(x, q, 1))


def _rope(x, cos, sin, q):
    return x * cos + _swap_groups(x, q) * sin


ADA_TN = 1536


def _adaln_kernel(c_ref, w_ref, b_ref, o_ref):
    c = c_ref[...]
    o_ref[0] = jnp.dot(_silu(c), w_ref[0], preferred_element_type=F32) + b_ref[0]


def _adaln(cond8, w_ada, b_ada):
    n = 6 * D_MODEL
    return pl.pallas_call(
        _adaln_kernel,
        grid=(DEPTH, n // ADA_TN),
        in_specs=[
            pl.BlockSpec((N_GROUPS, D_MODEL), lambda l, j: (0, 0)),
            pl.BlockSpec((1, D_MODEL, ADA_TN), lambda l, j: (l, 0, j)),
            pl.BlockSpec((1, 1, ADA_TN), lambda l, j: (l, 0, j)),
        ],
        out_specs=pl.BlockSpec((1, N_GROUPS, ADA_TN), lambda l, j: (l, 0, j)),
        out_shape=jax.ShapeDtypeStruct((DEPTH, N_GROUPS, n), F32),
        compiler_params=_cparams("arbitrary", "arbitrary"),
        name="adaln",
    )(cond8, w_ada, b_ada.reshape(DEPTH, 1, n))


def _chunk_scan(g, forward):
    n = g.shape[0]
    pos = lax.broadcasted_iota(jnp.int32, g.shape, 0) % GDN_CHUNK
    s = 1
    while s < GDN_CHUNK:
        if forward:
            g = g + jnp.where(pos >= s, pltpu.roll(g, s, 0), 0.0)
        else:
            g = g + jnp.where(pos < GDN_CHUNK - s, pltpu.roll(g, n - s, 0), 0.0)
        s *= 2
    return g


def _inproj_kernel(*refs, n_x):
    x_refs, refs = refs[:n_x], refs[n_x:]
    (mod_ref, n1g_ref, w_lead_ref, w_cz_ref,
     ca_ref, sa_ref, cs_ref, ss_ref, cq_ref, sq_ref,
     alog_ref, dtb_ref, gq_ref, gkv_ref, wuq_ref, wukk_ref, pkr_ref, wukv_ref,
     aq_ref, ak_ref, av_ref, bq_ref, bk_ref, bv_ref, ckv_ref, small_ref,
     cqkv_ref, gcol_ref, grow_ref, cz_ref) = refs
    mod = mod_ref[0]
    h = _rms(_pick_stream(x_refs)) * n1g_ref[...] * (1.0 + mod[1:2]) + mod[0:1]
    hb = h.astype(BF16)

    def proj(w_ref, lo, hi):
        return jnp.dot(hb, w_ref[:, lo:hi], preferred_element_type=F32)

    ca, sa = ca_ref[...], sa_ref[...]
    a_all = proj(w_lead_ref, O_AQ, O_BCQ)
    for c in range(4):
        q = a_all[:, c * LANES:(c + 1) * LANES]
        aq_ref[:, c * LANES:(c + 1) * LANES] = (_rope(q, ca, sa, 16) * (A_SCALE * LOG2E)).astype(BF16)
    ak_ref[...] = _rope(a_all[:, O_AK:O_AV], ca, sa, 16)
    av_ref[...] = a_all[:, O_AV:O_BCQ]

    b_all = proj(w_lead_ref, O_BCQ, O_MID)
    cq = (_rms(b_all[:, :B_Q_LORA]) * gq_ref[...]).astype(BF16)
    cqc, sqc = cq_ref[...], sq_ref[...]
    q_all = jnp.dot(cq, wuq_ref[...], preferred_element_type=F32)
    for hh in range(B_HEADS):
        qh = q_all[:, hh * LANES:(hh + 1) * LANES]
        bq_ref[:, hh * LANES:(hh + 1) * LANES] = (_rope(qh, cqc, sqc, 8) * (MLA_SCALE * LOG2E)).astype(BF16)

    ckv = _rms(b_all[:, B_Q_LORA:B_Q_LORA + B_KV_LORA]) * gkv_ref[...]
    ckv_ref[...] = ckv
    ckvb = ckv.astype(BF16)
    kr_blk = b_all[:, O_KR - O_BCQ:]
    mid = proj(w_lead_ref, O_MID, W_LEAD_COLS)
    ab_blk = mid[:, C_AB_COL // LANES * LANES - O_MID:][:, :LANES]
    lane_t = lax.broadcasted_iota(jnp.int32, kr_blk.shape, 1)
    small = jnp.where(lane_t < SM_A, kr_blk, jnp.where(lane_t < SM_B + 2 * C_HEADS, ab_blk, 0.0))
    span = jnp.concatenate([kr_blk, mid], axis=1)
    cqkv_ref[...] = pltpu.roll(span, span.shape[1] - (C_QKV_COL - O_KR), 1)[:, :1536]
    small_r = _rope(small, cs_ref[...], ss_ref[...], 8)
    small_ref[...] = small_r
    krb = small_r.astype(BF16)
    bk_ref[...] = (jnp.dot(ckvb, wukk_ref[...], preferred_element_type=F32)
                   + jnp.dot(krb, pkr_ref[...], preferred_element_type=F32)).astype(BF16)
    bv_ref[...] = jnp.dot(ckvb, wukv_ref[...], preferred_element_type=F32).astype(BF16)

    cz_ref[...] = proj(w_cz_ref, 0, 512)

    graw = -jnp.exp(alog_ref[...]) * _softplus(small + dtb_ref[...])
    lane = lax.broadcasted_iota(jnp.int32, small.shape, 1)
    gcol = jnp.where(lane < SM_A + C_HEADS, _chunk_scan(graw, True), _chunk_scan(graw, False))
    gcol = jnp.where((lane >= SM_A) & (lane < SM_B), gcol,
                     jnp.where((lane >= SM_B) & (lane < SM_B + 2 * C_HEADS), _sigmoid(small), 0.0))
    gcol_ref[...] = gcol
    grow_ref[...] = gcol.T[SM_A:SM_A + 16, :]


def _inproj(xs, mod, wts, tabs, layer):
    tile = lambda w: pl.BlockSpec((TM, w), lambda i: (i, 0))
    rope_spec = pl.BlockSpec((TM, LANES), lambda i: (_rope_block_of_tile(i), 0))
    res = lambda shape: _resident_layer(shape, layer)
    out_shapes = [
        jax.ShapeDtypeStruct((T_ALL, 512), BF16),
        jax.ShapeDtypeStruct((T_ALL, 128), F32),
        jax.ShapeDtypeStruct((T_ALL, 128), F32),
        jax.ShapeDtypeStruct((T_ALL, 1024), BF16),
        jax.ShapeDtypeStruct((T_ALL, 1024), BF16),
        jax.ShapeDtypeStruct((T_ALL, 512), BF16),
        jax.ShapeDtypeStruct((T_ALL, 256), F32),
        jax.ShapeDtypeStruct((T_ALL, 128), F32),
        jax.ShapeDtypeStruct((T_ALL, 1536), F32),
        jax.ShapeDtypeStruct((T_ALL, 128), F32),
        jax.ShapeDtypeStruct((16, T_ALL), F32),
        jax.ShapeDtypeStruct((T_ALL, 512), F32),
    ]
    out_specs = [tile(512), tile(128), tile(128), tile(1024), tile(1024), tile(512), tile(256), tile(128),
                 tile(1536), tile(128), pl.BlockSpec((16, TM), lambda i: (0, i)), tile(512)]
    in_specs = _stream_specs(D_MODEL, len(xs)) + [
        _mod_spec(layer),
        res((1, D_MODEL)),
        res((D_MODEL, W_LEAD_COLS)),
        pl.BlockSpec((None, D_MODEL, 512), lambda i: (layer, 0, 0), pipeline_mode=pl.Buffered(1)),
        rope_spec, rope_spec, rope_spec, rope_spec, rope_spec, rope_spec,
        res((1, LANES)), res((1, LANES)),
        res((1, B_Q_LORA)), res((1, B_KV_LORA)),
        res((B_Q_LORA, 1024)), res((B_KV_LORA, 1024)), _resident((LANES, 1024)),
        res((B_KV_LORA, 512)),
    ]
    return pl.pallas_call(
        functools.partial(_inproj_kernel, n_x=len(xs)),
        grid=(N_TILES,),
        in_specs=in_specs,
        out_specs=out_specs,
        out_shape=out_shapes,
        compiler_params=_cparams("arbitrary"),
        name="inproj",
    )(*xs, mod, wts["n1g"], wts["w_lead"], wts["w_tail"],
      tabs["ca"], tabs["sa"], tabs["cs"], tabs["ss"], tabs["cq"], tabs["sq"],
      wts["alog"], wts["dtb"], wts["gq"], wts["gkv"], wts["wuq"], wts["wukk"], wts["pkr"], wts["wukv"])


def _mla_cache_kernel(ckv_ref, kr_ref, wukk_ref, pkr_ref, wukv_ref, k_ref, v_ref):
    ckvb = ckv_ref[...].astype(BF16)
    krb = kr_ref[...].astype(BF16)
    k_ref[...] = (jnp.dot(ckvb, wukk_ref[...], preferred_element_type=F32)
                  + jnp.dot(krb, pkr_ref[...], preferred_element_type=F32)).astype(BF16)
    v_ref[...] = jnp.dot(ckvb, wukv_ref[...], preferred_element_type=F32).astype(BF16)


def _mla_cache(cache_ckv, cache_kr, wts, layer):
    n = DEC_BATCH * PAST_LEN
    res = lambda shape: _resident_layer(shape, layer)
    return pl.pallas_call(
        _mla_cache_kernel,
        grid=(DEC_BATCH,),
        in_specs=[pl.BlockSpec((None, None, PAST_LEN, B_KV_LORA), lambda i: (i, layer, 0, 0)),
                  pl.BlockSpec((None, None, PAST_LEN, LANES), lambda i: (i, layer, 0, 0)),
                  res((B_KV_LORA, 1024)), _resident((LANES, 1024)), res((B_KV_LORA, 512))],
        out_specs=[pl.BlockSpec((PAST_LEN, 1024), lambda i: (i, 0)),
                   pl.BlockSpec((PAST_LEN, 512), lambda i: (i, 0))],
        out_shape=[jax.ShapeDtypeStruct((n, 1024), BF16), jax.ShapeDtypeStruct((n, 512), BF16)],
        compiler_params=_cparams("arbitrary"),
        name="mla_cache",
    )(cache_ckv, cache_kr, wts["wukk"], wts["pkr"], wts["wukv"])


TQ = 256
A_WIN_KEYS = TQ + 2 * A_WINDOW
_NT = (((1,), (1,)), ((), ()))


def _softmax_pv(scores, values, extra_logit=None):
    m = scores[0].max(axis=-1, keepdims=True)
    for s in scores[1:]:
        m = jnp.maximum(m, s.max(axis=-1, keepdims=True))
    if extra_logit is not None:
        m = jnp.maximum(m, extra_logit)
    den = None
    acc = None
    for s, v in zip(scores, values):
        e = jnp.exp2(s - m)
        d = e.sum(axis=-1, keepdims=True)
        den = d if den is None else den + d
        pv = jnp.dot(e.astype(BF16), v, preferred_element_type=F32)
        acc = pv if acc is None else acc + pv
    if extra_logit is not None:
        den = den + jnp.exp2(extra_logit - m)
    return acc / den


def _attn_a_kernel(sink_ref, q_ref, k1_ref, v1_ref, *rest, latent, layer):
    if latent:
        k2_ref, v2_ref, o_ref = rest
        qt = pl.program_id(1)
        t0 = qt * TQ
        start = pl.multiple_of(jnp.clip(t0 - A_WINDOW, 0, DEC_SEQ - A_WIN_KEYS), A_WINDOW)
        k1 = k1_ref[pl.ds(start, A_WIN_KEYS), :]
        v1 = v1_ref[pl.ds(start, A_WIN_KEYS), :]
        qpos = t0 + lax.broadcasted_iota(jnp.int32, (TQ, A_WIN_KEYS), 0)
        kpos = start + lax.broadcasted_iota(jnp.int32, (TQ, A_WIN_KEYS), 1)
        valid = jnp.abs(qpos - kpos) <= A_WINDOW
    else:
        (o_ref,) = rest
        k1 = k1_ref[...]
        v1 = v1_ref[...]
    both = lambda x: (x.astype(BF16), pltpu.roll(x, A_HEAD_DIM, 1).astype(BF16))
    k1, v1 = both(k1), both(v1)
    if latent:
        k2, v2 = both(k2_ref[...]), both(v2_ref[...])
    lane = lax.broadcasted_iota(jnp.int32, (TQ, LANES), 1)
    low = lane < A_HEAD_DIM
    top = lax.broadcasted_iota(jnp.int32, (2 * TQ, 1), 0) < TQ
    if latent:
        valid2 = jnp.concatenate([valid, valid], axis=0)
    for hk in range(A_KV_HEADS):
        chunks = (2 * hk, 2 * hk + 1)
        qcs = [q_ref[:, c * LANES:(c + 1) * LANES] for c in chunks]
        outs = []
        for p in range(2):
            sel = 0 if p == hk else 1
            keep = low if p == 0 else ~low
            qz = jnp.concatenate([jnp.where(keep, qc, jnp.zeros_like(qc)) for qc in qcs], axis=0)
            s1 = lax.dot_general(qz, k1[sel], _NT, preferred_element_type=F32)
            sink = jnp.where(top, sink_ref[layer, 2 * chunks[0] + p], sink_ref[layer, 2 * chunks[1] + p]) * LOG2E
            if latent:
                s1 = jnp.where(valid2, s1, NEG_INF)
                s2 = lax.dot_general(qz, k2[sel], _NT, preferred_element_type=F32)
                outs.append(_softmax_pv([s1, s2], [v1[sel], v2[sel]], sink))
            else:
                outs.append(_softmax_pv([s1], [v1[sel]], sink))
        for i, c in enumerate(chunks):
            rows = slice(i * TQ, (i + 1) * TQ)
            o_ref[:, c * LANES:(c + 1) * LANES] = jnp.where(low, outs[0][rows], outs[1][rows]).astype(BF16)


def _attn_a_ctx(sink, aq, ak, av, layer):
    nb = T_CTX // TQ
    return pl.pallas_call(
        functools.partial(_attn_a_kernel, latent=False, layer=layer),
        grid=(nb,),
        in_specs=[pl.BlockSpec(memory_space=pltpu.SMEM),
                  pl.BlockSpec((TQ, 512), lambda b: (b, 0)),
                  pl.BlockSpec((SEQ, LANES), lambda b: (b, 0)),
                  pl.BlockSpec((SEQ, LANES), lambda b: (b, 0))],
        out_specs=pl.BlockSpec((TQ, 512), lambda b: (b, 0)),
        out_shape=jax.ShapeDtypeStruct((T_CTX, 512), BF16),
        compiler_params=_cparams("arbitrary"),
        name="attn_a_ctx",
    )(sink, aq, ak, av)


def _attn_a_lat(sink, aq, ak, av, ck, cv, layer):
    nq = DEC_SEQ // TQ
    ctx_tiles = T_CTX // TQ
    seq_blocks = T_CTX // DEC_SEQ
    cache_spec = pl.BlockSpec((None, None, PAST_LEN, LANES), lambda b, i: (b, layer, 0, 0))
    return pl.pallas_call(
        functools.partial(_attn_a_kernel, latent=True, layer=layer),
        grid=(DEC_BATCH, nq),
        in_specs=[pl.BlockSpec(memory_space=pltpu.SMEM),
                  pl.BlockSpec((TQ, 512), lambda b, i: (ctx_tiles + b * nq + i, 0)),
                  pl.BlockSpec((DEC_SEQ, LANES), lambda b, i: (seq_blocks + b, 0)),
                  pl.BlockSpec((DEC_SEQ, LANES), lambda b, i: (seq_blocks + b, 0)),
                  cache_spec, cache_spec],
        out_specs=pl.BlockSpec((TQ, 512), lambda b, i: (b * nq + i, 0)),
        out_shape=jax.ShapeDtypeStruct((T_LAT, 512), BF16),
        compiler_params=_cparams("arbitrary", "arbitrary"),
        name="attn_a_lat",
    )(sink, aq, ak, av, ck, cv)


MLA_TQ_LAT = 512


def _mla_kernel(q_ref, k1_ref, v1_ref, *rest, latent):
    if latent:
        k2_ref, v2_ref, o_ref = rest
    else:
        (o_ref,) = rest
    lane = lax.broadcasted_iota(jnp.int32, (q_ref.shape[0], LANES), 1)
    low = lane < B_V
    for j in range(B_HEADS // 2):
        vs = slice((j // 2) * 2 * LANES, (j // 2 + 1) * 2 * LANES)
        half = slice((j % 2) * LANES, (j % 2 + 1) * LANES)
        outs = []
        for sub in range(2):
            hh = 2 * j + sub
            hs = slice(hh * LANES, (hh + 1) * LANES)
            qh = q_ref[:, hs]
            s1 = lax.dot_general(qh, k1_ref[:, hs], _NT, preferred_element_type=F32)
            if latent:
                s2 = lax.dot_general(qh, k2_ref[:, hs], _NT, preferred_element_type=F32)
                outs.append(_softmax_pv([s1, s2], [v1_ref[:, vs], v2_ref[:, vs]])[:, half])
            else:
                outs.append(_softmax_pv([s1], [v1_ref[:, vs]])[:, half])
        o_ref[:, j * LANES:(j + 1) * LANES] = jnp.where(low, outs[0], outs[1]).astype(BF16)


def _mla_ctx(bq, bk, bv):
    nb = T_CTX // TQ
    return pl.pallas_call(
        functools.partial(_mla_kernel, latent=False),
        grid=(nb,),
        in_specs=[pl.BlockSpec((TQ, 1024), lambda b: (b, 0)),
                  pl.BlockSpec((SEQ, 1024), lambda b: (b, 0)),
                  pl.BlockSpec((SEQ, 512), lambda b: (b, 0))],
        out_specs=pl.BlockSpec((TQ, 512), lambda b: (b, 0)),
        out_shape=jax.ShapeDtypeStruct((T_CTX, 512), BF16),
        compiler_params=_cparams("arbitrary"),
        name="mla_ctx",
    )(bq, bk, bv)


def _mla_lat(bq, bk, bv, k2, v2):
    tq = MLA_TQ_LAT
    nq = DEC_SEQ // tq
    ctx_tiles = T_CTX // tq
    seq_blocks = T_CTX // DEC_SEQ
    return pl.pallas_call(
        functools.partial(_mla_kernel, latent=True),
        grid=(DEC_BATCH, nq),
        in_specs=[pl.BlockSpec((tq, 1024), lambda b, i: (ctx_tiles + b * nq + i, 0)),
                  pl.BlockSpec((DEC_SEQ, 1024), lambda b, i: (seq_blocks + b, 0)),
                  pl.BlockSpec((DEC_SEQ, 512), lambda b, i: (seq_blocks + b, 0)),
                  pl.BlockSpec((PAST_LEN, 1024), lambda b, i: (b, 0)),
                  pl.BlockSpec((PAST_LEN, 512), lambda b, i: (b, 0))],
        out_specs=pl.BlockSpec((tq, 512), lambda b, i: (b * nq + i, 0)),
        out_shape=jax.ShapeDtypeStruct((T_LAT, 512), BF16),
        compiler_params=_cparams("arbitrary", "arbitrary"),
        name="mla_lat",
    )(bq, bk, bv, k2, v2)


def _gdn_kernel(xq_ref, xk_ref, xv_ref, wq_ref, wk_ref, wv_ref, gcol_ref, grow_ref, cz_ref, ong_ref, *rest,
                n, hps, has_s0, emit_state):
    rest = list(rest)
    s0_ref = rest.pop(0) if has_s0 else None
    oc_ref = rest.pop(0)
    sfin_ref = rest.pop(0) if emit_state else None
    q_s, k_s, v_s, u_s, w_s, qk_s, qd_s, kdt_s, ge_s, st_s, o_s = rest
    nc = n // GDN_CHUNK
    head0 = pl.program_id(1) * hps
    C = GDN_CHUNK
    width = hps * LANES
    hl = lambda hh: slice(hh * LANES, (hh + 1) * LANES)
    chains = [(d, hh) for d in range(2) for hh in range(hps)]

    row8 = lax.broadcasted_iota(jnp.int32, (8, width), 0)

    def conv_silu(x_ref, w_ref):
        x = x_ref[...]
        w = w_ref[...]
        y = pltpu.roll(x, 1, 0) * w[0:1] + x * w[1:2] + pltpu.roll(x, n - 1, 0) * w[2:3]
        top = y[0:8] - jnp.where(row8 == 0, x[n - 1:n] * w[0:1], 0.0)
        bot = y[n - 8:n] - jnp.where(row8 == 7, x[0:1] * w[2:3], 0.0)
        return _silu(jnp.concatenate([top, y[8:n - 8], bot], axis=0))

    def l2n(x):
        return x * lax.rsqrt(jnp.sum(x * x, axis=-1, keepdims=True) + EPS)

    qc = conv_silu(xq_ref, wq_ref)
    kc = conv_silu(xk_ref, wk_ref)
    for hh in range(hps):
        q_s[:, hl(hh)] = l2n(qc[:, hl(hh)]) * (C_DK ** -0.5)
        k_s[:, hl(hh)] = l2n(kc[:, hl(hh)])
    v_s[...] = conv_silu(xv_ref, wv_ref)
    o_s[...] = jnp.zeros_like(o_s)

    lane_c = lax.broadcasted_iota(jnp.int32, (C, LANES), 1)
    ri = lax.broadcasted_iota(jnp.int32, (C, C), 0)
    ci = lax.broadcasted_iota(jnp.int32, (C, C), 1)
    eye = (ri == ci).astype(F32)
    same_block = {}
    s = 2
    while s <= C:
        same_block[s] = (ri // s) == (ci // s)
        s *= 2

    def prep(it, carry):
        items = []
        for gi in range(group):
            r = it * group + gi
            rows = pl.ds(pl.multiple_of(r * C, C), C)
            gtile = gcol_ref[rows, :]
            for hh in range(hps):
                q = q_s[rows, hl(hh)]
                k = k_s[rows, hl(hh)]
                v = v_s[rows, hl(hh)]
                qk_raw = _dot_nt(q, k)
                for d in range(2):
                    slot = d * hps + hh
                    j = d * C_HEADS + head0 + hh
                    gc = jnp.sum(jnp.where(lane_c == SM_A + j, gtile, 0.0), axis=1, keepdims=True)
                    beta = jnp.sum(jnp.where(lane_c == SM_B + j, gtile, 0.0), axis=1, keepdims=True)
                    grow = grow_ref[j, r]
                    incl = (ri >= ci) if d == 0 else (ri <= ci)
                    strict = (ri > ci) if d == 0 else (ri < ci)
                    decay = jnp.where(incl, jnp.exp(jnp.where(incl, gc - grow, 0.0)), 0.0)
                    kb = k * beta
                    a = jnp.where(strict, _dot_nt(kb, k) * decay, 0.0)
                    g_end = gc[C - 1:C, :] if d == 0 else gc[0:1, :]
                    egc = jnp.exp(gc)
                    qk_s[slot, rows, :] = (qk_raw * decay).astype(BF16)
                    qd_s[slot, rows, :] = (q * egc).astype(BF16)
                    kdt_s[slot, rows, :] = (k * jnp.exp(g_end - gc)).T.astype(BF16)
                    ge_s[slot, r] = jnp.broadcast_to(jnp.exp(g_end), (8, LANES))
                    items.append((slot, rows, a, jnp.concatenate([v * beta, kb * egc], axis=1), d))
        ts = [eye - jnp.where(same_block[2], a, 0.0) for (_, _, a, _, _) in items]
        s = 2
        while s < C:
            level = same_block[2 * s] & ~same_block[s]
            es = [jnp.where(level, a, 0.0) for (_, _, a, _, _) in items]
            if s < 8:
                xs = [_dot(t, e) for t, e in zip(ts, es)]
                ts = [t - _dot(x, t) for t, x in zip(ts, xs)]
            else:
                def active(t, d):
                    return jnp.concatenate([t[b * s:(b + 1) * s] for b in range(1 - d, C // s, 2)], axis=0)

                def merged(t, upd, d):
                    return jnp.concatenate(
                        [upd[(b // 2) * s:(b // 2 + 1) * s] if b % 2 == 1 - d else t[b * s:(b + 1) * s]
                         for b in range(C // s)], axis=0)

                halves = [active(t, it_[4]) for t, it_ in zip(ts, items)]
                xs = [_dot(h, e) for h, e in zip(halves, es)]
                upds = [h - _dot(x, t) for h, x, t in zip(halves, xs, ts)]
                ts = [merged(t, u, it_[4]) for t, u, it_ in zip(ts, upds, items)]
            s *= 2
        for t, (slot, rows, _, rhs, _) in zip(ts, items):
            uw = _dot(t, rhs)
            u_s[slot, rows, :] = uw[:, :C_DV]
            w_s[slot, rows, :] = uw[:, C_DV:].astype(BF16)
        return carry

    group = max(1, min(nc, GDN_PREP_CHAINS // (2 * hps)))
    lax.fori_loop(0, nc // group, prep, 0)

    for d, hh in chains:
        if has_s0:
            st_s[d * hps + hh] = s0_ref[d, hh]
        else:
            st_s[d * hps + hh] = jnp.zeros((C_DK, C_DV), F32)

    def step(i, carry):
        rows = [pl.ds(pl.multiple_of(i * C, C), C), pl.ds(pl.multiple_of((nc - 1 - i) * C, C), C)]
        chunk = [i, nc - 1 - i]
        slots = [d * hps + hh for d, hh in chains]
        st = [st_s[c] for c in slots]
        stb = [s_.astype(BF16) for s_ in st]
        ws = [jnp.dot(w_s[c, rows[d], :], sb, preferred_element_type=F32)
              for c, (d, _), sb in zip(slots, chains, stb)]
        vnb = [(u_s[c, rows[d], :] - x).astype(BF16) for c, (d, _), x in zip(slots, chains, ws)]
        for c, (d, hh), sb, vb, s_ in zip(slots, chains, stb, vnb, st):
            o = (jnp.dot(qd_s[c, rows[d], :], sb, preferred_element_type=F32)
                 + jnp.dot(qk_s[c, rows[d], :], vb, preferred_element_type=F32))
            o_s[rows[d], hl(hh)] += o
            st_s[c] = s_ * ge_s[c, chunk[d]][0:1, :] + jnp.dot(kdt_s[c, rows[d], :], vb,
                                                                preferred_element_type=F32)
        return carry

    lax.fori_loop(0, nc, step, 0)
    if emit_state:
        for d, hh in chains:
            sfin_ref[0, d, hh] = st_s[d * hps + hh]

    for hh in range(hps):
        o = o_s[:, hl(hh)]
        oc_ref[:, hl(hh)] = (_rms(o) * ong_ref[...] * _silu(cz_ref[:, hl(hh)])).astype(BF16)


def _gdn(cqkv, conv_w, gcol, grow3, cz, ong, state_c, layer, *, latent):
    n = DEC_SEQ if latent else SEQ
    nseq = DEC_BATCH if latent else BATCH
    hps = GDN_HEADS_PER_STEP_LAT if latent else GDN_HEADS_PER_STEP_CTX
    seq0 = (T_CTX // n) if latent else 0
    nc = n // GDN_CHUNK
    width = hps * LANES
    sect = C_HEADS // hps
    col = lambda base: pl.BlockSpec((n, width), lambda b, h: (seq0 + b, base + h))
    wcol = lambda base: pl.BlockSpec((None, 3, width), lambda b, h: (layer, 0, base + h))
    in_specs = [col(0), col(sect), col(2 * sect), wcol(0), wcol(sect), wcol(2 * sect),
                pl.BlockSpec((n, LANES), lambda b, h: (seq0 + b, 0)),
                pl.BlockSpec((16, nc, 1, LANES), lambda b, h: (0, seq0 + b, 0, 0)),
                pl.BlockSpec((n, width), lambda b, h: (seq0 + b, h)),
                pl.BlockSpec((None, 1, LANES), lambda b, h: (layer, 0, 0))]
    args = [cqkv, cqkv, cqkv, conv_w, conv_w, conv_w, gcol, grow3, cz, ong]
    out_specs = [pl.BlockSpec((n, width), lambda b, h: (b, h))]
    out_shape = [jax.ShapeDtypeStruct((nseq * n, 512), BF16)]
    if latent:
        in_specs.append(pl.BlockSpec((None, None, 2, hps, C_DK, C_DV), lambda b, h: (b, layer, 0, h, 0, 0)))
        args.append(state_c)
    else:
        out_specs.append(pl.BlockSpec((1, 2, hps, C_DK, C_DV), lambda b, h: (b, 0, h, 0, 0)))
        out_shape.append(jax.ShapeDtypeStruct((nseq, 2, C_HEADS, C_DK, C_DV), F32))
    nch = 2 * hps
    scratch = [pltpu.VMEM((n, width), F32), pltpu.VMEM((n, width), F32), pltpu.VMEM((n, width), F32),
               pltpu.VMEM((nch, n, LANES), F32), pltpu.VMEM((nch, n, LANES), BF16),
               pltpu.VMEM((nch, n, LANES), BF16), pltpu.VMEM((nch, n, LANES), BF16),
               pltpu.VMEM((nch, n, LANES), BF16), pltpu.VMEM((nch, nc, 8, LANES), F32),
               pltpu.VMEM((nch, C_DK, C_DV), F32), pltpu.VMEM((n, width), F32)]
    res = pl.pallas_call(
        functools.partial(_gdn_kernel, n=n, hps=hps, has_s0=latent, emit_state=not latent),
        grid=(nseq, C_HEADS // hps),
        in_specs=in_specs,
        out_specs=out_specs,
        out_shape=out_shape,
        scratch_shapes=scratch,
        compiler_params=_cparams("arbitrary", "arbitrary"),
        name="gdn_lat" if latent else "gdn_ctx",
    )(*args)
    return res


def _merge_kernel(*refs, n_x):
    x_refs, refs = refs[:n_x], refs[n_x:]
    branch_refs, refs = refs[:6], refs[6:]
    mod_ref, n1g_ref, wtail_ref, wbr_ref, wout_ref, o_ref = refs
    mod = mod_ref[0]
    ys = [jnp.dot(_pick_stream(branch_refs[2 * j:2 * j + 2]), wbr_ref[j], preferred_element_type=F32)
          for j in range(3)]
    x = _pick_stream(x_refs)
    hb = (_rms(x) * n1g_ref[...] * (1.0 + mod[1:2]) + mod[0:1]).astype(BF16)
    z = None
    for j in range(3):
        cols = slice(512 + j * D_MODEL, 512 + (j + 1) * D_MODEL)
        gate = _sigmoid(jnp.dot(hb, wtail_ref[:, cols], preferred_element_type=F32))
        z = gate * ys[j] if z is None else z + gate * ys[j]
    o_ref[...] = x + mod[2:3] * jnp.dot(z.astype(BF16), wout_ref[...], preferred_element_type=F32)


def _merge(xs, branches, mod, wts, layer):
    res = lambda shape: _resident_layer(shape, layer)
    in_specs = _stream_specs(D_MODEL, len(xs))
    for _ in range(3):
        in_specs += _stream_specs(BRANCH_W, 2)
    in_specs += [_mod_spec(layer), res((1, D_MODEL)), res((D_MODEL, W_TAIL_COLS)),
                 res((3, BRANCH_W, D_MODEL)), res((D_MODEL, D_MODEL))]
    return pl.pallas_call(
        functools.partial(_merge_kernel, n_x=len(xs)),
        grid=(N_TILES,),
        in_specs=in_specs,
        out_specs=pl.BlockSpec((TM, D_MODEL), lambda i: (i, 0)),
        out_shape=jax.ShapeDtypeStruct((T_ALL, D_MODEL), F32),
        compiler_params=_cparams("arbitrary"),
        name="merge",
    )(*xs, *branches, mod, wts["n1g"], wts["w_tail"], wts["wbr"], wts["wout"])


FF_CHUNK = 1024


def _ffn_kernel(x_ref, mod_ref, n2g_ref, wff1_ref, wff2_ref, fg_ref, o_ref, *, final):
    mod = mod_ref[0]
    x = x_ref[...]
    hb = (_rms(x) * n2g_ref[...] * (1.0 + mod[4:5]) + mod[3:4]).astype(BF16)
    acc = None
    for c in range(D_FF // FF_CHUNK):
        cols = slice(c * FF_CHUNK, (c + 1) * FF_CHUNK)
        f = jnp.maximum(jnp.dot(hb, wff1_ref[:, cols], preferred_element_type=F32), 0.0)
        p = jnp.dot((f * f).astype(BF16), wff2_ref[cols, :], preferred_element_type=F32)
        acc = p if acc is None else acc + p
    x = x + mod[5:6] * acc
    if final:
        x = _rms(x) * fg_ref[...]
    o_ref[...] = x


def _ffn(x_all, mod, wts, final_g, layer, *, tile0, n_tiles, final):
    res = lambda shape: _resident_layer(shape, layer)
    return pl.pallas_call(
        functools.partial(_ffn_kernel, final=final),
        grid=(n_tiles,),
        in_specs=[pl.BlockSpec((TM, D_MODEL), lambda i: (tile0 + i, 0)), _mod_spec(layer, tile0),
                  res((1, D_MODEL)), res((D_MODEL, D_FF)), res((D_FF, D_MODEL)), _resident((1, D_MODEL))],
        out_specs=pl.BlockSpec((TM, D_MODEL), lambda i: (i, 0)),
        out_shape=jax.ShapeDtypeStruct((n_tiles * TM, D_MODEL), F32),
        compiler_params=_cparams("arbitrary"),
        name="ffn",
    )(x_all, mod, wts["n2g"], wts["wff1"], wts["wff2"], final_g)


def _rope_tables(r):
    quarter = r // 4
    t = np.arange(DEC_SEQ)
    inv_freq = np.power(np.float32(ROPE_BASE), -np.arange(quarter, dtype=np.float32) / np.float32(quarter))
    cs, ss = [], []
    for pos in (t // GRID_W, t % GRID_W):
        ang = pos.astype(np.float32)[:, None] * inv_freq[None, :]
        c, s = np.cos(ang), np.sin(ang)
        cs += [c, c]
        ss += [-s, s]
    return np.concatenate(cs, axis=1), np.concatenate(ss, axis=1)


def _tables():
    def with_identity(c, s):
        one = np.ones((TM, LANES), np.float32)
        return (np.concatenate([one, c], axis=0).astype(np.float32),
                np.concatenate([0.0 * one, s], axis=0).astype(np.float32))

    c64, s64 = _rope_tables(A_HEAD_DIM)
    ca, sa = with_identity(np.tile(c64, (1, 2)), np.tile(s64, (1, 2)))
    c32, s32 = _rope_tables(B_ROPE)
    ones = lambda w: np.ones((DEC_SEQ, w), np.float32)
    zeros = lambda w: np.zeros((DEC_SEQ, w), np.float32)
    cs, ss = with_identity(np.concatenate([c32, ones(96)], axis=1), np.concatenate([s32, zeros(96)], axis=1))
    cq, sq = with_identity(np.concatenate([ones(64), c32, ones(32)], axis=1),
                           np.concatenate([zeros(64), s32, zeros(32)], axis=1))
    return {"ca": ca, "sa": sa, "cs": cs, "ss": ss, "cq": cq, "sq": sq}


def _prep_weights(norm1_g, norm2_g, w_in, b_gq, b_w_uq, b_gkv, b_w_ukv, c_a_log, c_dt_bias, c_onorm_g,
                  w_branch, w_out, w_ff1, w_ff2):
    offs = [0]
    for width in W_IN_SPLITS:
        offs.append(offs[-1] + width)
    assert (C_QKV_COL, C_AB_COL, C_AB_COL % LANES) == (offs[6], offs[7], SM_A) and offs[8] - offs[7] == SM_B - SM_A
    w_lead = w_in[:, :, :W_LEAD_COLS].astype(BF16)
    w_tail = w_in[:, :, offs[9]:offs[11]].astype(BF16)

    pad_lane = lambda v: jnp.zeros((DEPTH, 1, LANES), F32).at[:, 0, SM_A:SM_A + 8].set(v.reshape(DEPTH, 8))
    uq = b_w_uq.reshape(DEPTH, B_Q_LORA, B_HEADS, B_NOPE + B_ROPE)
    uq = jnp.pad(uq, ((0, 0), (0, 0), (0, 0), (0, LANES - B_NOPE - B_ROPE)))
    uq = uq.reshape(DEPTH, B_Q_LORA, 1024).astype(BF16)
    ukv = b_w_ukv.reshape(DEPTH, B_KV_LORA, B_HEADS, B_NOPE + B_V)
    ukk = jnp.pad(ukv[..., :B_NOPE], ((0, 0), (0, 0), (0, 0), (0, LANES - B_NOPE)))
    ukk = ukk.reshape(DEPTH, B_KV_LORA, 1024).astype(BF16)
    ukvv = ukv[..., B_NOPE:].reshape(DEPTH, B_KV_LORA, 512).astype(BF16)
    src = np.arange(LANES)[:, None]
    dst = np.arange(1024)[None, :]
    pkr = jnp.asarray((dst % LANES == B_NOPE + src) & (src < B_ROPE), BF16)

    return {"w_lead": w_lead, "w_tail": w_tail,
            "alog": pad_lane(c_a_log), "dtb": pad_lane(c_dt_bias),
            "n1g": norm1_g.reshape(DEPTH, 1, D_MODEL), "n2g": norm2_g.reshape(DEPTH, 1, D_MODEL),
            "gq": b_gq.reshape(DEPTH, 1, B_Q_LORA), "gkv": b_gkv.reshape(DEPTH, 1, B_KV_LORA),
            "ong": c_onorm_g.reshape(DEPTH, 1, C_DV),
            "wuq": uq, "wukk": ukk, "pkr": pkr, "wukv": ukvv,
            "wbr": w_branch.astype(BF16), "wout": w_out.astype(BF16), "wff1": w_ff1.astype(BF16), "wff2": w_ff2.astype(BF16)}


def kernel(x_prompt, x_sample, cache_a_k, cache_a_v, cache_b_ckv, cache_b_krope, state_c, c, c_ctx, norm1_g,
           norm2_g, w_ada, b_ada, w_in, a_sink, b_gq, b_w_uq, b_gkv, b_w_ukv, c_conv_w, c_a_log, c_dt_bias,
           c_onorm_g, w_branch, w_out, w_ff1, w_ff2, final_g):
    xs = (x_prompt.reshape(T_CTX, D_MODEL), x_sample.reshape(T_LAT, D_MODEL))
    cond8 = jnp.concatenate([c_ctx[None, :], c, jnp.zeros((N_GROUPS - 1 - DEC_BATCH, D_MODEL), F32)], axis=0)
    mod = _adaln(cond8, w_ada, b_ada).reshape(DEPTH, N_GROUPS, 6, D_MODEL)
    tabs = _tables()
    fg = final_g.reshape(1, D_MODEL)
    wts = _prep_weights(norm1_g, norm2_g, w_in, b_gq, b_w_uq, b_gkv, b_w_ukv, c_a_log, c_dt_bias, c_onorm_g,
                        w_branch, w_out, w_ff1, w_ff2)
    ck = cache_a_k.reshape(DEC_BATCH, DEPTH, PAST_LEN, LANES)
    cv = cache_a_v.reshape(DEC_BATCH, DEPTH, PAST_LEN, LANES)
    cache_kr = jnp.pad(cache_b_krope, ((0, 0), (0, 0), (0, 0), (0, LANES - B_ROPE)))

    ak_l, av_l, ckv_l, kr_l, sc_l = [], [], [], [], []
    for l in range(DEPTH):
        (aq, ak, av, bq, bk, bv, ckv, small, cqkv, gcol, grow, cz) = _inproj(xs, mod, wts, tabs, l)

        oa_ctx = _attn_a_ctx(a_sink, aq, ak, av, l)
        oa_lat = _attn_a_lat(a_sink, aq, ak, av, ck, cv, l)

        k2, v2 = _mla_cache(cache_b_ckv, cache_kr, wts, l)
        ob_ctx = _mla_ctx(bq, bk, bv)
        ob_lat = _mla_lat(bq, bk, bv, k2, v2)

        grow3 = grow.reshape(16, T_ALL // GDN_CHUNK, 1, GDN_CHUNK)
        oc_ctx, s_fin = _gdn(cqkv, c_conv_w, gcol, grow3, cz, wts["ong"], state_c, l, latent=False)
        (oc_lat,) = _gdn(cqkv, c_conv_w, gcol, grow3, cz, wts["ong"], state_c, l, latent=True)

        branches = (oa_ctx, oa_lat, ob_ctx, ob_lat, oc_ctx, oc_lat)
        x_mid = _merge(xs, branches, mod, wts, l)
        ffn = functools.partial(_ffn, x_mid, mod, wts, fg, l)
        if l < DEPTH - 1:
            xs = (ffn(tile0=0, n_tiles=N_TILES, final=False),)
        else:
            xs = (ffn(tile0=0, n_tiles=N_CTX_TILES, final=True),
                  ffn(tile0=N_CTX_TILES, n_tiles=N_TILES - N_CTX_TILES, final=True))

        ak_l.append(ak[:T_CTX].reshape(BATCH, SEQ, A_KV_HEADS, A_HEAD_DIM))
        av_l.append(av[:T_CTX].reshape(BATCH, SEQ, A_KV_HEADS, A_HEAD_DIM))
        ckv_l.append(ckv[:T_CTX].reshape(BATCH, SEQ, B_KV_LORA))
        kr_l.append(small[:T_CTX, :B_ROPE].reshape(BATCH, SEQ, B_ROPE))
        sc_l.append(s_fin)

    y_prompt = xs[0].reshape(BATCH, SEQ, D_MODEL)
    y_sample = xs[1].reshape(DEC_BATCH, DEC_SEQ, D_MODEL)
    return (y_prompt, y_sample, jnp.stack(ak_l, axis=1), jnp.stack(av_l, axis=1), jnp.stack(ckv_l, axis=1),
            jnp.stack(kr_l, axis=1), jnp.stack(sc_l, axis=1))
```

```python
import functools
import math

import jax
import jax.numpy as jnp
import numpy as np
from jax import lax
from jax.experimental import pallas as pl
from jax.experimental.pallas import tpu as pltpu

F32 = jnp.float32
BF16 = jnp.bfloat16

D_MODEL = 1024
BATCH = 16
SEQ = 256
DEPTH = 2
DEC_BATCH = 4
DEC_SEQ = 2048
PAST_LEN = 256
GRID_W = 64
ROPE_BASE = 10000.0
EPS = 1e-6
NEG_INF = -1e30

A_HEADS = 8
A_KV_HEADS = 2
A_GROUP = 4
A_HEAD_DIM = 64
A_WINDOW = 128
A_SCALE = A_HEAD_DIM ** -0.5
B_HEADS = 8
B_Q_LORA = 384
B_KV_LORA = 256
B_NOPE = 64
B_ROPE = 32
B_V = 64
MLA_SCALE = (B_NOPE + B_ROPE) ** -0.5
LOG2E = math.log2(math.e)
C_HEADS = 4
C_DK = 128
C_DV = 128
D_FF = 4 * D_MODEL
BRANCH_W = 512

LANES = 128

T_CTX = BATCH * SEQ
T_LAT = DEC_BATCH * DEC_SEQ
T_ALL = T_CTX + T_LAT
TM = 512
N_CTX_TILES = T_CTX // TM
LAT_TILES_PER_SEQ = DEC_SEQ // TM
N_TILES = T_ALL // TM
N_GROUPS = 8
GDN_CHUNK = 128
GDN_PREP_CHAINS = 16
GDN_HEADS_PER_STEP_CTX = 4
GDN_HEADS_PER_STEP_LAT = 2

W_IN_SPLITS = (512, 128, 128, B_Q_LORA, B_KV_LORA, B_ROPE, 1536, 8, 8, 512, 3072)
O_AQ, O_AK, O_AV, O_BCQ, O_BCKV, O_SMALL, W_HEAD_COLS = 0, 512, 640, 768, 1152, 1408, 1536
W_TAIL_COLS = 512 + 3 * D_MODEL
SM_KR, SM_A, SM_B = 0, 32, 40

VMEM_LIMIT = 56 * 1024 * 1024


def _cparams(*sem):
    return pltpu.CompilerParams(dimension_semantics=sem, vmem_limit_bytes=VMEM_LIMIT)


def _resident(shape):
    nd = len(shape)
    return pl.BlockSpec(shape, lambda *_: (0,) * nd, pipeline_mode=pl.Buffered(1))


def _resident_layer(shape, layer):
    nd = len(shape)
    return pl.BlockSpec((None,) + tuple(shape), lambda *_: (layer,) + (0,) * nd, pipeline_mode=pl.Buffered(1))


def _group_of_tile(i):
    return jnp.where(i < N_CTX_TILES, 0, 1 + (i - N_CTX_TILES) // LAT_TILES_PER_SEQ)


def _rope_block_of_tile(i):
    return jnp.where(i < N_CTX_TILES, 0, 1 + (i - N_CTX_TILES) % LAT_TILES_PER_SEQ)


def _mod_spec(layer, tile0=0):
    return pl.BlockSpec((None, 1, 6, D_MODEL), lambda i: (layer, _group_of_tile(tile0 + i), 0, 0))


def _stream_specs(width, n_arrays, tile0=0):
    if n_arrays == 1:
        return [pl.BlockSpec((TM, width), lambda i: (tile0 + i, 0))]
    return [pl.BlockSpec((TM, width), lambda i: (jnp.minimum(tile0 + i, N_CTX_TILES - 1), 0)),
            pl.BlockSpec((TM, width), lambda i: (jnp.maximum(tile0 + i - N_CTX_TILES, 0), 0))]


def _pick_stream(refs, tile0=0):
    if len(refs) == 1:
        return refs[0][...]
    return jnp.where(tile0 + pl.program_id(0) < N_CTX_TILES, refs[0][...], refs[1][...])


def _rms(x):
    return x * lax.rsqrt(jnp.mean(x * x, axis=-1, keepdims=True) + EPS)


def _sigmoid(x):
    return 0.5 * jnp.tanh(0.5 * x) + 0.5


def _silu(x):
    h = 0.5 * x
    return h * jnp.tanh(h) + h


def _softplus(x):
    return jnp.maximum(x, 0.0) + jnp.log(1.0 + jnp.exp(-jnp.abs(x)))


def _dot(a, b):
    return jnp.dot(a.astype(BF16), b.astype(BF16), preferred_element_type=F32)


def _dot_nt(a, b):
    return lax.dot_general(a.astype(BF16), b.astype(BF16), (((1,), (1,)), ((), ())),
                           preferred_element_type=F32)


def _swap_groups(x, q):
    w = x.shape[1]
    lane = lax.broadcasted_iota(jnp.int32, x.shape, 1)
    even = ((lane // q) % 2) == 0
    return jnp.where(even, pltpu.roll(x, w - q, 1), pltpu.roll(x, q, 1))


def _rope(x, cos, sin, q):
    return x * cos + _swap_groups(x, q) * sin


ADA_TN = 1536


def _adaln_kernel(c_ref, w_ref, b_ref, o_ref):
    c = c_ref[...]
    o_ref[0] = jnp.dot(_silu(c), w_ref[0], preferred_element_type=F32) + b_ref[0]


def _adaln(cond8, w_ada, b_ada):
    n = 6 * D_MODEL
    return pl.pallas_call(
        _adaln_kernel,
        grid=(DEPTH, n // ADA_TN),
        in_specs=[
            pl.BlockSpec((N_GROUPS, D_MODEL), lambda l, j: (0, 0)),
            pl.BlockSpec((1, D_MODEL, ADA_TN), lambda l, j: (l, 0, j)),
            pl.BlockSpec((1, 1, ADA_TN), lambda l, j: (l, 0, j)),
        ],
        out_specs=pl.BlockSpec((1, N_GROUPS, ADA_TN), lambda l, j: (l, 0, j)),
        out_shape=jax.ShapeDtypeStruct((DEPTH, N_GROUPS, n), F32),
        compiler_params=_cparams("arbitrary", "arbitrary"),
        name="adaln",
    )(cond8, w_ada, b_ada.reshape(DEPTH, 1, n))


def _chunk_scan(g, forward):
    n = g.shape[0]
    pos = lax.broadcasted_iota(jnp.int32, g.shape, 0) % GDN_CHUNK
    s = 1
    while s < GDN_CHUNK:
        if forward:
            g = g + jnp.where(pos >= s, pltpu.roll(g, s, 0), 0.0)
        else:
            g = g + jnp.where(pos < GDN_CHUNK - s, pltpu.roll(g, n - s, 0), 0.0)
        s *= 2
    return g


def _inproj_kernel(*refs, n_x):
    x_refs, refs = refs[:n_x], refs[n_x:]
    (mod_ref, n1g_ref, w_head_ref, w_cqkv_ref, w_cz_ref,
     ca_ref, sa_ref, cs_ref, ss_ref, cq_ref, sq_ref,
     alog_ref, dtb_ref, gq_ref, gkv_ref, wuq_ref, wukk_ref, pkr_ref, wukv_ref,
     aq_ref, ak_ref, av_ref, bq_ref, bk_ref, bv_ref, ckv_ref, small_ref,
     cqkv_ref, gcol_ref, grow_ref, cz_ref) = refs
    mod = mod_ref[0]
    h = _rms(_pick_stream(x_refs)) * n1g_ref[...] * (1.0 + mod[1:2]) + mod[0:1]
    hb = h.astype(BF16)

    def proj(w_ref, lo, hi):
        return jnp.dot(hb, w_ref[:, lo:hi], preferred_element_type=F32)

    ca, sa = ca_ref[...], sa_ref[...]
    a_all = proj(w_head_ref, O_AQ, O_BCQ)
    for c in range(4):
        q = a_all[:, c * LANES:(c + 1) * LANES]
        aq_ref[:, c * LANES:(c + 1) * LANES] = (_rope(q, ca, sa, 16) * (A_SCALE * LOG2E)).astype(BF16)
    ak_ref[...] = _rope(a_all[:, O_AK:O_AV], ca, sa, 16)
    av_ref[...] = a_all[:, O_AV:O_BCQ]

    b_all = proj(w_head_ref, O_BCQ, W_HEAD_COLS)
    cq = (_rms(b_all[:, :B_Q_LORA]) * gq_ref[...]).astype(BF16)
    cqc, sqc = cq_ref[...], sq_ref[...]
    q_all = jnp.dot(cq, wuq_ref[...], preferred_element_type=F32)
    for hh in range(B_HEADS):
        qh = q_all[:, hh * LANES:(hh + 1) * LANES]
        bq_ref[:, hh * LANES:(hh + 1) * LANES] = (_rope(qh, cqc, sqc, 8) * (MLA_SCALE * LOG2E)).astype(BF16)

    ckv = _rms(b_all[:, B_Q_LORA:B_Q_LORA + B_KV_LORA]) * gkv_ref[...]
    ckv_ref[...] = ckv
    ckvb = ckv.astype(BF16)
    small = b_all[:, O_SMALL - O_BCQ:]
    small_r = _rope(small, cs_ref[...], ss_ref[...], 8)
    small_ref[...] = small_r
    krb = small_r.astype(BF16)
    bk_ref[...] = (jnp.dot(ckvb, wukk_ref[...], preferred_element_type=F32)
                   + jnp.dot(krb, pkr_ref[...], preferred_element_type=F32)).astype(BF16)
    bv_ref[...] = jnp.dot(ckvb, wukv_ref[...], preferred_element_type=F32).astype(BF16)

    for c in range(3):
        cqkv_ref[:, c * 512:(c + 1) * 512] = proj(w_cqkv_ref, c * 512, (c + 1) * 512)
    cz_ref[...] = proj(w_cz_ref, 0, 512)

    graw = -jnp.exp(alog_ref[...]) * _softplus(small + dtb_ref[...])
    lane = lax.broadcasted_iota(jnp.int32, small.shape, 1)
    gcol = jnp.where(lane < SM_A + C_HEADS, _chunk_scan(graw, True), _chunk_scan(graw, False))
    gcol = jnp.where((lane >= SM_A) & (lane < SM_B), gcol,
                     jnp.where((lane >= SM_B) & (lane < SM_B + 2 * C_HEADS), _sigmoid(small), 0.0))
    gcol_ref[...] = gcol
    grow_ref[...] = gcol.T[SM_A:SM_A + 16, :]


def _inproj(xs, mod, wts, tabs, layer):
    tile = lambda w: pl.BlockSpec((TM, w), lambda i: (i, 0))
    rope_spec = pl.BlockSpec((TM, LANES), lambda i: (_rope_block_of_tile(i), 0))
    res = lambda shape: _resident_layer(shape, layer)
    out_shapes = [
        jax.ShapeDtypeStruct((T_ALL, 512), BF16),
        jax.ShapeDtypeStruct((T_ALL, 128), F32),
        jax.ShapeDtypeStruct((T_ALL, 128), F32),
        jax.ShapeDtypeStruct((T_ALL, 1024), BF16),
        jax.ShapeDtypeStruct((T_ALL, 1024), BF16),
        jax.ShapeDtypeStruct((T_ALL, 512), BF16),
        jax.ShapeDtypeStruct((T_ALL, 256), F32),
        jax.ShapeDtypeStruct((T_ALL, 128), F32),
        jax.ShapeDtypeStruct((T_ALL, 1536), F32),
        jax.ShapeDtypeStruct((T_ALL, 128), F32),
        jax.ShapeDtypeStruct((16, T_ALL), F32),
        jax.ShapeDtypeStruct((T_ALL, 512), F32),
    ]
    out_specs = [tile(512), tile(128), tile(128), tile(1024), tile(1024), tile(512), tile(256), tile(128),
                 tile(1536), tile(128), pl.BlockSpec((16, TM), lambda i: (0, i)), tile(512)]
    in_specs = _stream_specs(D_MODEL, len(xs)) + [
        _mod_spec(layer),
        res((1, D_MODEL)),
        res((D_MODEL, W_HEAD_COLS)), res((D_MODEL, 1536)),
        pl.BlockSpec((None, D_MODEL, 512), lambda i: (layer, 0, 0), pipeline_mode=pl.Buffered(1)),
        rope_spec, rope_spec, rope_spec, rope_spec, rope_spec, rope_spec,
        res((1, LANES)), res((1, LANES)),
        res((1, B_Q_LORA)), res((1, B_KV_LORA)),
        res((B_Q_LORA, 1024)), res((B_KV_LORA, 1024)), _resident((LANES, 1024)),
        res((B_KV_LORA, 512)),
    ]
    return pl.pallas_call(
        functools.partial(_inproj_kernel, n_x=len(xs)),
        grid=(N_TILES,),
        in_specs=in_specs,
        out_specs=out_specs,
        out_shape=out_shapes,
        compiler_params=_cparams("arbitrary"),
        name="inproj",
    )(*xs, mod, wts["n1g"], wts["w_head"], wts["w_cqkv"], wts["w_tail"],
      tabs["ca"], tabs["sa"], tabs["cs"], tabs["ss"], tabs["cq"], tabs["sq"],
      wts["alog"], wts["dtb"], wts["gq"], wts["gkv"], wts["wuq"], wts["wukk"], wts["pkr"], wts["wukv"])


def _mla_cache_kernel(ckv_ref, kr_ref, wukk_ref, pkr_ref, wukv_ref, k_ref, v_ref):
    ckvb = ckv_ref[...].astype(BF16)
    krb = kr_ref[...].astype(BF16)
    k_ref[...] = (jnp.dot(ckvb, wukk_ref[...], preferred_element_type=F32)
                  + jnp.dot(krb, pkr_ref[...], preferred_element_type=F32)).astype(BF16)
    v_ref[...] = jnp.dot(ckvb, wukv_ref[...], preferred_element_type=F32).astype(BF16)


def _mla_cache(cache_ckv, cache_kr, wts, layer):
    n = DEC_BATCH * PAST_LEN
    res = lambda shape: _resident_layer(shape, layer)
    return pl.pallas_call(
        _mla_cache_kernel,
        grid=(DEC_BATCH,),
        in_specs=[pl.BlockSpec((None, None, PAST_LEN, B_KV_LORA), lambda i: (i, layer, 0, 0)),
                  pl.BlockSpec((None, None, PAST_LEN, LANES), lambda i: (i, layer, 0, 0)),
                  res((B_KV_LORA, 1024)), _resident((LANES, 1024)), res((B_KV_LORA, 512))],
        out_specs=[pl.BlockSpec((PAST_LEN, 1024), lambda i: (i, 0)),
                   pl.BlockSpec((PAST_LEN, 512), lambda i: (i, 0))],
        out_shape=[jax.ShapeDtypeStruct((n, 1024), BF16), jax.ShapeDtypeStruct((n, 512), BF16)],
        compiler_params=_cparams("arbitrary"),
        name="mla_cache",
    )(cache_ckv, cache_kr, wts["wukk"], wts["pkr"], wts["wukv"])


TQ = 256
A_WIN_KEYS = TQ + 2 * A_WINDOW
_NT = (((1,), (1,)), ((), ()))


def _softmax_pv(scores, values, extra_logit=None):
    m = scores[0].max(axis=-1, keepdims=True)
    for s in scores[1:]:
        m = jnp.maximum(m, s.max(axis=-1, keepdims=True))
    if extra_logit is not None:
        m = jnp.maximum(m, extra_logit)
    den = None
    acc = None
    for s, v in zip(scores, values):
        e = jnp.exp2(s - m)
        d = e.sum(axis=-1, keepdims=True)
        den = d if den is None else den + d
        pv = jnp.dot(e.astype(BF16), v, preferred_element_type=F32)
        acc = pv if acc is None else acc + pv
    if extra_logit is not None:
        den = den + jnp.exp2(extra_logit - m)
    return acc / den


def _attn_a_kernel(sink_ref, q_ref, k1_ref, v1_ref, *rest, latent, layer):
    if latent:
        k2_ref, v2_ref, o_ref = rest
        qt = pl.program_id(1)
        t0 = qt * TQ
        start = pl.multiple_of(jnp.clip(t0 - A_WINDOW, 0, DEC_SEQ - A_WIN_KEYS), A_WINDOW)
        k1 = k1_ref[pl.ds(start, A_WIN_KEYS), :]
        v1 = v1_ref[pl.ds(start, A_WIN_KEYS), :]
        qpos = t0 + lax.broadcasted_iota(jnp.int32, (TQ, A_WIN_KEYS), 0)
        kpos = start + lax.broadcasted_iota(jnp.int32, (TQ, A_WIN_KEYS), 1)
        valid = jnp.abs(qpos - kpos) <= A_WINDOW
    else:
        (o_ref,) = rest
        k1 = k1_ref[...]
        v1 = v1_ref[...]
    both = lambda x: (x.astype(BF16), pltpu.roll(x, A_HEAD_DIM, 1).astype(BF16))
    k1, v1 = both(k1), both(v1)
    if latent:
        k2, v2 = both(k2_ref[...]), both(v2_ref[...])
    lane = lax.broadcasted_iota(jnp.int32, (TQ, LANES), 1)
    low = lane < A_HEAD_DIM
    top = lax.broadcasted_iota(jnp.int32, (2 * TQ, 1), 0) < TQ
    if latent:
        valid2 = jnp.concatenate([valid, valid], axis=0)
    def scores(hk, p):
        sel = 0 if p == hk else 1
        keep = low if p == 0 else ~low
        qz = jnp.concatenate([jnp.where(keep, q_ref[:, c * LANES:(c + 1) * LANES], jnp.zeros((TQ, LANES), BF16))
                              for c in (2 * hk, 2 * hk + 1)], axis=0)
        s = [lax.dot_general(qz, k1[sel], _NT, preferred_element_type=F32)]
        if latent:
            s = [jnp.where(valid2, s[0], NEG_INF), lax.dot_general(qz, k2[sel], _NT, preferred_element_type=F32)]
        return s

    work = [(hk, p) for hk in range(A_KV_HEADS) for p in range(2)]
    nxt = scores(*work[0])
    outs = {}
    for idx, (hk, p) in enumerate(work):
        cur = nxt
        if idx + 1 < len(work):
            nxt = scores(*work[idx + 1])
        sel = 0 if p == hk else 1
        sink = jnp.where(top, sink_ref[layer, 4 * hk + p], sink_ref[layer, 4 * hk + 2 + p]) * LOG2E
        values = [v1[sel], v2[sel]] if latent else [v1[sel]]
        outs[(hk, p)] = _softmax_pv(cur, values, sink)
        if p == 1:
            for i, c in enumerate((2 * hk, 2 * hk + 1)):
                rows = slice(i * TQ, (i + 1) * TQ)
                o_ref[:, c * LANES:(c + 1) * LANES] = jnp.where(low, outs[(hk, 0)][rows],
                                                                outs[(hk, 1)][rows]).astype(BF16)


def _attn_a_ctx(sink, aq, ak, av, layer):
    nb = T_CTX // TQ
    return pl.pallas_call(
        functools.partial(_attn_a_kernel, latent=False, layer=layer),
        grid=(nb,),
        in_specs=[pl.BlockSpec(memory_space=pltpu.SMEM),
                  pl.BlockSpec((TQ, 512), lambda b: (b, 0)),
                  pl.BlockSpec((SEQ, LANES), lambda b: (b, 0)),
                  pl.BlockSpec((SEQ, LANES), lambda b: (b, 0))],
        out_specs=pl.BlockSpec((TQ, 512), lambda b: (b, 0)),
        out_shape=jax.ShapeDtypeStruct((T_CTX, 512), BF16),
        compiler_params=_cparams("arbitrary"),
        name="attn_a_ctx",
    )(sink, aq, ak, av)


def _attn_a_lat(sink, aq, ak, av, ck, cv, layer):
    nq = DEC_SEQ // TQ
    ctx_tiles = T_CTX // TQ
    seq_blocks = T_CTX // DEC_SEQ
    cache_spec = pl.BlockSpec((None, None, PAST_LEN, LANES), lambda b, i: (b, layer, 0, 0))
    return pl.pallas_call(
        functools.partial(_attn_a_kernel, latent=True, layer=layer),
        grid=(DEC_BATCH, nq),
        in_specs=[pl.BlockSpec(memory_space=pltpu.SMEM),
                  pl.BlockSpec((TQ, 512), lambda b, i: (ctx_tiles + b * nq + i, 0)),
                  pl.BlockSpec((DEC_SEQ, LANES), lambda b, i: (seq_blocks + b, 0)),
                  pl.BlockSpec((DEC_SEQ, LANES), lambda b, i: (seq_blocks + b, 0)),
                  cache_spec, cache_spec],
        out_specs=pl.BlockSpec((TQ, 512), lambda b, i: (b * nq + i, 0)),
        out_shape=jax.ShapeDtypeStruct((T_LAT, 512), BF16),
        compiler_params=_cparams("arbitrary", "arbitrary"),
        name="attn_a_lat",
    )(sink, aq, ak, av, ck, cv)


MLA_TQ_LAT = 512


def _mla_kernel(q_ref, k1_ref, v1_ref, *rest, latent):
    if latent:
        k2_ref, v2_ref, o_ref = rest
    else:
        (o_ref,) = rest
    lane = lax.broadcasted_iota(jnp.int32, (q_ref.shape[0], LANES), 1)
    low = lane < B_V
    def scores(hh):
        hs = slice(hh * LANES, (hh + 1) * LANES)
        qh = q_ref[:, hs]
        s = [lax.dot_general(qh, k1_ref[:, hs], _NT, preferred_element_type=F32)]
        if latent:
            s.append(lax.dot_general(qh, k2_ref[:, hs], _NT, preferred_element_type=F32))
        return s

    nxt = scores(0)
    for j in range(B_HEADS // 2):
        vs = slice((j // 2) * 2 * LANES, (j // 2 + 1) * 2 * LANES)
        half = slice((j % 2) * LANES, (j % 2 + 1) * LANES)
        values = [v1_ref[:, vs], v2_ref[:, vs]] if latent else [v1_ref[:, vs]]
        outs = []
        for sub in range(2):
            hh = 2 * j + sub
            cur = nxt
            if hh + 1 < B_HEADS:
                nxt = scores(hh + 1)
            outs.append(_softmax_pv(cur, values)[:, half])
        o_ref[:, j * LANES:(j + 1) * LANES] = jnp.where(low, outs[0], outs[1]).astype(BF16)


def _mla_ctx(bq, bk, bv):
    nb = T_CTX // TQ
    return pl.pallas_call(
        functools.partial(_mla_kernel, latent=False),
        grid=(nb,),
        in_specs=[pl.BlockSpec((TQ, 1024), lambda b: (b, 0)),
                  pl.BlockSpec((SEQ, 1024), lambda b: (b, 0)),
                  pl.BlockSpec((SEQ, 512), lambda b: (b, 0))],
        out_specs=pl.BlockSpec((TQ, 512), lambda b: (b, 0)),
        out_shape=jax.ShapeDtypeStruct((T_CTX, 512), BF16),
        compiler_params=_cparams("arbitrary"),
        name="mla_ctx",
    )(bq, bk, bv)


def _mla_lat(bq, bk, bv, k2, v2):
    tq = MLA_TQ_LAT
    nq = DEC_SEQ // tq
    ctx_tiles = T_CTX // tq
    seq_blocks = T_CTX // DEC_SEQ
    return pl.pallas_call(
        functools.partial(_mla_kernel, latent=True),
        grid=(DEC_BATCH, nq),
        in_specs=[pl.BlockSpec((tq, 1024), lambda b, i: (ctx_tiles + b * nq + i, 0)),
                  pl.BlockSpec((DEC_SEQ, 1024), lambda b, i: (seq_blocks + b, 0)),
                  pl.BlockSpec((DEC_SEQ, 512), lambda b, i: (seq_blocks + b, 0)),
                  pl.BlockSpec((PAST_LEN, 1024), lambda b, i: (b, 0)),
                  pl.BlockSpec((PAST_LEN, 512), lambda b, i: (b, 0))],
        out_specs=pl.BlockSpec((tq, 512), lambda b, i: (b * nq + i, 0)),
        out_shape=jax.ShapeDtypeStruct((T_LAT, 512), BF16),
        compiler_params=_cparams("arbitrary", "arbitrary"),
        name="mla_lat",
    )(bq, bk, bv, k2, v2)


def _gdn_kernel(xq_ref, xk_ref, xv_ref, wq_ref, wk_ref, wv_ref, gcol_ref, grow_ref, cz_ref, ong_ref, *rest,
                n, hps, has_s0, emit_state):
    rest = list(rest)
    s0_ref = rest.pop(0) if has_s0 else None
    oc_ref = rest.pop(0)
    sfin_ref = rest.pop(0) if emit_state else None
    q_s, k_s, v_s, u_s, w_s, qk_s, qd_s, kdt_s, ge_s, st_s, o_s = rest
    nc = n // GDN_CHUNK
    head0 = pl.program_id(1) * hps
    C = GDN_CHUNK
    width = hps * LANES
    hl = lambda hh: slice(hh * LANES, (hh + 1) * LANES)
    chains = [(d, hh) for d in range(2) for hh in range(hps)]

    row8 = lax.broadcasted_iota(jnp.int32, (8, width), 0)

    def conv_silu(x_ref, w_ref):
        x = x_ref[...]
        w = w_ref[...]
        y = pltpu.roll(x, 1, 0) * w[0:1] + x * w[1:2] + pltpu.roll(x, n - 1, 0) * w[2:3]
        top = y[0:8] - jnp.where(row8 == 0, x[n - 1:n] * w[0:1], 0.0)
        bot = y[n - 8:n] - jnp.where(row8 == 7, x[0:1] * w[2:3], 0.0)
        return _silu(jnp.concatenate([top, y[8:n - 8], bot], axis=0))

    def l2n(x):
        return x * lax.rsqrt(jnp.sum(x * x, axis=-1, keepdims=True) + EPS)

    qc = conv_silu(xq_ref, wq_ref)
    kc = conv_silu(xk_ref, wk_ref)
    for hh in range(hps):
        q_s[:, hl(hh)] = l2n(qc[:, hl(hh)]) * (C_DK ** -0.5)
        k_s[:, hl(hh)] = l2n(kc[:, hl(hh)])
    v_s[...] = conv_silu(xv_ref, wv_ref)
    o_s[...] = jnp.zeros_like(o_s)

    lane_c = lax.broadcasted_iota(jnp.int32, (C, LANES), 1)
    ri = lax.broadcasted_iota(jnp.int32, (C, C), 0)
    ci = lax.broadcasted_iota(jnp.int32, (C, C), 1)
    eye = (ri == ci).astype(F32)
    same_block = {}
    s = 2
    while s <= C:
        same_block[s] = (ri // s) == (ci // s)
        s *= 2

    def prep(it, carry):
        items = []
        for gi in range(group):
            r = it * group + gi
            rows = pl.ds(pl.multiple_of(r * C, C), C)
            gtile = gcol_ref[rows, :]
            for hh in range(hps):
                q = q_s[rows, hl(hh)]
                k = k_s[rows, hl(hh)]
                v = v_s[rows, hl(hh)]
                qk_raw = _dot_nt(q, k)
                for d in range(2):
                    slot = d * hps + hh
                    j = d * C_HEADS + head0 + hh
                    gc = jnp.sum(jnp.where(lane_c == SM_A + j, gtile, 0.0), axis=1, keepdims=True)
                    beta = jnp.sum(jnp.where(lane_c == SM_B + j, gtile, 0.0), axis=1, keepdims=True)
                    grow = grow_ref[j, r]
                    incl = (ri >= ci) if d == 0 else (ri <= ci)
                    strict = (ri > ci) if d == 0 else (ri < ci)
                    decay = jnp.where(incl, jnp.exp(jnp.where(incl, gc - grow, 0.0)), 0.0)
                    kb = k * beta
                    a = jnp.where(strict, _dot_nt(kb, k) * decay, 0.0)
                    g_end = gc[C - 1:C, :] if d == 0 else gc[0:1, :]
                    egc = jnp.exp(gc)
                    qk_s[slot, rows, :] = (qk_raw * decay).astype(BF16)
                    qd_s[slot, rows, :] = (q * egc).astype(BF16)
                    kdt_s[slot, rows, :] = (k * jnp.exp(g_end - gc)).T.astype(BF16)
                    ge_s[slot, r] = jnp.broadcast_to(jnp.exp(g_end), (8, LANES))
                    items.append((slot, rows, a, jnp.concatenate([v * beta, kb * egc], axis=1), d))
        ts = [eye - jnp.where(same_block[2], a, 0.0) for (_, _, a, _, _) in items]
        s = 2
        while s < C:
            level = same_block[2 * s] & ~same_block[s]
            es = [jnp.where(level, a, 0.0) for (_, _, a, _, _) in items]
            if s < 8:
                xs = [_dot(t, e) for t, e in zip(ts, es)]
                ts = [t - _dot(x, t) for t, x in zip(ts, xs)]
            else:
                def active(t, d):
                    return jnp.concatenate([t[b * s:(b + 1) * s] for b in range(1 - d, C // s, 2)], axis=0)

                def merged(t, upd, d):
                    return jnp.concatenate(
                        [upd[(b // 2) * s:(b // 2 + 1) * s] if b % 2 == 1 - d else t[b * s:(b + 1) * s]
                         for b in range(C // s)], axis=0)

                halves = [active(t, it_[4]) for t, it_ in zip(ts, items)]
                xs = [_dot(h, e) for h, e in zip(halves, es)]
                upds = [h - _dot(x, t) for h, x, t in zip(halves, xs, ts)]
                ts = [merged(t, u, it_[4]) for t, u, it_ in zip(ts, upds, items)]
            s *= 2
        for t, (slot, rows, _, rhs, _) in zip(ts, items):
            uw = _dot(t, rhs)
            u_s[slot, rows, :] = uw[:, :C_DV]
            w_s[slot, rows, :] = uw[:, C_DV:].astype(BF16)
        return carry

    group = max(1, min(nc, GDN_PREP_CHAINS // (2 * hps)))
    lax.fori_loop(0, nc // group, prep, 0)

    for d, hh in chains:
        if has_s0:
            st_s[d * hps + hh] = s0_ref[d, hh]
        else:
            st_s[d * hps + hh] = jnp.zeros((C_DK, C_DV), F32)

    def step(i, carry):
        rows = [pl.ds(pl.multiple_of(i * C, C), C), pl.ds(pl.multiple_of((nc - 1 - i) * C, C), C)]
        chunk = [i, nc - 1 - i]
        slots = [d * hps + hh for d, hh in chains]
        st = [st_s[c] for c in slots]
        stb = [s_.astype(BF16) for s_ in st]
        ws = [jnp.dot(w_s[c, rows[d], :], sb, preferred_element_type=F32)
              for c, (d, _), sb in zip(slots, chains, stb)]
        vnb = [(u_s[c, rows[d], :] - x).astype(BF16) for c, (d, _), x in zip(slots, chains, ws)]
        for c, (d, hh), sb, vb, s_ in zip(slots, chains, stb, vnb, st):
            o = (jnp.dot(qd_s[c, rows[d], :], sb, preferred_element_type=F32)
                 + jnp.dot(qk_s[c, rows[d], :], vb, preferred_element_type=F32))
            o_s[rows[d], hl(hh)] += o
            st_s[c] = s_ * ge_s[c, chunk[d]][0:1, :] + jnp.dot(kdt_s[c, rows[d], :], vb,
                                                                preferred_element_type=F32)
        return carry

    lax.fori_loop(0, nc, step, 0)
    if emit_state:
        for d, hh in chains:
            sfin_ref[0, d, hh] = st_s[d * hps + hh]

    for hh in range(hps):
        o = o_s[:, hl(hh)]
        oc_ref[:, hl(hh)] = (_rms(o) * ong_ref[...] * _silu(cz_ref[:, hl(hh)])).astype(BF16)


def _gdn(cqkv, conv_w, gcol, grow3, cz, ong, state_c, layer, *, latent):
    n = DEC_SEQ if latent else SEQ
    nseq = DEC_BATCH if latent else BATCH
    hps = GDN_HEADS_PER_STEP_LAT if latent else GDN_HEADS_PER_STEP_CTX
    seq0 = (T_CTX // n) if latent else 0
    nc = n // GDN_CHUNK
    width = hps * LANES
    sect = C_HEADS // hps
    col = lambda base: pl.BlockSpec((n, width), lambda b, h: (seq0 + b, base + h))
    wcol = lambda base: pl.BlockSpec((None, 3, width), lambda b, h: (layer, 0, base + h))
    in_specs = [col(0), col(sect), col(2 * sect), wcol(0), wcol(sect), wcol(2 * sect),
                pl.BlockSpec((n, LANES), lambda b, h: (seq0 + b, 0)),
                pl.BlockSpec((16, nc, 1, LANES), lambda b, h: (0, seq0 + b, 0, 0)),
                pl.BlockSpec((n, width), lambda b, h: (seq0 + b, h)),
                pl.BlockSpec((None, 1, LANES), lambda b, h: (layer, 0, 0))]
    args = [cqkv, cqkv, cqkv, conv_w, conv_w, conv_w, gcol, grow3, cz, ong]
    out_specs = [pl.BlockSpec((n, width), lambda b, h: (b, h))]
    out_shape = [jax.ShapeDtypeStruct((nseq * n, 512), BF16)]
    if latent:
        in_specs.append(pl.BlockSpec((None, None, 2, hps, C_DK, C_DV), lambda b, h: (b, layer, 0, h, 0, 0)))
        args.append(state_c)
    else:
        out_specs.append(pl.BlockSpec((1, 2, hps, C_DK, C_DV), lambda b, h: (b, 0, h, 0, 0)))
        out_shape.append(jax.ShapeDtypeStruct((nseq, 2, C_HEADS, C_DK, C_DV), F32))
    nch = 2 * hps
    scratch = [pltpu.VMEM((n, width), F32), pltpu.VMEM((n, width), F32), pltpu.VMEM((n, width), F32),
               pltpu.VMEM((nch, n, LANES), F32), pltpu.VMEM((nch, n, LANES), BF16),
               pltpu.VMEM((nch, n, LANES), BF16), pltpu.VMEM((nch, n, LANES), BF16),
               pltpu.VMEM((nch, n, LANES), BF16), pltpu.VMEM((nch, nc, 8, LANES), F32),
               pltpu.VMEM((nch, C_DK, C_DV), F32), pltpu.VMEM((n, width), F32)]
    res = pl.pallas_call(
        functools.partial(_gdn_kernel, n=n, hps=hps, has_s0=latent, emit_state=not latent),
        grid=(nseq, C_HEADS // hps),
        in_specs=in_specs,
        out_specs=out_specs,
        out_shape=out_shape,
        scratch_shapes=scratch,
        compiler_params=_cparams("arbitrary", "arbitrary"),
        name="gdn_lat" if latent else "gdn_ctx",
    )(*args)
    return res


def _merge_kernel(*refs, n_x):
    x_refs, refs = refs[:n_x], refs[n_x:]
    branch_refs, refs = refs[:6], refs[6:]
    mod_ref, n1g_ref, wtail_ref, wbr_ref, wout_ref, o_ref = refs
    mod = mod_ref[0]
    ys = [jnp.dot(_pick_stream(branch_refs[2 * j:2 * j + 2]), wbr_ref[j], preferred_element_type=F32)
          for j in range(3)]
    x = _pick_stream(x_refs)
    hb = (_rms(x) * n1g_ref[...] * (1.0 + mod[1:2]) + mod[0:1]).astype(BF16)
    z = None
    for j in range(3):
        cols = slice(512 + j * D_MODEL, 512 + (j + 1) * D_MODEL)
        gate = _sigmoid(jnp.dot(hb, wtail_ref[:, cols], preferred_element_type=F32))
        z = gate * ys[j] if z is None else z + gate * ys[j]
    o_ref[...] = x + mod[2:3] * jnp.dot(z.astype(BF16), wout_ref[...], preferred_element_type=F32)


def _merge(xs, branches, mod, wts, layer):
    res = lambda shape: _resident_layer(shape, layer)
    in_specs = _stream_specs(D_MODEL, len(xs))
    for _ in range(3):
        in_specs += _stream_specs(BRANCH_W, 2)
    in_specs += [_mod_spec(layer), res((1, D_MODEL)), res((D_MODEL, W_TAIL_COLS)),
                 res((3, BRANCH_W, D_MODEL)), res((D_MODEL, D_MODEL))]
    return pl.pallas_call(
        functools.partial(_merge_kernel, n_x=len(xs)),
        grid=(N_TILES,),
        in_specs=in_specs,
        out_specs=pl.BlockSpec((TM, D_MODEL), lambda i: (i, 0)),
        out_shape=jax.ShapeDtypeStruct((T_ALL, D_MODEL), F32),
        compiler_params=_cparams("arbitrary"),
        name="merge",
    )(*xs, *branches, mod, wts["n1g"], wts["w_tail"], wts["wbr"], wts["wout"])


FF_CHUNK = 1024


def _ffn_kernel(x_ref, mod_ref, n2g_ref, wff1_ref, wff2_ref, fg_ref, o_ref, *, final):
    mod = mod_ref[0]
    x = x_ref[...]
    hb = (_rms(x) * n2g_ref[...] * (1.0 + mod[4:5]) + mod[3:4]).astype(BF16)
    acc = None
    for c in range(D_FF // FF_CHUNK):
        cols = slice(c * FF_CHUNK, (c + 1) * FF_CHUNK)
        f = jnp.maximum(jnp.dot(hb, wff1_ref[:, cols], preferred_element_type=F32), 0.0)
        p = jnp.dot((f * f).astype(BF16), wff2_ref[cols, :], preferred_element_type=F32)
        acc = p if acc is None else acc + p
    x = x + mod[5:6] * acc
    if final:
        x = _rms(x) * fg_ref[...]
    o_ref[...] = x


def _ffn(x_all, mod, wts, final_g, layer, *, tile0, n_tiles, final):
    res = lambda shape: _resident_layer(shape, layer)
    return pl.pallas_call(
        functools.partial(_ffn_kernel, final=final),
        grid=(n_tiles,),
        in_specs=[pl.BlockSpec((TM, D_MODEL), lambda i: (tile0 + i, 0)), _mod_spec(layer, tile0),
                  res((1, D_MODEL)), res((D_MODEL, D_FF)), res((D_FF, D_MODEL)), _resident((1, D_MODEL))],
        out_specs=pl.BlockSpec((TM, D_MODEL), lambda i: (i, 0)),
        out_shape=jax.ShapeDtypeStruct((n_tiles * TM, D_MODEL), F32),
        compiler_params=_cparams("arbitrary"),
        name="ffn",
    )(x_all, mod, wts["n2g"], wts["wff1"], wts["wff2"], final_g)


def _rope_tables(r):
    quarter = r // 4
    t = np.arange(DEC_SEQ)
    inv_freq = np.power(np.float32(ROPE_BASE), -np.arange(quarter, dtype=np.float32) / np.float32(quarter))
    cs, ss = [], []
    for pos in (t // GRID_W, t % GRID_W):
        ang = pos.astype(np.float32)[:, None] * inv_freq[None, :]
        c, s = np.cos(ang), np.sin(ang)
        cs += [c, c]
        ss += [-s, s]
    return np.concatenate(cs, axis=1), np.concatenate(ss, axis=1)


def _tables():
    def with_identity(c, s):
        one = np.ones((TM, LANES), np.float32)
        return (np.concatenate([one, c], axis=0).astype(np.float32),
                np.concatenate([0.0 * one, s], axis=0).astype(np.float32))

    c64, s64 = _rope_tables(A_HEAD_DIM)
    ca, sa = with_identity(np.tile(c64, (1, 2)), np.tile(s64, (1, 2)))
    c32, s32 = _rope_tables(B_ROPE)
    ones = lambda w: np.ones((DEC_SEQ, w), np.float32)
    zeros = lambda w: np.zeros((DEC_SEQ, w), np.float32)
    cs, ss = with_identity(np.concatenate([c32, ones(96)], axis=1), np.concatenate([s32, zeros(96)], axis=1))
    cq, sq = with_identity(np.concatenate([ones(64), c32, ones(32)], axis=1),
                           np.concatenate([zeros(64), s32, zeros(32)], axis=1))
    return {"ca": ca, "sa": sa, "cs": cs, "ss": ss, "cq": cq, "sq": sq}


def _prep_weights(norm1_g, norm2_g, w_in, b_gq, b_w_uq, b_gkv, b_w_ukv, c_a_log, c_dt_bias, c_onorm_g,
                  w_branch, w_out, w_ff1, w_ff2):
    offs = [0]
    for width in W_IN_SPLITS:
        offs.append(offs[-1] + width)
    piece = lambda i, j: w_in[:, :, offs[i]:offs[j]]
    w_head = jnp.concatenate([piece(0, 6), piece(7, 9), jnp.zeros((DEPTH, D_MODEL, LANES - 48), F32)],
                             axis=2).astype(BF16)
    w_cqkv = piece(6, 7).astype(BF16)
    w_tail = piece(9, 11).astype(BF16)

    pad_lane = lambda v: jnp.zeros((DEPTH, 1, LANES), F32).at[:, 0, SM_A:SM_A + 8].set(v.reshape(DEPTH, 8))
    uq = b_w_uq.reshape(DEPTH, B_Q_LORA, B_HEADS, B_NOPE + B_ROPE)
    uq = jnp.pad(uq, ((0, 0), (0, 0), (0, 0), (0, LANES - B_NOPE - B_ROPE)))
    uq = uq.reshape(DEPTH, B_Q_LORA, 1024).astype(BF16)
    ukv = b_w_ukv.reshape(DEPTH, B_KV_LORA, B_HEADS, B_NOPE + B_V)
    ukk = jnp.pad(ukv[..., :B_NOPE], ((0, 0), (0, 0), (0, 0), (0, LANES - B_NOPE)))
    ukk = ukk.reshape(DEPTH, B_KV_LORA, 1024).astype(BF16)
    ukvv = ukv[..., B_NOPE:].reshape(DEPTH, B_KV_LORA, 512).astype(BF16)
    src = np.arange(LANES)[:, None]
    dst = np.arange(1024)[None, :]
    pkr = jnp.asarray((dst % LANES == B_NOPE + src) & (src < B_ROPE), BF16)

    return {"w_head": w_head, "w_cqkv": w_cqkv, "w_tail": w_tail,
            "alog": pad_lane(c_a_log), "dtb": pad_lane(c_dt_bias),
            "n1g": norm1_g.reshape(DEPTH, 1, D_MODEL), "n2g": norm2_g.reshape(DEPTH, 1, D_MODEL),
            "gq": b_gq.reshape(DEPTH, 1, B_Q_LORA), "gkv": b_gkv.reshape(DEPTH, 1, B_KV_LORA),
            "ong": c_onorm_g.reshape(DEPTH, 1, C_DV),
            "wuq": uq, "wukk": ukk, "pkr": pkr, "wukv": ukvv,
            "wbr": w_branch.astype(BF16), "wout": w_out.astype(BF16), "wff1": w_ff1.astype(BF16), "wff2": w_ff2.astype(BF16)}


def kernel(x_prompt, x_sample, cache_a_k, cache_a_v, cache_b_ckv, cache_b_krope, state_c, c, c_ctx, norm1_g,
           norm2_g, w_ada, b_ada, w_in, a_sink, b_gq, b_w_uq, b_gkv, b_w_ukv, c_conv_w, c_a_log, c_dt_bias,
           c_onorm_g, w_branch, w_out, w_ff1, w_ff2, final_g):
    xs = (x_prompt.reshape(T_CTX, D_MODEL), x_sample.reshape(T_LAT, D_MODEL))
    cond8 = jnp.concatenate([c_ctx[None, :], c, jnp.zeros((N_GROUPS - 1 - DEC_BATCH, D_MODEL), F32)], axis=0)
    mod = _adaln(cond8, w_ada, b_ada).reshape(DEPTH, N_GROUPS, 6, D_MODEL)
    tabs = _tables()
    fg = final_g.reshape(1, D_MODEL)
    wts = _prep_weights(norm1_g, norm2_g, w_in, b_gq, b_w_uq, b_gkv, b_w_ukv, c_a_log, c_dt_bias, c_onorm_g,
                        w_branch, w_out, w_ff1, w_ff2)
    ck = cache_a_k.reshape(DEC_BATCH, DEPTH, PAST_LEN, LANES)
    cv = cache_a_v.reshape(DEC_BATCH, DEPTH, PAST_LEN, LANES)
    cache_kr = jnp.pad(cache_b_krope, ((0, 0), (0, 0), (0, 0), (0, LANES - B_ROPE)))

    ak_l, av_l, ckv_l, kr_l, sc_l = [], [], [], [], []
    for l in range(DEPTH):
        (aq, ak, av, bq, bk, bv, ckv, small, cqkv, gcol, grow, cz) = _inproj(xs, mod, wts, tabs, l)

        oa_ctx = _attn_a_ctx(a_sink, aq, ak, av, l)
        oa_lat = _attn_a_lat(a_sink, aq, ak, av, ck, cv, l)

        k2, v2 = _mla_cache(cache_b_ckv, cache_kr, wts, l)
        ob_ctx = _mla_ctx(bq, bk, bv)
        ob_lat = _mla_lat(bq, bk, bv, k2, v2)

        grow3 = grow.reshape(16, T_ALL // GDN_CHUNK, 1, GDN_CHUNK)
        oc_ctx, s_fin = _gdn(cqkv, c_conv_w, gcol, grow3, cz, wts["ong"], state_c, l, latent=False)
        (oc_lat,) = _gdn(cqkv, c_conv_w, gcol, grow3, cz, wts["ong"], state_c, l, latent=True)

        branches = (oa_ctx, oa_lat, ob_ctx, ob_lat, oc_ctx, oc_lat)
        x_mid = _merge(xs, branches, mod, wts, l)
        ffn = functools.partial(_ffn, x_mid, mod, wts, fg, l)
        if l < DEPTH - 1:
            xs = (ffn(tile0=0, n_tiles=N_TILES, final=False),)
        else:
            xs = (ffn(tile0=0, n_tiles=N_CTX_TILES, final=True),
                  ffn(tile0=N_CTX_TILES, n_tiles=N_TILES - N_CTX_TILES, final=True))

        ak_l.append(ak[:T_CTX].reshape(BATCH, SEQ, A_KV_HEADS, A_HEAD_DIM))
        av_l.append(av[:T_CTX].reshape(BATCH, SEQ, A_KV_HEADS, A_HEAD_DIM))
        ckv_l.append(ckv[:T_CTX].reshape(BATCH, SEQ, B_KV_LORA))
        kr_l.append(small[:T_CTX, :B_ROPE].reshape(BATCH, SEQ, B_ROPE))
        sc_l.append(s_fin)

    y_prompt = xs[0].reshape(BATCH, SEQ, D_MODEL)
    y_sample = xs[1].reshape(DEC_BATCH, DEC_SEQ, D_MODEL)
    return (y_prompt, y_sample, jnp.stack(ak_l, axis=1), jnp.stack(av_l, axis=1), jnp.stack(ckv_l, axis=1),
            jnp.stack(kr_l, axis=1), jnp.stack(sc_l, axis=1))
```
